```python
import math
import jax
import jax.numpy as jnp
from jax import lax
import numpy as np

D_MODEL = 1024
BATCH = 16
SEQ = 2048
DEPTH = 4
DEC_BATCH = 128
DEC_SEQ = 4
PAST_LEN = 8192
PAGE_SIZE = 128

N_MIXERS = 4
N_NSA = (DEPTH + 3) // 4
N_ML = (DEPTH + 2) // 4
N_MLA = (DEPTH + 1) // 4
N_HG = DEPTH // 4
PLE_DIM = 256
ROPE_THETA = 10000.0
NORM_EPS = 1e-6
NEG = -1e30
FORCE_BONUS = 1e4

NSA_HEADS = 16
NSA_KV_HEADS = 2
NSA_HPG = NSA_HEADS // NSA_KV_HEADS
NSA_DH = D_MODEL // NSA_HEADS
CMP_LEN = 32
CMP_STRIDE = 16
CMP_HID = 256
SLC_BLOCK = 64
SLC_TOPN = 16
WINDOW = 512
WIN_QBLK = 128
NSA_GATHER_ROWS = 256

ML_HEADS = 4
ML_DK = 128
ML_DV = D_MODEL // ML_HEADS
ML_CHUNK = 64

MLA_HEADS = 16
MLA_Q_LORA = 512
MLA_KV_LORA = 256
MLA_NOPE = 64
MLA_ROPE = 32
MLA_DV = 64
ATTN_QBLK = 128

HG_HEADS = 8
HG_DK = 128
HG_DV = D_MODEL // HG_HEADS
HG_CHUNK = 64

D_FF = 2816
CONV_W = 3

STATE_NAMES = ('nsa_cmp', 'nsa_slc', 'nsa_win', 'mla', 'ml_C', 'ml_n', 'ml_m', 'hg_S', 'ffn_conv')

kernel_name = 'hybrid_nsa_mlstm_mla_hgrn2_step'


def _rmsnorm(x, g):
    xf = x.astype(jnp.float32)
    y = xf * lax.rsqrt(jnp.mean(xf * xf, axis=-1, keepdims=True) + NORM_EPS)
    return (y * g.astype(jnp.float32)).astype(x.dtype)


def _rope(x, pos):
    half = x.shape[-1] // 2
    inv = jnp.power(ROPE_THETA, -jnp.arange(half, dtype=jnp.float32) / half)
    ang = pos.astype(jnp.float32)[:, None] * inv[None, :]
    cos = jnp.cos(ang)[:, None, :]
    sin = jnp.sin(ang)[:, None, :]
    xf = x.astype(jnp.float32)
    x1, x2 = xf[..., :half], xf[..., half:]
    return jnp.concatenate([x1 * cos - x2 * sin, x1 * sin + x2 * cos], axis=-1).astype(x.dtype)


def _masked_softmax(s, mask):
    s = jnp.where(mask, s, NEG)
    e = jnp.where(mask, jnp.exp(s - jnp.max(s, axis=-1, keepdims=True)), 0.0)
    return e / jnp.maximum(jnp.sum(e, axis=-1, keepdims=True), 1e-30)


def _paged_rows(pool, page_table):
    rows = pool[page_table]
    b, n_pages = page_table.shape
    return rows.reshape(b, n_pages * pool.shape[1], *pool.shape[2:])


def _to_chunks(a, L):
    b, t = a.shape[:2]
    a = a.reshape(b, t // L, L, *a.shape[2:])
    return a.transpose((1, 0, 3, 2) + tuple(range(4, a.ndim)))


def _from_chunks(a):
    n, b, h, L, d = a.shape
    return a.transpose(1, 0, 3, 2, 4).reshape(b, n * L, h, d)


def _nsa_compress(rows, pe, w1, w2):
    b, t, g, dh = rows.shape
    r_seg = CMP_LEN // CMP_STRIDE
    n_seg = t // CMP_STRIDE
    nc = n_seg - r_seg + 1
    seg = rows[:, :n_seg * CMP_STRIDE].reshape(b, n_seg, CMP_STRIDE, g, dh)
    pe_r = pe.reshape(r_seg, CMP_STRIDE, dh)
    w1_r = w1.reshape(r_seg, CMP_STRIDE, dh, CMP_HID)
    pre = sum(jnp.einsum('bnsgd,sdk->bngk', seg[:, r:r + nc] + pe_r[r][:, None, :], w1_r[r])
              for r in range(r_seg))
    return jax.nn.silu(pre) @ w2


def _nsa_cmp_attend(q, pos, kc, vc):
    nc = kc.shape[1]
    ends = jnp.arange(nc, dtype=jnp.int32) * CMP_STRIDE + CMP_LEN - 1
    mask = (ends[None, :] <= pos[:, None])[None, :, None, None, :]
    s = jnp.einsum('bqghd,bcgd->bqghc', q, kc, preferred_element_type=jnp.float32) * NSA_DH ** -0.5
    p = _masked_softmax(s, mask)
    o = jnp.einsum('bqghc,bcgd->bqghd', p.astype(vc.dtype), vc)
    return o, jnp.sum(p, axis=3)


def _nsa_select(q, pos, imp, ks, vs):
    b, t, g, hpg, dh = q.shape
    tk = ks.shape[1]
    ns = -(-tk // SLC_BLOCK)
    nc = imp.shape[-1]
    cs = jnp.arange(nc, dtype=jnp.int32) * CMP_STRIDE
    ss = jnp.arange(ns, dtype=jnp.int32) * SLC_BLOCK
    overlap = jnp.maximum(jnp.minimum(cs[:, None] + CMP_LEN, ss[None, :] + SLC_BLOCK)
                          - jnp.maximum(cs[:, None], ss[None, :]), 0)
    ov = overlap.astype(jnp.float32) / CMP_LEN
    score = jnp.einsum('bqgc,cs->bqgs', imp, ov)
    blk = jnp.arange(ns, dtype=jnp.int32)
    cur = pos // SLC_BLOCK
    forced = (blk[None, :] == 0) | (blk[None, :] == cur[:, None]) | (blk[None, :] == cur[:, None] - 1)
    valid = blk[None, :] <= cur[:, None]
    score = score + jnp.where(forced, FORCE_BONUS, 0.0)[None, :, None, :]
    score = jnp.where(valid[None, :, None, :], score, NEG)
    n_sel = min(SLC_TOPN, ns)
    top_v, top_i = lax.top_k(score, n_sel)
    sel_ok = top_v > NEG / 2
    pad = ns * SLC_BLOCK - tk
    kb = jnp.pad(ks, ((0, 0), (0, pad), (0, 0), (0, 0))).reshape(b, ns, SLC_BLOCK, g, dh).transpose(0, 1, 3, 2, 4)
    vb = jnp.pad(vs, ((0, 0), (0, pad), (0, 0), (0, 0))).reshape(b, ns, SLC_BLOCK, g, dh).transpose(0, 1, 3, 2, 4)
    qb = math.gcd(t, max(1, NSA_GATHER_ROWS // b))
    nqb = t // qb
    bi = jnp.arange(b)[:, None, None, None]
    gi = jnp.arange(g)[None, None, :, None]
    nk = n_sel * SLC_BLOCK

    def block(args):
        qq, qp, ti, ok = args
        ksel = kb[bi, ti, gi].reshape(b, qb, g, nk, dh)
        vsel = vb[bi, ti, gi].reshape(b, qb, g, nk, dh)
        kpos = (ti[..., None] * SLC_BLOCK + jnp.arange(SLC_BLOCK, dtype=jnp.int32)).reshape(b, qb, g, nk)
        mask = (jnp.repeat(ok, SLC_BLOCK, axis=-1) & (kpos <= qp[None, :, None, None]))[:, :, :, None, :]
        s = jnp.einsum('bqghd,bqgkd->bqghk', qq, ksel, preferred_element_type=jnp.float32) * NSA_DH ** -0.5
        p = _masked_softmax(s, mask)
        return jnp.einsum('bqghk,bqgkd->bqghd', p.astype(vsel.dtype), vsel)

    def split(a):
        return a.reshape(b, nqb, qb, *a.shape[2:]).swapaxes(0, 1)

    out = lax.map(block, (split(q), pos.reshape(nqb, qb), split(top_i), split(sel_ok)))
    return out.swapaxes(0, 1).reshape(q.shape)


def _nsa_window(q, pos, k, v, k_pos0):
    b, t, g, hpg, dh = q.shape
    qb = math.gcd(t, WIN_QBLK)
    nqb = t // qb
    band = WINDOW - 1 + qb
    kp = jnp.pad(k, ((0, 0), (WINDOW - 1, 0), (0, 0), (0, 0)))
    vp = jnp.pad(v, ((0, 0), (WINDOW - 1, 0), (0, 0), (0, 0)))

    def block(args):
        qq, qp = args
        start = qp[0] - k_pos0
        kb = lax.dynamic_slice_in_dim(kp, start, band, axis=1)
        vb = lax.dynamic_slice_in_dim(vp, start, band, axis=1)
        kpos = qp[0] - (WINDOW - 1) + jnp.arange(band, dtype=jnp.int32)
        mask = ((kpos[None, :] <= qp[:, None]) & (kpos[None, :] > qp[:, None] - WINDOW)
                & (kpos[None, :] >= k_pos0))
        s = jnp.einsum('bqghd,bkgd->bghqk', qq, kb, preferred_element_type=jnp.float32) * NSA_DH ** -0.5
        p = _masked_softmax(s, mask)
        return jnp.einsum('bghqk,bkgd->bqghd', p.astype(vb.dtype), vb)

    qs = q.reshape(b, nqb, qb, g, hpg, dh).swapaxes(0, 1)
    out = lax.map(block, (qs, pos.reshape(nqb, qb)))
    return out.swapaxes(0, 1).reshape(q.shape)


def _nsa_mixer(hn, pos, start, past_cmp, past_slc, win_buf, w_in, cmp_pe, cmp_w1, cmp_w2, w_out):
    b, t, _ = hn.shape
    g, hpg, dh = NSA_KV_HEADS, NSA_HPG, NSA_DH
    kvw = 2 * g * dh
    z = hn @ w_in
    q = _rope(z[..., :NSA_HEADS * dh].reshape(b, t, NSA_HEADS, dh), pos).reshape(b, t, g, hpg, dh)
    off = NSA_HEADS * dh

    def kv_rows(zz):
        kv = zz.reshape(b, t, 2, g, dh)
        return jnp.stack([_rope(kv[:, :, 0], pos), kv[:, :, 1]], axis=2)

    new_cmp = kv_rows(z[..., off:off + kvw])
    new_slc = kv_rows(z[..., off + kvw:off + 2 * kvw])
    new_win = kv_rows(z[..., off + 2 * kvw:off + 3 * kvw])
    gates = jax.nn.sigmoid(z[..., off + 3 * kvw:].astype(jnp.float32)).reshape(b, t, 3, g, hpg)
    if past_cmp is None:
        cmp_rows, slc_rows, win_rows = new_cmp, new_slc, new_win
        win_pos0, keep = start, min(WINDOW, t)
    else:
        cmp_rows = jnp.concatenate([past_cmp, new_cmp], axis=1)
        slc_rows = jnp.concatenate([past_slc, new_slc], axis=1)
        win_rows = jnp.concatenate([win_buf, new_win], axis=1)
        win_pos0, keep = start - win_buf.shape[1], win_buf.shape[1]
    kc = _nsa_compress(cmp_rows[:, :, 0], cmp_pe[0], cmp_w1[0], cmp_w2[0])
    vc = _nsa_compress(cmp_rows[:, :, 1], cmp_pe[1], cmp_w1[1], cmp_w2[1])
    o_cmp, imp = _nsa_cmp_attend(q, pos, kc, vc)
    o_slc = _nsa_select(q, pos, imp, slc_rows[:, :, 0], slc_rows[:, :, 1])
    o_win = _nsa_window(q, pos, win_rows[:, :, 0], win_rows[:, :, 1], win_pos0)
    o = (gates[:, :, 0, ..., None] * o_cmp + gates[:, :, 1, ..., None] * o_slc
         + gates[:, :, 2, ..., None] * o_win)
    y = o.astype(hn.dtype).reshape(b, t, NSA_HEADS * dh) @ w_out
    return y, new_cmp, new_slc, win_rows[:, -keep:]


def _mlstm_chunk(carry, xs):
    c_st, n_st, m_st = carry
    q, k, v, ig, lf = xs
    L = q.shape[2]
    causal = jnp.tril(jnp.ones((L, L), dtype=bool))
    bcum = jnp.cumsum(lf, axis=-1)
    dmat = jnp.where(causal, bcum[..., :, None] - bcum[..., None, :] + ig[..., None, :], NEG)
    inter = bcum + m_st[..., None]
    mt = jnp.maximum(jnp.max(dmat, axis=-1), inter)
    w_intra = jnp.exp(dmat - mt[..., None])
    w_state = jnp.exp(inter - mt)
    sc = jnp.einsum('bhtd,bhsd->bhts', q, k) * w_intra
    num = jnp.einsum('bhts,bhsv->bhtv', sc, v) + w_state[..., None] * jnp.einsum('bhvd,bhtd->bhtv', c_st, q)
    den = jnp.sum(sc, axis=-1) + w_state * jnp.einsum('bhd,bhtd->bht', n_st, q)
    h = num / jnp.maximum(jnp.abs(den), jnp.exp(-mt))[..., None]
    bl = bcum[..., -1]
    dl = bl[..., None] - bcum + ig
    m_new = jnp.maximum(bl + m_st, jnp.max(dl, axis=-1))
    ws = jnp.exp(dl - m_new[..., None])
    wc = jnp.exp(bl + m_st - m_new)
    c_new = wc[..., None, None] * c_st + jnp.einsum('bhsv,bhsd->bhvd', v * ws[..., None], k)
    n_new = wc[..., None] * n_st + jnp.einsum('bhs,bhsd->bhd', ws, k)
    return (c_new, n_new, m_new), h


def _mlstm_mixer(hn, state, w_in, gate_b, w_out):
    b, t, _ = hn.shape
    H, dk, dv = ML_HEADS, ML_DK, ML_DV
    z = (hn @ w_in).astype(jnp.float32)
    q = z[..., :H * dk].reshape(b, t, H, dk)
    k = z[..., H * dk:2 * H * dk].reshape(b, t, H, dk) * dk ** -0.5
    v = z[..., 2 * H * dk:2 * H * dk + H * dv].reshape(b, t, H, dv)
    a = 2 * H * dk + H * dv
    ig = z[..., a:a + H] + gate_b[0]
    lf = jax.nn.log_sigmoid(z[..., a + H:a + 2 * H] + gate_b[1])
    og = jax.nn.sigmoid(z[..., a + 2 * H:])
    if state is None:
        carry = (jnp.zeros((b, H, dv, dk), jnp.float32), jnp.zeros((b, H, dk), jnp.float32),
                 jnp.zeros((b, H), jnp.float32))
    else:
        carry = tuple(s.astype(jnp.float32) for s in state)
    L = math.gcd(t, ML_CHUNK)
    carry, hs = lax.scan(_mlstm_chunk, carry, tuple(_to_chunks(u, L) for u in (q, k, v, ig, lf)))
    h = _from_chunks(hs).reshape(b, t, H * dv) * og
    return h.astype(hn.dtype) @ w_out, tuple(s.astype(hn.dtype) for s in carry)


def _latent_attention(q, pos, kv, v_dim, scale):
    b, t, h, c = q.shape
    qb = math.gcd(t, ATTN_QBLK)
    nqb = t // qb
    k_pos = jnp.arange(kv.shape[1], dtype=jnp.int32)
    vals = kv[..., :v_dim]

    def block(args):
        qq, qp = args
        s = jnp.einsum('bqhc,bkc->bhqk', qq, kv, preferred_element_type=jnp.float32) * scale
        p = _masked_softmax(s, k_pos[None, :] <= qp[:, None])
        return jnp.einsum('bhqk,bkc->bqhc', p.astype(vals.dtype), vals)

    out = lax.map(block, (q.reshape(b, nqb, qb, h, c).swapaxes(0, 1), pos.reshape(nqb, qb)))
    return out.swapaxes(0, 1).reshape(b, t, h, v_dim)


def _mla_mixer(hn, pos, past_lat, w_in, q_norm, kv_norm, w_uq, w_uk, w_uv, w_out):
    b, t, _ = hn.shape
    z = hn @ w_in
    cq = _rmsnorm(z[..., :MLA_Q_LORA], q_norm)
    ckv = _rmsnorm(z[..., MLA_Q_LORA:MLA_Q_LORA + MLA_KV_LORA], kv_norm)
    kr = _rope(z[..., MLA_Q_LORA + MLA_KV_LORA:][:, :, None, :], pos)[:, :, 0]
    new_lat = jnp.concatenate([ckv, kr], axis=-1)
    lat = new_lat if past_lat is None else jnp.concatenate([past_lat, new_lat], axis=1)
    qf = (cq @ w_uq).reshape(b, t, MLA_HEADS, MLA_NOPE + MLA_ROPE)
    q_lat = jnp.einsum('bthn,chn->bthc', qf[..., :MLA_NOPE], w_uk)
    q_rope = _rope(qf[..., MLA_NOPE:], pos)
    q_cat = jnp.concatenate([q_lat, q_rope], axis=-1)
    o_lat = _latent_attention(q_cat, pos, lat, MLA_KV_LORA, (MLA_NOPE + MLA_ROPE) ** -0.5)
    o = jnp.einsum('bthc,chv->bthv', o_lat, w_uv).reshape(b, t, MLA_HEADS * MLA_DV)
    return o @ w_out, new_lat


def _hgrn_chunk(s_st, xs):
    q, k, v, lf = xs
    L = q.shape[2]
    causal = jnp.tril(jnp.ones((L, L), dtype=bool))
    bcum = jnp.cumsum(lf, axis=2)
    decay = jnp.exp(jnp.where(causal[:, :, None], bcum[:, :, :, None, :] - bcum[:, :, None, :, :], NEG))
    att = jnp.einsum('bhtc,bhsc,bhtsc->bhts', q, k, decay)
    o = jnp.einsum('bhts,bhsv->bhtv', att, v) + jnp.einsum('bhtc,bhcv->bhtv', q * jnp.exp(bcum), s_st)
    bl = bcum[:, :, -1]
    s_new = jnp.exp(bl)[..., None] * s_st + jnp.einsum('bhsc,bhsv->bhcv', k * jnp.exp(bl[:, :, None] - bcum), v)
    return s_new, o


def _hgrn_mixer(hn, state, lower, w_in, f_b, g_norm, w_out):
    b, t, _ = hn.shape
    H, dk, dv = HG_HEADS, HG_DK, HG_DV
    z = (hn @ w_in).astype(jnp.float32)
    q = z[..., :H * dk].reshape(b, t, H, dk)
    f = lower + (1.0 - lower) * jax.nn.sigmoid(z[..., H * dk:2 * H * dk] + f_b)
    lf = jnp.log(f).reshape(b, t, H, dk)
    k = (1.0 - f).reshape(b, t, H, dk)
    v = z[..., 2 * H * dk:2 * H * dk + H * dv].reshape(b, t, H, dv)
    gate = z[..., 2 * H * dk + H * dv:]
    s0 = jnp.zeros((b, H, dk, dv), jnp.float32) if state is None else state.astype(jnp.float32)
    L = math.gcd(t, HG_CHUNK)
    s_fin, os_ = lax.scan(_hgrn_chunk, s0, tuple(_to_chunks(u, L) for u in (q, k, v, lf)))
    o = _rmsnorm(_from_chunks(os_), g_norm).reshape(b, t, H * dv) * jax.nn.silu(gate)
    return o.astype(hn.dtype) @ w_out, s_fin.astype(hn.dtype)


def _conv_ffn(hn, buf, w_up, conv_w, conv_b, w_down):
    b, t, _ = hn.shape
    u = hn @ w_up
    prev = jnp.zeros((b, CONV_W - 1, u.shape[-1]), u.dtype) if buf is None else buf
    up = jnp.concatenate([prev, u], axis=1)
    c = sum(up[:, j:j + t] * conv_w[j] for j in range(CONV_W)) + conv_b
    a, g = jnp.split(c, 2, axis=-1)
    return (jax.nn.silu(a) * g) @ w_down, up[:, -(CONV_W - 1):]


def _ple_add(h, p, w_proj, w_gate, g_norm):
    gate = jax.nn.sigmoid(_rmsnorm(h, g_norm) @ w_gate)
    return h + gate * (p @ w_proj)


def _trunk(x, p, start, past, prm):
    b, t, _ = x.shape
    pos = start + jnp.arange(t, dtype=jnp.int32)
    sm = jax.nn.softmax(prm['hg_lb_logits'].astype(jnp.float32), axis=0)
    lower = jnp.cumsum(sm, axis=0) - sm[0]
    new = {name: [] for name in STATE_NAMES}
    h = x
    for i in range(DEPTH):
        kind, occ = i % N_MIXERS, i // N_MIXERS
        hn = _rmsnorm(h, prm['norm_mix'][i])
        if kind == 0:
            if past is None:
                pc = ps = wb = None
            else:
                pc = _paged_rows(past['nsa_cmp'][occ], past['page_table'])
                ps = _paged_rows(past['nsa_slc'][occ], past['page_table'])
                wb = past['nsa_win'][occ]
            y, c_rows, s_rows, w_rows = _nsa_mixer(
                hn, pos, start, pc, ps, wb, prm['nsa_w_in'][occ], prm['nsa_cmp_pe'][occ],
                prm['nsa_cmp_w1'][occ], prm['nsa_cmp_w2'][occ], prm['nsa_w_out'][occ])
            new['nsa_cmp'].append(c_rows)
            new['nsa_slc'].append(s_rows)
            new['nsa_win'].append(w_rows)
        elif kind == 1:
            st = None if past is None else (past['ml_C'][occ], past['ml_n'][occ], past['ml_m'][occ])
            y, (c_st, n_st, m_st) = _mlstm_mixer(hn, st, prm['ml_w_in'][occ], prm['ml_gate_b'][occ],
                                                 prm['ml_w_out'][occ])
            new['ml_C'].append(c_st)
            new['ml_n'].append(n_st)
            new['ml_m'].append(m_st)
        elif kind == 2:
            pl = None if past is None else _paged_rows(past['mla'][occ], past['page_table'])
            y, lat = _mla_mixer(hn, pos, pl, prm['mla_w_in'][occ], prm['mla_q_norm'][occ],
                                prm['mla_kv_norm'][occ], prm['mla_w_uq'][occ], prm['mla_w_uk'][occ],
                                prm['mla_w_uv'][occ], prm['mla_w_out'][occ])
            new['mla'].append(lat)
        else:
            st = None if past is None else past['hg_S'][occ]
            y, s_st = _hgrn_mixer(hn, st, lower[i], prm['hg_w_in'][occ], prm['hg_f_b'][occ],
                                  prm['hg_norm'][occ], prm['hg_w_out'][occ])
            new['hg_S'].append(s_st)
        h = h + y
        hn = _rmsnorm(h, prm['norm_ffn'][i])
        buf = None if past is None else past['ffn_conv'][i]
        y, cbuf = _conv_ffn(hn, buf, prm['ffn_w_up'][i], prm['ffn_conv_w'][i], prm['ffn_conv_b'][i],
                            prm['ffn_w_down'][i])
        new['ffn_conv'].append(cbuf)
        h = h + y
        h = _ple_add(h, p[i], prm['ple_w_proj'][i], prm['ple_w_gate'][i], prm['norm_ple'][i])
    return _rmsnorm(h, prm['norm_final']), {name: jnp.stack(v) for name, v in new.items()}


def setup_inputs(seed: int = 0) -> dict:
    key = jax.random.key(seed)
    keys = iter(jax.random.split(key, 64))

    def nrm(shape, scale=1.0):
        return jax.random.normal(next(keys), shape, jnp.float32) * scale

    n_pages = PAST_LEN // PAGE_SIZE
    n_used = DEC_BATCH * n_pages
    n_pool = n_used + n_used // 4
    win_buf = min(WINDOW, PAST_LEN)
    kvw = 2 * NSA_KV_HEADS * NSA_DH
    nsa_in = NSA_HEADS * NSA_DH + 3 * kvw + 3 * NSA_HEADS
    ml_in = 2 * ML_HEADS * ML_DK + 2 * ML_HEADS * ML_DV + 2 * ML_HEADS
    mla_in = MLA_Q_LORA + MLA_KV_LORA + MLA_ROPE
    hg_in = 2 * HG_HEADS * HG_DK + 2 * HG_HEADS * HG_DV
    lat_w = MLA_KV_LORA + MLA_ROPE
    d = D_MODEL
    x_prompt = nrm((BATCH, SEQ, d))
    x_sample = nrm((DEC_BATCH, DEC_SEQ, d))
    cache_nsa_cmp_kv = nrm((N_NSA, n_pool, PAGE_SIZE, 2, NSA_KV_HEADS, NSA_DH))
    cache_nsa_slc_kv = nrm((N_NSA, n_pool, PAGE_SIZE, 2, NSA_KV_HEADS, NSA_DH))
    state_nsa_win_kv = nrm((N_NSA, DEC_BATCH, win_buf, 2, NSA_KV_HEADS, NSA_DH))
    cache_mla_latent = nrm((N_MLA, n_pool, PAGE_SIZE, lat_w))
    state_mlstm_C = nrm((N_ML, DEC_BATCH, ML_HEADS, ML_DV, ML_DK))
    state_mlstm_n = nrm((N_ML, DEC_BATCH, ML_HEADS, ML_DK))
    state_mlstm_m = nrm((N_ML, DEC_BATCH, ML_HEADS), 0.5)
    state_hgrn_S = nrm((N_HG, DEC_BATCH, HG_HEADS, HG_DK, HG_DV), 0.5)
    state_ffn_conv = nrm((DEPTH, DEC_BATCH, CONV_W - 1, 2 * D_FF))
    page_table = jax.random.permutation(next(keys), n_pool)[:n_used].reshape(DEC_BATCH, n_pages).astype(jnp.int32)
    p_prompt = nrm((DEPTH, BATCH, SEQ, PLE_DIM))
    p_sample = nrm((DEPTH, DEC_BATCH, DEC_SEQ, PLE_DIM))
    return {
        'x_prompt': x_prompt,
        'x_sample': x_sample,
        'cache_nsa_cmp_kv': cache_nsa_cmp_kv,
        'cache_nsa_slc_kv': cache_nsa_slc_kv,
        'state_nsa_win_kv': state_nsa_win_kv,
        'cache_mla_latent': cache_mla_latent,
        'state_mlstm_C': state_mlstm_C,
        'state_mlstm_n': state_mlstm_n,
        'state_mlstm_m': state_mlstm_m,
        'state_hgrn_S': state_hgrn_S,
        'state_ffn_conv': state_ffn_conv,
        'page_table': page_table,
        'p_prompt': p_prompt,
        'p_sample': p_sample,
        'norm_mix': 1.0 + nrm((DEPTH, d), 0.01),
        'norm_ffn': 1.0 + nrm((DEPTH, d), 0.01),
        'norm_ple': 1.0 + nrm((DEPTH, d), 0.01),
        'norm_final': 1.0 + nrm((d,), 0.01),
        'nsa_w_in': nrm((N_NSA, d, nsa_in), d ** -0.5),
        'nsa_cmp_pe': nrm((N_NSA, 2, CMP_LEN, NSA_DH), 0.02),
        'nsa_cmp_w1': nrm((N_NSA, 2, CMP_LEN * NSA_DH, CMP_HID), (CMP_LEN * NSA_DH) ** -0.5),
        'nsa_cmp_w2': nrm((N_NSA, 2, CMP_HID, NSA_DH), CMP_HID ** -0.5),
        'nsa_w_out': nrm((N_NSA, NSA_HEADS * NSA_DH, d), (NSA_HEADS * NSA_DH) ** -0.5),
        'ml_w_in': nrm((N_ML, d, ml_in), d ** -0.5),
        'ml_gate_b': jnp.stack([nrm((N_ML, ML_HEADS), 0.1), 3.0 + nrm((N_ML, ML_HEADS), 0.5)], axis=1),
        'ml_w_out': nrm((N_ML, ML_HEADS * ML_DV, d), (ML_HEADS * ML_DV) ** -0.5),
        'mla_w_in': nrm((N_MLA, d, mla_in), d ** -0.5),
        'mla_q_norm': 1.0 + nrm((N_MLA, MLA_Q_LORA), 0.01),
        'mla_kv_norm': 1.0 + nrm((N_MLA, MLA_KV_LORA), 0.01),
        'mla_w_uq': nrm((N_MLA, MLA_Q_LORA, MLA_HEADS * (MLA_NOPE + MLA_ROPE)), MLA_Q_LORA ** -0.5),
        'mla_w_uk': nrm((N_MLA, MLA_KV_LORA, MLA_HEADS, MLA_NOPE), MLA_KV_LORA ** -0.5),
        'mla_w_uv': nrm((N_MLA, MLA_KV_LORA, MLA_HEADS, MLA_DV), MLA_KV_LORA ** -0.5),
        'mla_w_out': nrm((N_MLA, MLA_HEADS * MLA_DV, d), (MLA_HEADS * MLA_DV) ** -0.5),
        'hg_w_in': nrm((N_HG, d, hg_in), d ** -0.5),
        'hg_f_b': nrm((N_HG, HG_HEADS * HG_DK), 0.1),
        'hg_lb_logits': nrm((DEPTH, HG_HEADS * HG_DK), 0.1),
        'hg_norm': 1.0 + nrm((N_HG, HG_DV), 0.01),
        'hg_w_out': nrm((N_HG, HG_HEADS * HG_DV, d), (HG_HEADS * HG_DV) ** -0.5),
        'ffn_w_up': nrm((DEPTH, d, 2 * D_FF), d ** -0.5),
        'ffn_conv_w': nrm((DEPTH, CONV_W, 2 * D_FF), CONV_W ** -0.5),
        'ffn_conv_b': nrm((DEPTH, 2 * D_FF), 0.01),
        'ffn_w_down': nrm((DEPTH, D_FF, d), D_FF ** -0.5),
        'ple_w_proj': nrm((DEPTH, PLE_DIM, d), 0.5 * PLE_DIM ** -0.5),
        'ple_w_gate': nrm((DEPTH, d, d), d ** -0.5),
    }


def reference(x_prompt, x_sample, cache_nsa_cmp_kv, cache_nsa_slc_kv, state_nsa_win_kv, cache_mla_latent,
              state_mlstm_C, state_mlstm_n, state_mlstm_m, state_hgrn_S, state_ffn_conv, page_table,
              p_prompt, p_sample, norm_mix, norm_ffn, norm_ple, norm_final, nsa_w_in, nsa_cmp_pe, nsa_cmp_w1,
              nsa_cmp_w2, nsa_w_out, ml_w_in, ml_gate_b, ml_w_out, mla_w_in, mla_q_norm, mla_kv_norm, mla_w_uq,
              mla_w_uk, mla_w_uv, mla_w_out, hg_w_in, hg_f_b, hg_lb_logits, hg_norm, hg_w_out, ffn_w_up,
              ffn_conv_w, ffn_conv_b, ffn_w_down, ple_w_proj, ple_w_gate):
    prm = {
        'norm_mix': norm_mix, 'norm_ffn': norm_ffn, 'norm_ple': norm_ple, 'norm_final': norm_final,
        'nsa_w_in': nsa_w_in, 'nsa_cmp_pe': nsa_cmp_pe, 'nsa_cmp_w1': nsa_cmp_w1, 'nsa_cmp_w2': nsa_cmp_w2,
        'nsa_w_out': nsa_w_out, 'ml_w_in': ml_w_in, 'ml_gate_b': ml_gate_b, 'ml_w_out': ml_w_out,
        'mla_w_in': mla_w_in, 'mla_q_norm': mla_q_norm, 'mla_kv_norm': mla_kv_norm, 'mla_w_uq': mla_w_uq,
        'mla_w_uk': mla_w_uk, 'mla_w_uv': mla_w_uv, 'mla_w_out': mla_w_out, 'hg_w_in': hg_w_in,
        'hg_f_b': hg_f_b, 'hg_lb_logits': hg_lb_logits, 'hg_norm': hg_norm, 'hg_w_out': hg_w_out,
        'ffn_w_up': ffn_w_up, 'ffn_conv_w': ffn_conv_w, 'ffn_conv_b': ffn_conv_b, 'ffn_w_down': ffn_w_down,
        'ple_w_proj': ple_w_proj, 'ple_w_gate': ple_w_gate,
    }
    past = {
        'nsa_cmp': cache_nsa_cmp_kv, 'nsa_slc': cache_nsa_slc_kv, 'nsa_win': state_nsa_win_kv,
        'mla': cache_mla_latent, 'ml_C': state_mlstm_C, 'ml_n': state_mlstm_n, 'ml_m': state_mlstm_m,
        'hg_S': state_hgrn_S, 'ffn_conv': state_ffn_conv, 'page_table': page_table,
    }
    y_prompt, sp = _trunk(x_prompt, p_prompt, 0, None, prm)
    y_sample, ss = _trunk(x_sample, p_sample, PAST_LEN, past, prm)
    return (y_prompt, y_sample,
            sp['nsa_cmp'], ss['nsa_cmp'], sp['nsa_slc'], ss['nsa_slc'], sp['nsa_win'], ss['nsa_win'],
            sp['mla'], ss['mla'], sp['ml_C'], ss['ml_C'], sp['ml_n'], ss['ml_n'], sp['ml_m'], ss['ml_m'],
            sp['hg_S'], ss['hg_S'], sp['ffn_conv'], ss['ffn_conv'])
```

```python
import functools
import math

import numpy as np
import jax
import jax.numpy as jnp
from jax import lax
from jax.experimental import pallas as pl
from jax.experimental.pallas import tpu as pltpu

F32 = jnp.float32
BF16 = jnp.bfloat16
NEG = -1e30
NORM_EPS = 1e-6
ROPE_THETA = 10000.0
FORCE_BONUS = 1e4

LANES = 128
VMEM_LIMIT = 56 * 1024 * 1024

PAGE = 128
NSA_HEADS, NSA_G, NSA_DH = 16, 2, 64
CMP_LEN, CMP_STRIDE, CMP_HID = 32, 16, 256
SLC_BLOCK, SLC_TOPN, WINDOW = 64, 16, 512
ML_HEADS, ML_DK, ML_DV = 4, 128, 256
MLA_HEADS, MLA_Q_LORA, MLA_KV_LORA, MLA_NOPE, MLA_ROPE, MLA_DV = 16, 512, 256, 64, 32, 64
HG_HEADS, HG_DK, HG_DV = 8, 128, 128
CONV_W = 3


def _cparams(*sem):
    return pltpu.CompilerParams(dimension_semantics=sem, vmem_limit_bytes=VMEM_LIMIT)


def _dot(a, b):
    return jnp.dot(a.astype(BF16), b.astype(BF16), preferred_element_type=F32)


def _dot_nt(a, b):
    return lax.dot_general(a.astype(BF16), b.astype(BF16), (((1,), (1,)), ((), ())),
                           preferred_element_type=F32)


def _dot_f32(a, b):
    return jnp.dot(a, b, precision=lax.Precision.HIGHEST, preferred_element_type=F32)


def _dot_nt_f32(a, b):
    return lax.dot_general(a, b, (((1,), (1,)), ((), ())), precision=lax.Precision.HIGHEST,
                           preferred_element_type=F32)


def _rms(x, g):
    return x * lax.rsqrt(jnp.mean(x * x, axis=-1, keepdims=True) + NORM_EPS) * g


def _sigmoid(x):
    return 1.0 / (1.0 + jnp.exp(-x))


def _silu(x):
    return x * _sigmoid(x)


def _rope_tile(z, cos, sin, half):
    n = z.shape[1]
    reps = n // LANES
    c = jnp.concatenate([cos] * reps, axis=1) if reps > 1 else cos
    s = jnp.concatenate([sin] * reps, axis=1) if reps > 1 else sin
    lane = lax.broadcasted_iota(jnp.int32, z.shape, 1)
    lower = (lane & (2 * half - 1)) < half
    partner = jnp.where(lower, pltpu.roll(z, n - half, 1), pltpu.roll(z, half, 1))
    return z * c + partner * s


def _rope_tables(pos, half):
    inv = jnp.power(ROPE_THETA, -jnp.arange(half, dtype=F32) / half)
    ang = pos.astype(F32)[:, None] * inv[None, :]
    cos, sin = jnp.cos(ang), jnp.sin(ang)
    reps = LANES // (2 * half)
    return (jnp.tile(jnp.concatenate([cos, cos], axis=1), (1, reps)),
            jnp.tile(jnp.concatenate([-sin, sin], axis=1), (1, reps)))


def _mm_kernel(*refs, norm, res, epi, n_aux, x_slabs, out_slabs):
    x_ref = refs[0]
    pos = 1
    g_ref = None
    if norm:
        g_ref = refs[pos]
        pos += 1
    w_ref = refs[pos]
    pos += 1
    aux = refs[pos:pos + n_aux]
    pos += n_aux
    r_ref = None
    if res:
        r_ref = refs[pos]
        pos += 1
    o_ref = refs[pos]
    j = pl.program_id(1)
    if norm:
        xn_ref = refs[pos + 1]

        @pl.when(j == 0)
        def _():
            xn_ref[...] = _rms(x_ref[...], g_ref[...]).astype(BF16)

        x = xn_ref[...]
    elif x_slabs:
        x = jnp.concatenate([x_ref[s] for s in range(x_slabs)], axis=1)
    else:
        x = x_ref[...]
    z = _dot(x, w_ref[...])
    if epi is not None:
        z = epi(z, j, *[a[...] for a in aux])
    if res:
        z = z + r_ref[...]
    if out_slabs:
        for s in range(out_slabs):
            o_ref[s] = z[:, s * LANES:(s + 1) * LANES].astype(o_ref.dtype)
    else:
        o_ref[...] = z.astype(o_ref.dtype)


def _mm(x, w, *, g=None, res=None, epi=None, aux=(), out_dtype=F32, layout='flat',
        seq=None, tm=1024, tn=1024, x_cols=None, x_slab_seq=None):
    k, n = w.shape
    if x_slab_seq is not None:
        b, ks, t, _ = x.shape
        m = b * t
    else:
        m = x.shape[0]
        t = seq
    tm = min(tm, m if t is None else t)
    tn = min(tn, n)
    assert m % tm == 0 and n % tn == 0
    ni, nj = m // tm, n // tn
    tps = None if t is None else t // tm
    in_specs, args = [], []
    if x_slab_seq is not None:
        in_specs.append(pl.BlockSpec((None, ks, tm, LANES), lambda i, j: (i // tps, 0, i % tps, 0)))
    elif x_cols is not None:
        in_specs.append(pl.BlockSpec((tm, x_cols[0]), lambda i, j: (i, x_cols[1])))
    else:
        in_specs.append(pl.BlockSpec((tm, k), lambda i, j: (i, 0)))
    args.append(x)
    if g is not None:
        in_specs.append(pl.BlockSpec((1, k), lambda i, j: (0, 0)))
        args.append(g.reshape(1, k).astype(F32))
    in_specs.append(pl.BlockSpec((k, tn), lambda i, j: (0, j)))
    args.append(w)
    for a, bs, im in aux:
        in_specs.append(pl.BlockSpec(bs, im))
        args.append(a)
    if res is not None:
        in_specs.append(pl.BlockSpec((tm, tn), lambda i, j: (i, j)))
        args.append(res)
    if layout == 'flat':
        out_shape = jax.ShapeDtypeStruct((m, n), out_dtype)
        out_spec = pl.BlockSpec((tm, tn), lambda i, j: (i, j))
        out_slabs = 0
    elif layout == 'tiles':
        out_shape = jax.ShapeDtypeStruct((nj, m, tn), out_dtype)
        out_spec = pl.BlockSpec((None, tm, tn), lambda i, j: (j, i, 0))
        out_slabs = 0
    else:
        out_slabs = tn // LANES
        out_shape = jax.ShapeDtypeStruct((m // t, n // LANES, t, LANES), out_dtype)
        out_spec = pl.BlockSpec((None, out_slabs, tm, LANES), lambda i, j: (i // tps, j, i % tps, 0))
    scratch = [pltpu.VMEM((tm, k), BF16)] if g is not None else []
    kern = functools.partial(_mm_kernel, norm=g is not None, res=res is not None, epi=epi,
                             n_aux=len(aux), x_slabs=(ks if x_slab_seq is not None else 0),
                             out_slabs=out_slabs)
    return pl.pallas_call(
        kern, grid=(ni, nj), in_specs=in_specs, out_specs=out_spec, out_shape=out_shape,
        scratch_shapes=scratch, compiler_params=_cparams('parallel', 'arbitrary'),
        name='mm')(*args)


def _rope_aux(cos, sin, tm, period_blocks):
    if period_blocks is None:
        im = lambda i, j: (i, 0)
    else:
        im = lambda i, j: (i % period_blocks, 0)
    return [(cos, (tm, LANES), im), (sin, (tm, LANES), im)]


def _headmm_kernel(x_ref, w_ref, o_ref, *, heads, a, c):
    for h in range(heads):
        o_ref[:, h * c:(h + 1) * c] = _dot(x_ref[:, h * a:(h + 1) * a], w_ref[h]).astype(o_ref.dtype)


def _headmm(x, w, out_dtype=BF16, tm=512):
    m = x.shape[0]
    heads, a, c = w.shape
    tm = min(tm, m)
    return pl.pallas_call(
        functools.partial(_headmm_kernel, heads=heads, a=a, c=c), grid=(m // tm,),
        in_specs=[pl.BlockSpec((tm, heads * a), lambda i: (i, 0)),
                  pl.BlockSpec((heads, a, c), lambda i: (0, 0, 0))],
        out_specs=pl.BlockSpec((tm, heads * c), lambda i: (i, 0)),
        out_shape=jax.ShapeDtypeStruct((m, heads * c), out_dtype),
        compiler_params=_cparams('parallel'), name='headmm')(x, w)


def _ffn_kernel(*refs, tps, seq_len, has_prev, nj):
    if has_prev:
        (h_ref, g_ref, wa_ref, wg_ref, ca_ref, cg_ref, wd_ref, p1a_ref, p1g_ref, p2a_ref, p2g_ref,
         o_ref, ua_ref, ug_ref, hn_ref, acc_ref) = refs
    else:
        (h_ref, g_ref, wa_ref, wg_ref, ca_ref, cg_ref, wd_ref,
         o_ref, sa_ref, sg_ref, hn_ref, acc_ref, carry_ref) = refs
    i, j = pl.program_id(0), pl.program_id(1)
    tm = h_ref.shape[0]

    @pl.when(j == 0)
    def _():
        x = h_ref[...]
        hn_ref[...] = _rms(x, g_ref[...]).astype(BF16)
        acc_ref[...] = x

    hn = hn_ref[...]

    def conv(u, c_ref, which, p1_ref, p2_ref):
        cw = c_ref[...]
        if has_prev:
            t = lax.broadcasted_iota(jnp.int32, u.shape, 0) % seq_len
            s1 = jnp.where(t >= 1, pltpu.roll(u, 1, 0), p1_ref[...])
            s2 = jnp.where(t >= 2, pltpu.roll(u, 2, 0), p2_ref[...])
        else:
            @pl.when(i % tps == 0)
            def _():
                carry_ref[j, which] = jnp.zeros((8, u.shape[1]), F32)

            prev = carry_ref[j, which]
            x = jnp.concatenate([prev, u], axis=0)
            s1 = pltpu.roll(x, 1, 0)[8:]
            s2 = pltpu.roll(x, 2, 0)[8:]
            carry_ref[j, which] = u[tm - 8:]
        return cw[0:1] * s2 + cw[1:2] * s1 + cw[2:3] * u + cw[3:4]

    ua = _dot(hn, wa_ref[...])
    ug = _dot(hn, wg_ref[...])
    if has_prev:
        ua_ref[...] = ua
        ug_ref[...] = ug
        ca = conv(ua, ca_ref, 0, p1a_ref, p2a_ref)
        cg = conv(ug, cg_ref, 1, p1g_ref, p2g_ref)
    else:
        sa_ref[...] = pltpu.roll(ua[tm - 8:], CONV_W - 1, 0)[0:CONV_W - 1]
        sg_ref[...] = pltpu.roll(ug[tm - 8:], CONV_W - 1, 0)[0:CONV_W - 1]
        ca = conv(ua, ca_ref, 0, None, None)
        cg = conv(ug, cg_ref, 1, None, None)
    act = _silu(ca) * cg
    acc_ref[...] += _dot(act, wd_ref[...])

    @pl.when(j == nj - 1)
    def _():
        o_ref[...] = acc_ref[...]


def _ffn(h, g, w_up, conv4, w_down, seq_len, prev=None, tn=256):
    m, d = h.shape
    f = w_down.shape[0]
    nj = f // tn
    has_prev = prev is not None
    tm = m if has_prev else min(1024, seq_len)
    tps = max(seq_len // tm, 1)
    ni = m // tm
    specs = [pl.BlockSpec((tm, d), lambda i, j: (i, 0)),
             pl.BlockSpec((1, d), lambda i, j: (0, 0)),
             pl.BlockSpec((d, tn), lambda i, j: (0, j)),
             pl.BlockSpec((d, tn), lambda i, j: (0, j + nj)),
             pl.BlockSpec((4, tn), lambda i, j: (0, j)),
             pl.BlockSpec((4, tn), lambda i, j: (0, j + nj)),
             pl.BlockSpec((tn, d), lambda i, j: (j, 0))]
    args = [h, g.reshape(1, d).astype(F32), w_up, w_up, conv4, conv4, w_down]
    scratch = [pltpu.VMEM((tm, d), BF16), pltpu.VMEM((tm, d), F32)]
    if has_prev:
        p1, p2 = prev
        specs += [pl.BlockSpec((tm, tn), lambda i, j: (i, j)), pl.BlockSpec((tm, tn), lambda i, j: (i, j + nj)),
                  pl.BlockSpec((tm, tn), lambda i, j: (i, j)), pl.BlockSpec((tm, tn), lambda i, j: (i, j + nj))]
        args += [p1, p1, p2, p2]
        out_shape = [jax.ShapeDtypeStruct((m, d), F32), jax.ShapeDtypeStruct((m, f), F32),
                     jax.ShapeDtypeStruct((m, f), F32)]
        out_specs = [pl.BlockSpec((tm, d), lambda i, j: (i, 0)), pl.BlockSpec((tm, tn), lambda i, j: (i, j)),
                     pl.BlockSpec((tm, tn), lambda i, j: (i, j))]
    else:
        out_shape = [jax.ShapeDtypeStruct((m, d), F32), jax.ShapeDtypeStruct((ni, CONV_W - 1, f), F32),
                     jax.ShapeDtypeStruct((ni, CONV_W - 1, f), F32)]
        out_specs = [pl.BlockSpec((tm, d), lambda i, j: (i, 0)),
                     pl.BlockSpec((None, CONV_W - 1, tn), lambda i, j: (i, 0, j)),
                     pl.BlockSpec((None, CONV_W - 1, tn), lambda i, j: (i, 0, j))]
        scratch.append(pltpu.VMEM((nj, 2, 8, tn), F32))
    outs = pl.pallas_call(
        functools.partial(_ffn_kernel, tps=tps, seq_len=seq_len, has_prev=has_prev, nj=nj),
        grid=(ni, nj), in_specs=specs, out_specs=out_specs, out_shape=out_shape,
        scratch_shapes=scratch, compiler_params=_cparams('arbitrary', 'arbitrary'), name='ffn')(*args)
    if has_prev:
        return outs
    return outs[0], outs[1][tps - 1::tps], outs[2][tps - 1::tps]


def _ple_kernel(*refs, final):
    if final:
        h_ref, g_ref, wg_ref, p_ref, wp_ref, gf_ref, o_ref = refs
    else:
        h_ref, g_ref, wg_ref, p_ref, wp_ref, o_ref = refs
    h = h_ref[...]
    gate = _sigmoid(_dot(_rms(h, g_ref[...]), wg_ref[...]))
    y = h + gate * _dot(p_ref[...], wp_ref[...])
    if final:
        y = _rms(y, gf_ref[...])
    o_ref[...] = y


def _ple(h, g, w_gate, p, w_proj, final_g=None, tm=1024):
    m, d = h.shape
    pd = p.shape[1]
    tm = min(tm, m)
    specs = [pl.BlockSpec((tm, d), lambda i: (i, 0)), pl.BlockSpec((1, d), lambda i: (0, 0)),
             pl.BlockSpec((d, d), lambda i: (0, 0)), pl.BlockSpec((tm, pd), lambda i: (i, 0)),
             pl.BlockSpec((pd, d), lambda i: (0, 0))]
    args = [h, g.reshape(1, d).astype(F32), w_gate, p, w_proj]
    if final_g is not None:
        specs.append(pl.BlockSpec((1, d), lambda i: (0, 0)))
        args.append(final_g.reshape(1, d).astype(F32))
    return pl.pallas_call(
        functools.partial(_ple_kernel, final=final_g is not None), grid=(m // tm,),
        in_specs=specs, out_specs=pl.BlockSpec((tm, d), lambda i: (i, 0)),
        out_shape=jax.ShapeDtypeStruct((m, d), F32), compiler_params=_cparams('parallel'),
        name='ple')(*args)


def _softmax_tile(s, mask):
    s = jnp.where(mask, s, NEG)
    e = jnp.where(mask, jnp.exp(s - jnp.max(s, axis=-1, keepdims=True)), 0.0)
    return e / jnp.maximum(jnp.sum(e, axis=-1, keepdims=True), 1e-30)


def _online_init(m_ref, l_ref, acc_ref):
    m_ref[...] = jnp.full(m_ref.shape, NEG, F32)
    l_ref[...] = jnp.zeros(l_ref.shape, F32)
    acc_ref[...] = jnp.zeros(acc_ref.shape, F32)


def _online_update(s, mask, v, m_ref, l_ref, acc_ref):
    if mask is not None:
        s = jnp.where(mask, s, NEG)
    m_old = m_ref[...]
    m_new = jnp.maximum(m_old, jnp.max(s, axis=-1, keepdims=True))
    p = jnp.exp(s - m_new)
    if mask is not None:
        p = jnp.where(mask, p, 0.0)
    alpha = jnp.exp(m_old - m_new)
    l_ref[...] = alpha * l_ref[...] + jnp.sum(p, axis=-1, keepdims=True)
    acc_ref[...] = alpha * acc_ref[...] + _dot(p, v)
    m_ref[...] = m_new


def _online_result(l_ref, acc_ref):
    return acc_ref[...] / jnp.maximum(l_ref[...], 1e-30)


def _cumsum(x, axis):
    n = x.shape[axis]
    idx = lax.broadcasted_iota(jnp.int32, x.shape, axis)
    k = 1
    while k < n:
        x = x + jnp.where(idx >= k, pltpu.roll(x, k, axis), 0.0)
        k *= 2
    return x


def _log_sigmoid(x):
    return jnp.minimum(x, 0.0) - jnp.log(1.0 + jnp.exp(-jnp.abs(x)))


def _compress_rows(k_ref, v_ref, n_seg, pe_ref, w1_ref, w2p_ref):
    outs = []
    for kind, rows_ref in enumerate((k_ref, v_ref)):
        o = None
        for g in range(NSA_G):
            lo = g * NSA_DH
            pre = None
            for r in range(CMP_LEN // CMP_STRIDE):
                acc = jnp.zeros((n_seg, CMP_HID), F32)
                for s in range(CMP_STRIDE):
                    row = r * CMP_STRIDE + s
                    x = rows_ref[pl.ds(s, n_seg, stride=CMP_STRIDE), :][:, lo:lo + NSA_DH]
                    acc = acc + _dot(x + pe_ref[kind, row:row + 1, :],
                                     w1_ref[kind, row * NSA_DH:(row + 1) * NSA_DH, :])
                pre = acc if r == 0 else pre + pltpu.roll(acc, n_seg - r, 0)
            t = _dot(_silu(pre), w2p_ref[kind, g])
            o = t if o is None else o + t
        outs.append(o)
    return outs


def _nsa_cmp_kernel(k_ref, v_ref, pe_ref, w1_ref, w2p_ref, kc_ref, vc_ref, *, n_seg):
    kc, vc = _compress_rows(k_ref, v_ref, n_seg, pe_ref, w1_ref, w2p_ref)
    kc_ref[...] = kc
    vc_ref[...] = vc


def _nsa_compress(rows, pe, w1, w2p):
    b, t, _ = rows.shape
    n_seg = t // CMP_STRIDE
    full = lambda shp: pl.BlockSpec(shp, lambda i: (0,) * len(shp))
    return pl.pallas_call(
        functools.partial(_nsa_cmp_kernel, n_seg=n_seg), grid=(b,),
        in_specs=[pl.BlockSpec((None, t, LANES), lambda i: (i, 0, 0)),
                  pl.BlockSpec((None, t, LANES), lambda i: (i, 0, 1)), full(pe.shape), full(w1.shape),
                  full(w2p.shape)],
        out_specs=[pl.BlockSpec((None, n_seg, LANES), lambda i: (i, 0, 0))] * 2,
        out_shape=[jax.ShapeDtypeStruct((b, n_seg, LANES), F32)] * 2,
        compiler_params=_cparams('parallel'), name='nsa_compress')(rows, rows, pe, w1, w2p)


def _overlap_t(nc, ns, nc_pad, ns_pad):
    cs = np.arange(nc)[None, :] * CMP_STRIDE
    ss = np.arange(ns)[:, None] * SLC_BLOCK
    ov = np.maximum(np.minimum(cs + CMP_LEN, ss + SLC_BLOCK) - np.maximum(cs, ss), 0) / CMP_LEN
    out = np.zeros((ns_pad, nc_pad), np.float32)
    out[:ns, :nc] = ov
    return jnp.asarray(out)


def _select_blocks(sc, valid, blk, n_cand, n_sel):
    rank = jnp.zeros(sc.shape, F32)
    for j in range(n_cand):
        rj = sc[j:j + 1, :]
        beats = (rj > sc) | ((rj == sc) & (blk > j))
        rank = rank + jnp.where(beats, 1.0, 0.0)
    return jnp.where(valid & (rank < n_sel), 1.0, 0.0)


def _nsa_attn_kernel(q_ref, gt_ref, kc_ref, vc_ref, ks_ref, vs_ref, kw_ref, vw_ref, ovt_ref, e_ref,
                     o_ref, ocmp_ref, msk_ref, m_ref, l_ref, acc_ref, *, tq, t_len, nc, n_sel):
    qi = pl.program_id(1)
    q0 = qi * tq
    scale = NSA_DH ** -0.5
    kt = 256
    ns = t_len // SLC_BLOCK
    lane = lax.broadcasted_iota(jnp.int32, (tq, LANES), 1)
    n_pairs = NSA_HEADS // 2

    def q_pair(hp):
        g = hp // (n_pairs // NSA_G)
        slab = q_ref[hp].astype(F32)
        rolled = pltpu.roll(slab, NSA_DH, 1)
        keep = (lane < NSA_DH) if g == 0 else (lane >= NSA_DH)
        a_src, b_src = (slab, rolled) if g == 0 else (rolled, slab)
        return jnp.concatenate([jnp.where(keep, a_src, 0.0), jnp.where(keep, b_src, 0.0)],
                               axis=0).astype(BF16)

    kc = kc_ref[...]
    vc = vc_ref[...]
    ncp = kc.shape[0]
    cidx = lax.broadcasted_iota(jnp.int32, (2 * tq, ncp), 1)
    qpos_c = q0 + (lax.broadcasted_iota(jnp.int32, (2 * tq, ncp), 0) & (tq - 1))
    cmask = (cidx < nc) & (cidx * CMP_STRIDE + (CMP_LEN - 1) <= qpos_c)
    imp = [jnp.zeros((tq, ncp), F32) for _ in range(NSA_G)]
    for hp in range(n_pairs):
        g = hp // (n_pairs // NSA_G)
        p = _softmax_tile(_dot_nt(q_pair(hp), kc) * scale, cmask)
        ocmp_ref[hp] = _dot(p, vc)
        imp[g] = imp[g] + p[:tq] + p[tq:]

    nsp = ovt_ref.shape[0]
    blk = lax.broadcasted_iota(jnp.int32, (ns, tq), 0)
    cur = (q0 + lax.broadcasted_iota(jnp.int32, (ns, tq), 1)) // SLC_BLOCK
    forced = (blk == 0) | (blk == cur) | (blk == cur - 1)
    valid = blk <= cur
    for g in range(NSA_G):
        sc = _dot_nt_f32(ovt_ref[...], imp[g])[:ns]
        sc = jnp.where(valid, sc + jnp.where(forced, FORCE_BONUS, 0.0), NEG)
        sel_t = _select_blocks(sc, valid, blk, ns, n_sel)
        if nsp > ns:
            sel_t = jnp.concatenate([sel_t, jnp.zeros((nsp - ns, tq), F32)], axis=0)
        sel = sel_t.T
        for jt in range(t_len // kt):
            msk_ref[g, jt] = _dot(sel, e_ref[:, jt * kt:(jt + 1) * kt])

    row2 = lax.broadcasted_iota(jnp.int32, (2 * tq, kt), 0) & (tq - 1)
    col2 = lax.broadcasted_iota(jnp.int32, (2 * tq, kt), 1)
    qpos = q0 + row2
    gt = gt_ref[...]
    for hp in range(n_pairs):
        g = hp // (n_pairs // NSA_G)
        q2 = q_pair(hp)

        _online_init(m_ref, l_ref, acc_ref)

        def slc_body(jt, carry):
            start = pl.multiple_of(jt * kt, kt)
            k = ks_ref[pl.ds(start, kt), :]
            v = vs_ref[pl.ds(start, kt), :]
            mf = msk_ref[g, jt]
            mf2 = jnp.concatenate([mf, mf], axis=0)
            mask = (mf2 > 0.5) & (start + col2 <= qpos)
            _online_update(_dot_nt(q2, k) * scale, mask, v, m_ref, l_ref, acc_ref)
            return carry

        lax.fori_loop(0, (q0 + tq) // kt, slc_body, 0)
        o_slc = _online_result(l_ref, acc_ref)

        _online_init(m_ref, l_ref, acc_ref)
        for d in range(-WINDOW, tq, kt):
            @pl.when(q0 + d >= 0)
            def _():
                start = pl.multiple_of(q0 + d, kt)
                k = kw_ref[pl.ds(start, kt), :]
                v = vw_ref[pl.ds(start, kt), :]
                kpos = start + col2
                mask = (kpos <= qpos) & (kpos > qpos - WINDOW)
                _online_update(_dot_nt(q2, k) * scale, mask, v, m_ref, l_ref, acc_ref)
        o_win = _online_result(l_ref, acc_ref)

        def gate(c):
            col = c * NSA_HEADS + 2 * hp
            return jnp.concatenate([gt[:, col:col + 1], gt[:, col + 1:col + 2]], axis=0)

        o = gate(0) * ocmp_ref[hp] + gate(1) * o_slc + gate(2) * o_win
        o_a, o_b = o[:tq], o[tq:]
        if g == 0:
            out = jnp.where(lane < NSA_DH, o_a, pltpu.roll(o_b, NSA_DH, 1))
        else:
            out = jnp.where(lane < NSA_DH, pltpu.roll(o_a, NSA_DH, 1), o_b)
        o_ref[hp] = out.astype(o_ref.dtype)


def _nsa_attn(q_slab, gates, kc, vc, slc_rows, win_rows, tq=256):
    b, n_pairs, t, _ = q_slab.shape
    tq = min(tq, t)
    n_seg = kc.shape[1]
    nc = n_seg - 1
    ns = t // SLC_BLOCK
    nsp = LANES
    ovt = _overlap_t(nc, ns, n_seg, nsp)
    e = jnp.asarray((np.arange(t)[None, :] // SLC_BLOCK == np.arange(nsp)[:, None]).astype(np.float32), BF16)
    kern = functools.partial(_nsa_attn_kernel, tq=tq, t_len=t, nc=nc, n_sel=min(SLC_TOPN, ns))
    seq = lambda c: pl.BlockSpec((None, t, LANES), lambda i, j: (i, 0, c))
    return pl.pallas_call(
        kern, grid=(b, t // tq),
        in_specs=[pl.BlockSpec((None, n_pairs, tq, LANES), lambda i, j: (i, 0, j, 0)),
                  pl.BlockSpec((None, tq, LANES), lambda i, j: (i, j, 0)),
                  pl.BlockSpec((None, n_seg, LANES), lambda i, j: (i, 0, 0)),
                  pl.BlockSpec((None, n_seg, LANES), lambda i, j: (i, 0, 0)),
                  seq(0), seq(1), seq(0), seq(1),
                  pl.BlockSpec(ovt.shape, lambda i, j: (0, 0)),
                  pl.BlockSpec(e.shape, lambda i, j: (0, 0))],
        out_specs=pl.BlockSpec((None, n_pairs, tq, LANES), lambda i, j: (i, 0, j, 0)),
        out_shape=jax.ShapeDtypeStruct((b, n_pairs, t, LANES), BF16),
        scratch_shapes=[pltpu.VMEM((n_pairs, 2 * tq, LANES), F32),
                        pltpu.VMEM((NSA_G, t // 256, tq, 256), F32),
                        pltpu.VMEM((2 * tq, 1), F32), pltpu.VMEM((2 * tq, 1), F32),
                        pltpu.VMEM((2 * tq, LANES), F32)],
        compiler_params=_cparams('parallel', 'arbitrary'), name='nsa_attn')(
            q_slab, gates, kc, vc, slc_rows, slc_rows, win_rows, win_rows, ovt, e)


def _mlstm_kernel(*refs, L, t_real, has_state, pad):
    it = iter(refs)
    z_ref, gc_ref, gr_ref, bc_ref, br_ref = next(it), next(it), next(it), next(it), next(it)
    if has_state:
        c0_ref, n0_ref, m0_ref = next(it), next(it), next(it)
    h_ref, c_out, n_out, m_out = next(it), next(it), next(it), next(it)
    c_scr, n_scr, m_scr = next(it), next(it), next(it)
    if pad:
        zp_ref, gcp_ref, grp_ref = next(it), next(it), next(it)
    ci = pl.program_id(1)
    nchunks = pl.num_programs(1)

    @pl.when(ci == 0)
    def _():
        if has_state:
            c_scr[...] = c0_ref[...]
            n_scr[...] = n0_ref[...]
            m_scr[...] = m0_ref[...]
        else:
            c_scr[...] = jnp.zeros(c_scr.shape, F32)
            n_scr[...] = jnp.zeros(n_scr.shape, F32)
            m_scr[...] = jnp.zeros(m_scr.shape, F32)

    if pad:
        @pl.when((pl.program_id(0) == 0) & (ci == 0))
        def _():
            zp_ref[...] = jnp.zeros(zp_ref.shape, F32)
            gcp_ref[...] = jnp.zeros(gcp_ref.shape, F32)
            grp_ref[...] = jnp.zeros(grp_ref.shape, F32)

        zp_ref[:, 0:t_real, :] = z_ref[...]
        gcp_ref[0:t_real, :] = gc_ref[...]
        grp_ref[:, 0:t_real] = gr_ref[...]
        z_ref, gc_ref, gr_ref = zp_ref, gcp_ref, grp_ref

    H = ML_HEADS
    gcol = gc_ref[...] + bc_ref[...]
    grow = gr_ref[...] + br_ref[...]
    lf_c = _log_sigmoid(gcol)
    lf_r = _log_sigmoid(grow)
    ig_c, ig_r = gcol, grow
    if t_real < L:
        rv = lax.broadcasted_iota(jnp.int32, gcol.shape, 0) < t_real
        lv = lax.broadcasted_iota(jnp.int32, grow.shape, 1) < t_real
        lf_c, ig_c = jnp.where(rv, lf_c, 0.0), jnp.where(rv, ig_c, NEG)
        lf_r, ig_r = jnp.where(lv, lf_r, 0.0), jnp.where(lv, ig_r, NEG)
    bcum_c = _cumsum(lf_c, 0)
    bcum_r = _cumsum(lf_r, 1)
    tri = lax.broadcasted_iota(jnp.int32, (L, L), 0) >= lax.broadcasted_iota(jnp.int32, (L, L), 1)
    for h in range(H):
        q = z_ref[h]
        k = z_ref[H + h] * (ML_DK ** -0.5)
        v = jnp.concatenate([z_ref[2 * H + 2 * h], z_ref[2 * H + 2 * h + 1]], axis=1)
        og = jnp.concatenate([z_ref[4 * H + 2 * h], z_ref[4 * H + 2 * h + 1]], axis=1)
        bc_t = bcum_c[:, H + h:H + h + 1]
        bc_s = bcum_r[H + h:H + h + 1, :]
        ig_s = ig_r[h:h + 1, :]
        ig_t = ig_c[:, h:h + 1]
        m_old = m_scr[h:h + 1, 0:1]
        dmat = jnp.where(tri, bc_t - bc_s + ig_s, NEG)
        inter = bc_t + m_old
        mt = jnp.maximum(jnp.max(dmat, axis=1, keepdims=True), inter)
        w_intra = jnp.exp(dmat - mt)
        w_state = jnp.exp(inter - mt)
        sc = _dot_nt(q, k) * w_intra
        c_st = c_scr[h]
        n_st = n_scr[h:h + 1, :]
        num = _dot(sc, v) + w_state * _dot_nt(q, c_st)
        den = jnp.sum(sc, axis=1, keepdims=True) + w_state * jnp.sum(q * n_st, axis=1, keepdims=True)
        hh = num / jnp.maximum(jnp.abs(den), jnp.exp(-mt)) * og
        h_ref[2 * h] = hh[0:h_ref.shape[1], 0:LANES].astype(h_ref.dtype)
        h_ref[2 * h + 1] = hh[0:h_ref.shape[1], LANES:].astype(h_ref.dtype)
        bl = bc_s[:, L - 1:L]
        dl_r = bl - bc_s + ig_s
        dl_t = bl - bc_t + ig_t
        m_new = jnp.maximum(bl + m_old, jnp.max(dl_r, axis=1, keepdims=True))
        ws_t = jnp.exp(dl_t - m_new)
        wc = jnp.exp(bl + m_old - m_new)
        c_scr[h] = wc * c_st + _dot((v * ws_t).T, k)
        n_scr[h:h + 1, :] = wc * n_st + jnp.sum(ws_t * k, axis=0, keepdims=True)
        m_scr[h:h + 1, :] = jnp.broadcast_to(m_new, (1, LANES))

    @pl.when(ci == nchunks - 1)
    def _():
        c_out[...] = c_scr[...]
        n_out[...] = n_scr[...]
        m_out[...] = m_scr[...]


def _mlstm(z_slab, gates_col, gates_row, gate_b, state=None, L=256):
    b, ns, t, _ = z_slab.shape
    pad = t < 8
    L = 128 if pad else min(L, t)
    lr = t if pad else L
    nch = 1 if pad else t // L
    H = ML_HEADS
    bcol = jnp.zeros((1, LANES), F32).at[0, :2 * H].set(gate_b.reshape(-1).astype(F32))
    brow = gate_b.reshape(2 * H, 1).astype(F32)
    specs = [pl.BlockSpec((None, ns, lr, LANES), lambda i, c: (i, 0, c, 0)),
             pl.BlockSpec((None, lr, LANES), lambda i, c: (i, c, 0)),
             pl.BlockSpec((None, 2 * H, lr), lambda i, c: (i, 0, c)),
             pl.BlockSpec((1, LANES), lambda i, c: (0, 0)),
             pl.BlockSpec((2 * H, 1), lambda i, c: (0, 0))]
    args = [z_slab, gates_col, gates_row, bcol, brow]
    if state is not None:
        c0, n0, m0 = state
        n0p = jnp.zeros((b, 8, LANES), F32).at[:, :H].set(n0.astype(F32))
        m0p = jnp.zeros((b, 8, LANES), F32).at[:, :H].set(jnp.broadcast_to(m0.astype(F32)[..., None], (b, H, LANES)))
        specs += [pl.BlockSpec((None, H, ML_DV, ML_DK), lambda i, c: (i, 0, 0, 0)),
                  pl.BlockSpec((None, 8, LANES), lambda i, c: (i, 0, 0)),
                  pl.BlockSpec((None, 8, LANES), lambda i, c: (i, 0, 0))]
        args += [c0.astype(F32), n0p, m0p]
    scratch = [pltpu.VMEM((H, ML_DV, ML_DK), F32), pltpu.VMEM((8, LANES), F32), pltpu.VMEM((8, LANES), F32)]
    if pad:
        scratch += [pltpu.VMEM((ns, L, LANES), F32), pltpu.VMEM((L, LANES), F32), pltpu.VMEM((2 * H, L), F32)]
    kern = functools.partial(_mlstm_kernel, L=L, t_real=t if pad else L, has_state=state is not None, pad=pad)
    h_slab, c_f, n_f, m_f = pl.pallas_call(
        kern, grid=(b, nch), in_specs=specs,
        out_specs=[pl.BlockSpec((None, 2 * H, lr, LANES), lambda i, c: (i, 0, c, 0)),
                   pl.BlockSpec((None, H, ML_DV, ML_DK), lambda i, c: (i, 0, 0, 0)),
                   pl.BlockSpec((None, 8, LANES), lambda i, c: (i, 0, 0)),
                   pl.BlockSpec((None, 8, LANES), lambda i, c: (i, 0, 0))],
        out_shape=[jax.ShapeDtypeStruct((b, 2 * H, t, LANES), BF16),
                   jax.ShapeDtypeStruct((b, H, ML_DV, ML_DK), F32),
                   jax.ShapeDtypeStruct((b, 8, LANES), F32), jax.ShapeDtypeStruct((b, 8, LANES), F32)],
        scratch_shapes=scratch, compiler_params=_cparams('arbitrary', 'arbitrary'), name='mlstm')(*args)
    return h_slab, c_f, n_f[:, :H], m_f[:, :H, 0]


def _hgrn_levels(L):
    n_lev = int(math.log2(L))
    t = np.arange(L)
    pall = np.zeros((n_lev * L, L), np.float32)
    lmask = np.zeros((n_lev, L, L), np.float32)
    for lev in range(n_lev):
        w = L >> lev
        mid = (t // w) * w + w // 2
        pall[lev * L + t, mid - 1] = 1.0
        same = (t[:, None] // w) == (t[None, :] // w)
        lmask[lev] = same & ((t[:, None] % w) >= w // 2) & ((t[None, :] % w) < w // 2)
    return jnp.asarray(pall), jnp.asarray(lmask)


def _hgrn_kernel(*refs, L, t_real, has_state, pad):
    it = iter(refs)
    z_ref, lf_ref, gn_ref, pall_ref, lmask_ref = next(it), next(it), next(it), next(it), next(it)
    if has_state:
        s0_ref = next(it)
    o_ref, s_out = next(it), next(it)
    st_scr = next(it)
    if pad:
        zp_ref = next(it)
    ci = pl.program_id(1)
    nchunks = pl.num_programs(1)
    H = HG_HEADS
    n_lev = lmask_ref.shape[0]

    @pl.when(ci == 0)
    def _():
        for h in range(H):
            st_scr[h] = s0_ref[h].T if has_state else jnp.zeros((HG_DV, HG_DK), F32)

    if pad:
        @pl.when((pl.program_id(0) == 0) & (ci == 0))
        def _():
            zp_ref[...] = jnp.zeros(zp_ref.shape, F32)

        zp_ref[:, 0:t_real, :] = z_ref[...]
        z_ref = zp_ref

    rows = lax.broadcasted_iota(jnp.int32, (L, LANES), 0)
    eye = lax.broadcasted_iota(jnp.int32, (L, L), 0) == lax.broadcasted_iota(jnp.int32, (L, L), 1)
    gn = gn_ref[...]

    def head(h, carry):
        q = z_ref[h]
        zf = z_ref[H + h]
        v = z_ref[2 * H + h]
        gate = z_ref[3 * H + h]
        lower = lf_ref[pl.ds(h, 1), :]
        fb = lf_ref[pl.ds(H + h, 1), :]
        f = lower + (1.0 - lower) * _sigmoid(zf + fb)
        lf = jnp.log(f)
        k = 1.0 - f
        if t_real < L:
            lf = jnp.where(rows < t_real, lf, 0.0)
            k = jnp.where(rows < t_real, k, 0.0)
        bcum = _cumsum(lf, 0)
        refs_all = _dot_f32(pall_ref[...], bcum)
        att = jnp.where(eye, jnp.sum(q * k, axis=1, keepdims=True), 0.0)
        for lev in range(n_lev):
            r = refs_all[lev * L:(lev + 1) * L]
            qt = q * jnp.exp(jnp.minimum(bcum - r, 0.0))
            kt = k * jnp.exp(jnp.minimum(r - bcum, 0.0))
            att = att + _dot_nt(qt, kt) * lmask_ref[lev]
        s_t = st_scr[h]
        o = _dot(att, v) + _dot_nt(q * jnp.exp(bcum), s_t)
        bl = bcum[L - 1:L, :]
        st_scr[h] = jnp.exp(bl) * s_t + _dot(v.T, k * jnp.exp(bl - bcum))
        on = _rms(o, gn) * _silu(gate)
        o_ref[h] = on[0:o_ref.shape[1]].astype(o_ref.dtype)
        return carry

    lax.fori_loop(0, H, head, 0)

    @pl.when(ci == nchunks - 1)
    def _():
        for h in range(H):
            s_out[h] = st_scr[h].T


def _hgrn(z_slab, lower, f_b, g_norm, state=None, L=128):
    b, ns, t, _ = z_slab.shape
    pad = t < 8
    L = L if pad else min(L, t)
    lr = t if pad else L
    nch = 1 if pad else t // L
    H = HG_HEADS
    pall, lmask = _hgrn_levels(L)
    lowfb = jnp.concatenate([lower.reshape(H, HG_DK), f_b.reshape(H, HG_DK)], axis=0).astype(F32)
    specs = [pl.BlockSpec((None, ns, lr, LANES), lambda i, c: (i, 0, c, 0)),
             pl.BlockSpec((2 * H, LANES), lambda i, c: (0, 0)),
             pl.BlockSpec((1, LANES), lambda i, c: (0, 0)),
             pl.BlockSpec(pall.shape, lambda i, c: (0, 0)),
             pl.BlockSpec(lmask.shape, lambda i, c: (0, 0, 0))]
    args = [z_slab, lowfb, g_norm.reshape(1, HG_DV).astype(F32), pall, lmask]
    if state is not None:
        specs.append(pl.BlockSpec((None, H, HG_DK, HG_DV), lambda i, c: (i, 0, 0, 0)))
        args.append(state.astype(F32))
    scratch = [pltpu.VMEM((H, HG_DV, HG_DK), F32)]
    if pad:
        scratch.append(pltpu.VMEM((ns, L, LANES), F32))
    kern = functools.partial(_hgrn_kernel, L=L, t_real=t if pad else L, has_state=state is not None, pad=pad)
    return pl.pallas_call(
        kern, grid=(b, nch), in_specs=specs,
        out_specs=[pl.BlockSpec((None, H, lr, LANES), lambda i, c: (i, 0, c, 0)),
                   pl.BlockSpec((None, H, HG_DK, HG_DV), lambda i, c: (i, 0, 0, 0))],
        out_shape=[jax.ShapeDtypeStruct((b, H, t, LANES), BF16),
                   jax.ShapeDtypeStruct((b, H, HG_DK, HG_DV), F32)],
        scratch_shapes=scratch, compiler_params=_cparams('arbitrary', 'arbitrary'), name='hgrn')(*args)


def _mla_attn_kernel(ql_ref, qr_ref, lc_ref, lr_ref, o_ref, m_ref, l_ref, acc_ref, *, tq):
    qi = pl.program_id(1)
    q0 = qi * tq
    kt = 256
    hg_n = 4
    scale = (MLA_NOPE + MLA_ROPE) ** -0.5
    lane = lax.broadcasted_iota(jnp.int32, (tq, LANES), 1)
    rows = lax.broadcasted_iota(jnp.int32, (hg_n * tq, kt), 0) & (tq - 1)
    cols = lax.broadcasted_iota(jnp.int32, (hg_n * tq, kt), 1)
    per_slab = LANES // MLA_ROPE
    for hg in range(MLA_HEADS // hg_n):
        heads = [hg * hg_n + i for i in range(hg_n)]
        q_lat = jnp.concatenate([ql_ref[:, h * MLA_KV_LORA:(h + 1) * MLA_KV_LORA] for h in heads], axis=0)
        parts = []
        for h in heads:
            slab = qr_ref[:, (h // per_slab) * LANES:(h // per_slab + 1) * LANES].astype(F32)
            sh = (LANES - (h % per_slab) * MLA_ROPE) % LANES
            parts.append(jnp.where(lane < MLA_ROPE, pltpu.roll(slab, sh, 1) if sh else slab, 0.0))
        q_rope = jnp.concatenate(parts, axis=0).astype(BF16)
        _online_init(m_ref, l_ref, acc_ref)

        def tile(j, masked):
            start = pl.multiple_of(j * kt, kt)
            kc = lc_ref[pl.ds(start, kt), :].astype(BF16)
            kr = lr_ref[pl.ds(start, kt), :]
            s = (_dot_nt(q_lat, kc) + _dot_nt(q_rope, kr)) * scale
            mask = (start + cols <= q0 + rows) if masked else None
            _online_update(s, mask, kc, m_ref, l_ref, acc_ref)

        n_full = q0 // kt

        def body(j, carry):
            tile(j, False)
            return carry

        lax.fori_loop(0, n_full, body, 0)
        tile(n_full, True)
        o = _online_result(l_ref, acc_ref)
        for i, h in enumerate(heads):
            o_ref[:, h * MLA_KV_LORA:(h + 1) * MLA_KV_LORA] = o[i * tq:(i + 1) * tq].astype(o_ref.dtype)


def _mla_attn(q_lat, q_rope, lat, b, t, tq=128):
    tq = min(tq, t)
    wl = MLA_HEADS * MLA_KV_LORA
    wr = MLA_HEADS * MLA_ROPE
    lat3 = lat.reshape(b, t, lat.shape[-1])
    c_blk = MLA_Q_LORA // MLA_KV_LORA
    r_blk = (MLA_Q_LORA + MLA_KV_LORA) // LANES
    return pl.pallas_call(
        functools.partial(_mla_attn_kernel, tq=tq), grid=(b, t // tq),
        in_specs=[pl.BlockSpec((None, tq, wl), lambda i, j: (i, j, 0)),
                  pl.BlockSpec((None, tq, wr), lambda i, j: (i, j, 0)),
                  pl.BlockSpec((None, t, MLA_KV_LORA), lambda i, j: (i, 0, c_blk)),
                  pl.BlockSpec((None, t, LANES), lambda i, j: (i, 0, r_blk))],
        out_specs=pl.BlockSpec((None, tq, wl), lambda i, j: (i, j, 0)),
        out_shape=jax.ShapeDtypeStruct((b, t, wl), BF16),
        scratch_shapes=[pltpu.VMEM((4 * tq, 1), F32), pltpu.VMEM((4 * tq, 1), F32),
                        pltpu.VMEM((4 * tq, MLA_KV_LORA), F32)],
        compiler_params=_cparams('parallel', 'arbitrary'), name='mla_attn')(
            q_lat.reshape(b, t, wl), q_rope.reshape(b, t, wr), lat3, lat3).reshape(b * t, wl)


def _pad_cols(w, n):
    return jnp.pad(w, ((0, 0), (0, n - w.shape[1])))


def _prep_nsa(w_in, pe, w1, w2, w_out):
    nq = NSA_HEADS * NSA_DH
    kvw = 2 * NSA_G * NSA_DH
    w2 = w2.astype(BF16)
    z = jnp.zeros_like(w2)
    w2p = jnp.stack([jnp.concatenate([w2, z], axis=-1), jnp.concatenate([z, w2], axis=-1)], axis=1)
    return dict(q=w_in[:, :nq].astype(BF16), kv=w_in[:, nq:nq + 3 * kvw].astype(BF16),
                g=_pad_cols(w_in[:, nq + 3 * kvw:], LANES).astype(BF16),
                pe=pe.astype(F32), w1=w1.astype(BF16), w2p=w2p, out=w_out.astype(BF16))


def _prep_ml(w_in, gate_b, w_out):
    a = 2 * ML_HEADS * ML_DK + ML_HEADS * ML_DV
    main = jnp.concatenate([w_in[:, :a], w_in[:, a + 2 * ML_HEADS:]], axis=1)
    return dict(main=main.astype(BF16), gate=_pad_cols(w_in[:, a:a + 2 * ML_HEADS], LANES).astype(BF16),
                gate_b=gate_b, out=w_out.astype(BF16))


def _prep_mla(w_in, q_norm, kv_norm, w_uq, w_uk, w_uv, w_out):
    uq = w_uq.reshape(MLA_Q_LORA, MLA_HEADS, MLA_NOPE + MLA_ROPE)
    return dict(w_in=_pad_cols(w_in, 7 * LANES).astype(BF16),
                q_norm=q_norm.reshape(1, -1).astype(F32), kv_norm=kv_norm.reshape(1, -1).astype(F32),
                uq_nope=uq[:, :, :MLA_NOPE].reshape(MLA_Q_LORA, -1).astype(BF16),
                uq_rope=uq[:, :, MLA_NOPE:].reshape(MLA_Q_LORA, -1).astype(BF16),
                ukt=jnp.transpose(w_uk, (1, 2, 0)).astype(BF16),
                uv=jnp.transpose(w_uv, (1, 0, 2)).astype(BF16), out=w_out.astype(BF16))


def _sig_tile2(z, j):
    return jnp.where(j == 2, _sigmoid(z), z)


def _rope64_epi(z, j, c, s):
    return _rope_tile(z, c, s, NSA_DH // 2)


def _rope_kv_epi(z, j, c, s):
    return jnp.concatenate([_rope_tile(z[:, :LANES], c, s, NSA_DH // 2), z[:, LANES:]], axis=1)


def _rope32_epi(z, j, c, s):
    return _rope_tile(z, c, s, MLA_ROPE // 2)


def _sigmoid_epi(z, j):
    return _sigmoid(z)


def _mla_in_epi(z, j, qn, kvn, c, s):
    a, bnd = MLA_Q_LORA, MLA_Q_LORA + MLA_KV_LORA
    return jnp.concatenate([_rms(z[:, :a], qn), _rms(z[:, a:bnd], kvn),
                            _rope_tile(z[:, bnd:], c, s, MLA_ROPE // 2)], axis=1)


def _const_aux(a):
    return (a, a.shape, lambda i, j: (0,) * a.ndim)


def _nsa_fresh(h, g, w, b, t, tabs):
    cos, sin = tabs[NSA_DH // 2]
    tm = min(1024, t)
    rope = _rope_aux(cos, sin, tm, t // tm)
    q = _mm(h, w['q'], g=g, epi=_rope64_epi, aux=rope, out_dtype=BF16, layout='slab', seq=t)
    kv = _mm(h, w['kv'], g=g, epi=_rope_kv_epi, aux=rope, layout='tiles', seq=t, tn=256)
    gates = _mm(h, w['g'], g=g, epi=_sigmoid_epi, seq=t)
    rows = [kv[i].reshape(b, t, 256) for i in range(3)]
    kc, vc = _nsa_compress(rows[0], w['pe'], w['w1'], w['w2p'])
    o = _nsa_attn(q, gates.reshape(b, t, LANES), kc, vc, rows[1], rows[2])
    h = _mm(o, w['out'], res=h, x_slab_seq=t)
    return h, rows


def _ml_fresh(h, g, w, b, t):
    z = _mm(h, w['main'], g=g, epi=_sig_tile2, layout='slab', seq=t)
    gc = _mm(h, w['gate'], g=g, seq=t).reshape(b, t, LANES)
    gr = jnp.swapaxes(gc[:, :, :2 * ML_HEADS], 1, 2)
    hs, c_f, n_f, m_f = _mlstm(z, gc, gr, w['gate_b'])
    return _mm(hs, w['out'], res=h, x_slab_seq=t), (c_f, n_f, m_f)


def _mla_fresh(h, g, w, b, t, tabs):
    cos, sin = tabs[MLA_ROPE // 2]
    tm = min(1024, t)
    rope = _rope_aux(cos, sin, tm, t // tm)
    lat = _mm(h, w['w_in'], g=g, epi=_mla_in_epi,
              aux=[_const_aux(w['q_norm']), _const_aux(w['kv_norm'])] + rope, seq=t)
    qn = _mm(lat, w['uq_nope'], x_cols=(MLA_Q_LORA, 0), out_dtype=BF16, seq=t)
    qr = _mm(lat, w['uq_rope'], x_cols=(MLA_Q_LORA, 0), epi=_rope32_epi, aux=rope, out_dtype=BF16, seq=t)
    ql = _headmm(qn, w['ukt'])
    ol = _mla_attn(ql, qr, lat, b, t)
    o = _headmm(ol, w['uv'])
    h = _mm(o, w['out'], res=h)
    new_lat = lat[:, MLA_Q_LORA:MLA_Q_LORA + MLA_KV_LORA + MLA_ROPE].reshape(b, t, -1)
    return h, new_lat


def _hg_fresh(h, g, w_in, lower, f_b, g_norm, w_out, b, t):
    z = _mm(h, w_in, g=g, layout='slab', seq=t)
    os_, s_f = _hgrn(z, lower, f_b, g_norm)
    return _mm(os_, w_out, res=h, x_slab_seq=t), s_f


PAGES_PER_STEP = 8


def _page_specs(width, n):
    return [pl.BlockSpec((None, PAGE, width), functools.partial(
        lambda i, s, pt, k: (pt[i, s * n + k], 0, 0), k=k)) for k in range(n)]


def _stack8(x):
    return jnp.concatenate([x] * (NSA_HEADS // NSA_G), axis=0)


def _nsa_cmp_past_kernel(pt_ref, *refs, n_pg, t, start, nc):
    pages = refs[:n_pg]
    q_ref, pe_ref, w1_ref, w2p_ref, ocmp_ref, imp_ref, k_scr, v_scr = refs[n_pg:]
    s_i = pl.program_id(1)
    for k in range(n_pg):
        off = pl.multiple_of((s_i * n_pg + k) * PAGE, PAGE)
        k_scr[pl.ds(off, PAGE), :] = pages[k][:, 0:LANES]
        v_scr[pl.ds(off, PAGE), :] = pages[k][:, LANES:2 * LANES]

    @pl.when(s_i == pl.num_programs(1) - 1)
    def _():
        n_seg = k_scr.shape[0] // CMP_STRIDE
        kc, vc = _compress_rows(k_scr, v_scr, n_seg, pe_ref, w1_ref, w2p_ref)
        q = q_ref[...]
        r_n = q.shape[0]
        cidx = lax.broadcasted_iota(jnp.int32, (r_n, n_seg), 1)
        qpos = start + lax.rem(lax.broadcasted_iota(jnp.int32, (r_n, n_seg), 0), t)
        mask = (cidx < nc) & (cidx * CMP_STRIDE + (CMP_LEN - 1) <= qpos)
        p = _softmax_tile(_dot_nt(q, kc) * (NSA_DH ** -0.5), mask)
        ocmp_ref[...] = _dot(p, vc)
        gt = NSA_G * t
        imp = p[0:gt]
        for hh in range(1, r_n // gt):
            imp = imp + p[hh * gt:(hh + 1) * gt]
        imp_ref[...] = imp


def _nsa_cmp_past(page_table, cache, q_ext, w, t, start):
    b, npg = page_table.shape
    n_pg = min(PAGES_PER_STEP, npg)
    r_n = q_ext.shape[1]
    n_seg = npg * PAGE // CMP_STRIDE
    nc = (start + t) // CMP_STRIDE - CMP_LEN // CMP_STRIDE + 1
    full = lambda a: pl.BlockSpec(a.shape, lambda i, s, pt: (0,) * a.ndim)
    gs = pltpu.PrefetchScalarGridSpec(
        num_scalar_prefetch=1, grid=(b, npg // n_pg),
        in_specs=_page_specs(256, n_pg) + [pl.BlockSpec((None, r_n, LANES), lambda i, s, pt: (i, 0, 0)),
                                           full(w['pe']), full(w['w1']), full(w['w2p'])],
        out_specs=[pl.BlockSpec((None, r_n, LANES), lambda i, s, pt: (i, 0, 0)),
                   pl.BlockSpec((None, NSA_G * t, n_seg), lambda i, s, pt: (i, 0, 0))],
        scratch_shapes=[pltpu.VMEM((npg * PAGE, LANES), F32), pltpu.VMEM((npg * PAGE, LANES), F32)])
    return pl.pallas_call(
        functools.partial(_nsa_cmp_past_kernel, n_pg=n_pg, t=t, start=start, nc=nc), grid_spec=gs,
        out_shape=[jax.ShapeDtypeStruct((b, r_n, LANES), F32),
                   jax.ShapeDtypeStruct((b, NSA_G * t, n_seg), F32)],
        compiler_params=_cparams('arbitrary', 'arbitrary'), name='nsa_cmp_past')(
            page_table, *([cache] * n_pg), q_ext, w['pe'], w['w1'], w['w2p']), nc


def _nsa_select_kernel(imp_ref, ovt_ref, pos_ref, sel_ref, sc_scr, rank_scr, *, ns, n_sel):
    shape = sc_scr.shape
    blk = lax.broadcasted_iota(jnp.int32, shape, 0)
    cur = pos_ref[...] // SLC_BLOCK
    forced = (blk == 0) | (blk == cur) | (blk == cur - 1)
    valid = blk <= cur
    sc = _dot_nt_f32(ovt_ref[...], imp_ref[...])
    sc_scr[...] = jnp.where(valid, sc + jnp.where(forced, FORCE_BONUS, 0.0), NEG)
    rank_scr[...] = jnp.zeros(shape, F32)

    def body(j, carry):
        sc_all = sc_scr[...]
        rj = sc_scr[pl.ds(j, 1), :]
        beats = (rj > sc_all) | ((rj == sc_all) & (blk > j))
        rank_scr[...] += jnp.where(beats, 1.0, 0.0)
        return carry

    lax.fori_loop(0, ns, body, 0)
    sel_ref[...] = jnp.where(valid & (rank_scr[...] < n_sel), 1.0, 0.0)


def _nsa_select_past(imp, nc, t, start):
    rows, n_seg = imp.shape
    ns = -(-(start + t) // SLC_BLOCK)
    nsp = -(-ns // 8) * 8
    ovt = _overlap_t(nc, ns, n_seg, nsp)
    pos = (start + jnp.arange(rows, dtype=jnp.int32) % t).reshape(1, rows)
    return pl.pallas_call(
        functools.partial(_nsa_select_kernel, ns=ns, n_sel=min(SLC_TOPN, ns)),
        out_shape=jax.ShapeDtypeStruct((nsp, rows), F32),
        scratch_shapes=[pltpu.VMEM((nsp, rows), F32), pltpu.VMEM((nsp, rows), F32)],
        compiler_params=pltpu.CompilerParams(vmem_limit_bytes=VMEM_LIMIT), name='nsa_select')(imp, ovt, pos)


def _nsa_slcwin_past_kernel(pt_ref, *refs, n_pg, t, wl):
    pages = refs[:n_pg]
    (msk_ref, q_ref, ns_ref, nw_ref, wb_ref, nf_ref, gt_ref, ocmp_ref, o_ref,
     m_ref, l_ref, acc_ref, oslc_scr, pad_scr) = refs[n_pg:]
    s_i = pl.program_id(1)
    scale = NSA_DH ** -0.5
    q = q_ref[...]
    r_n = q.shape[0]

    @pl.when(s_i == 0)
    def _():
        _online_init(m_ref, l_ref, acc_ref)

    ks = [pg[:, 0:LANES] for pg in pages]
    s = jnp.concatenate([_dot_nt(q, k) * scale for k in ks], axis=1)
    v = jnp.concatenate([pg[:, LANES:2 * LANES] for pg in pages], axis=0)
    mask = _stack8(msk_ref[...].astype(F32)) > 0.5
    _online_update(s, mask, v, m_ref, l_ref, acc_ref)

    @pl.when(s_i == pl.num_programs(1) - 1)
    def _():
        col = lax.broadcasted_iota(jnp.int32, (r_n, PAGE), 1)
        tq = lax.rem(lax.broadcasted_iota(jnp.int32, (r_n, PAGE), 0), t)
        new_ok = (col < t) & (col <= tq)

        def padded(ref):
            pad_scr[...] = jnp.zeros(pad_scr.shape, F32)
            pad_scr[0:t, :] = ref[...]
            return pad_scr[...]

        rows = padded(ns_ref)
        _online_update(_dot_nt(q, rows[:, 0:LANES]) * scale, new_ok & (_stack8(nf_ref[...]) > 0.5),
                       rows[:, LANES:], m_ref, l_ref, acc_ref)
        oslc_scr[...] = _online_result(l_ref, acc_ref)

        _online_init(m_ref, l_ref, acc_ref)
        wb = wb_ref[...]
        colw = lax.broadcasted_iota(jnp.int32, (r_n, wl), 1)
        tqw = lax.rem(lax.broadcasted_iota(jnp.int32, (r_n, wl), 0), t)
        _online_update(_dot_nt(q, wb[:, 0:LANES]) * scale, colw > tqw + (wl - WINDOW), wb[:, LANES:],
                       m_ref, l_ref, acc_ref)
        rows = padded(nw_ref)
        _online_update(_dot_nt(q, rows[:, 0:LANES]) * scale, new_ok, rows[:, LANES:], m_ref, l_ref, acc_ref)
        o_win = _online_result(l_ref, acc_ref)
        gt = gt_ref[...]
        o_ref[...] = gt[:, 0:1] * ocmp_ref[...] + gt[:, 1:2] * oslc_scr[...] + gt[:, 2:3] * o_win


def _nsa_slcwin_past(page_table, cache, key_mask, q_ext, new_slc, new_win, win_buf, new_flag, gates_r, ocmp, t):
    b, npg = page_table.shape
    n_pg = min(PAGES_PER_STEP, npg)
    r_n = q_ext.shape[1]
    gtn = NSA_G * t
    wl = win_buf.shape[1]
    per_b = lambda shp: pl.BlockSpec((None,) + shp, lambda i, s, pt: (i,) + (0,) * len(shp))
    gs = pltpu.PrefetchScalarGridSpec(
        num_scalar_prefetch=1, grid=(b, npg // n_pg),
        in_specs=_page_specs(256, n_pg) + [
            pl.BlockSpec((None, gtn, n_pg * PAGE), lambda i, s, pt: (i, 0, s)),
            per_b((r_n, LANES)), per_b((t, 256)), per_b((t, 256)), per_b((wl, 256)),
            per_b((gtn, LANES)), per_b((r_n, LANES)), per_b((r_n, LANES))],
        out_specs=per_b((r_n, LANES)),
        scratch_shapes=[pltpu.VMEM((r_n, 1), F32), pltpu.VMEM((r_n, 1), F32), pltpu.VMEM((r_n, LANES), F32),
                        pltpu.VMEM((r_n, LANES), F32), pltpu.VMEM((PAGE, 256), F32)])
    return pl.pallas_call(
        functools.partial(_nsa_slcwin_past_kernel, n_pg=n_pg, t=t, wl=wl), grid_spec=gs,
        out_shape=jax.ShapeDtypeStruct((b, r_n, LANES), F32),
        compiler_params=_cparams('arbitrary', 'arbitrary'), name='nsa_slcwin_past')(
            page_table, *([cache] * n_pg), key_mask, q_ext, new_slc, new_win, win_buf, new_flag, gates_r, ocmp)


def _mla_past_kernel(pt_ref, *refs, n_pg, t):
    pages = refs[:n_pg]
    ql_ref, qr_ref, new_ref, o_ref, m_ref, l_ref, acc_ref, pad_scr = refs[n_pg:]
    s_i = pl.program_id(1)
    scale = (MLA_NOPE + MLA_ROPE) ** -0.5
    ql = ql_ref[...]
    qr = qr_ref[...]
    r_n = ql.shape[0]

    @pl.when(s_i == 0)
    def _():
        _online_init(m_ref, l_ref, acc_ref)

    def scores(rows):
        kc = rows[:, 0:MLA_KV_LORA].astype(BF16)
        kr = rows[:, MLA_KV_LORA:MLA_KV_LORA + MLA_ROPE]
        return (_dot_nt(ql, kc) + _dot_nt(qr, kr)) * scale, kc

    parts = [scores(pg[...]) for pg in pages]
    _online_update(jnp.concatenate([p[0] for p in parts], axis=1), None,
                   jnp.concatenate([p[1] for p in parts], axis=0), m_ref, l_ref, acc_ref)

    @pl.when(s_i == pl.num_programs(1) - 1)
    def _():
        pad_scr[...] = jnp.zeros(pad_scr.shape, F32)
        pad_scr[0:t, :] = new_ref[...]
        s, kc = scores(pad_scr[...])
        col = lax.broadcasted_iota(jnp.int32, (r_n, PAGE), 1)
        tq = lax.rem(lax.broadcasted_iota(jnp.int32, (r_n, PAGE), 0), t)
        _online_update(s, (col < t) & (col <= tq), kc, m_ref, l_ref, acc_ref)
        o_ref[...] = _online_result(l_ref, acc_ref).astype(o_ref.dtype)


def _mla_past_attn(page_table, cache, ql, qr, new_lat, t):
    b, npg = page_table.shape
    n_pg = min(PAGES_PER_STEP, npg)
    r_n = ql.shape[1]
    width = cache.shape[-1]
    per_b = lambda shp: pl.BlockSpec((None,) + shp, lambda i, s, pt: (i,) + (0,) * len(shp))
    gs = pltpu.PrefetchScalarGridSpec(
        num_scalar_prefetch=1, grid=(b, npg // n_pg),
        in_specs=_page_specs(width, n_pg) + [per_b((r_n, MLA_KV_LORA)), per_b((r_n, MLA_ROPE)), per_b((t, width))],
        out_specs=per_b((r_n, MLA_KV_LORA)),
        scratch_shapes=[pltpu.VMEM((r_n, 1), F32), pltpu.VMEM((r_n, 1), F32), pltpu.VMEM((r_n, MLA_KV_LORA), F32),
                        pltpu.VMEM((PAGE, width), F32)])
    return pl.pallas_call(
        functools.partial(_mla_past_kernel, n_pg=n_pg, t=t), grid_spec=gs,
        out_shape=jax.ShapeDtypeStruct((b, r_n, MLA_KV_LORA), BF16),
        compiler_params=_cparams('arbitrary', 'arbitrary'), name='mla_past')(
            page_table, *([cache] * n_pg), ql, qr, new_lat)


def _to_slab(z, b, t):
    return jnp.transpose(z.reshape(b, t, -1, LANES), (0, 2, 1, 3))


def _from_slab(s, b, t):
    return jnp.transpose(s, (0, 2, 1, 3)).reshape(b * t, -1)


def _nsa_past(h, g, w, b, t, tabs, start, past, occ):
    assert t < CMP_STRIDE and start % SLC_BLOCK == 0
    cos, sin = tabs[NSA_DH // 2]
    m = b * t
    rope = _rope_aux(cos, sin, min(1024, m), None)
    q = _mm(h, w['q'], g=g, epi=_rope64_epi, aux=rope, out_dtype=BF16)
    kv = _mm(h, w['kv'], g=g, epi=_rope_kv_epi, aux=rope, layout='tiles', tn=256)
    gates = _mm(h, w['g'], g=g, epi=_sigmoid_epi)
    rows = [kv[i].reshape(b, t, 256) for i in range(3)]
    hpg = NSA_HEADS // NSA_G
    r_n = hpg * NSA_G * t
    q5 = jnp.transpose(q.reshape(b, t, NSA_G, hpg, NSA_DH), (0, 3, 2, 1, 4))
    eye = jnp.eye(NSA_G, dtype=q5.dtype)
    q_ext = (q5[:, :, :, :, None, :] * eye[None, None, :, None, :, None]).reshape(b, r_n, NSA_G * NSA_DH)
    pt = past['page_table']
    width = 2 * NSA_G * NSA_DH
    cmp_cache = past['nsa_cmp'][occ].reshape(-1, PAGE, width)
    slc_cache = past['nsa_slc'][occ].reshape(-1, PAGE, width)
    win_buf = past['nsa_win'][occ].reshape(b, -1, width)
    (ocmp, imp), nc = _nsa_cmp_past(pt, cmp_cache, q_ext, w, t, start)
    gtn = NSA_G * t
    sel = _nsa_select_past(imp.reshape(b * gtn, -1), nc, t, start)
    sel = jnp.transpose(sel).reshape(b, gtn, -1)
    n_past_blk = start // SLC_BLOCK
    key_mask = jnp.repeat(sel[:, :, :n_past_blk], SLC_BLOCK, axis=-1).astype(BF16)
    new_flag = jnp.broadcast_to(sel[:, :, n_past_blk:n_past_blk + 1], (b, gtn, LANES))
    g5 = jnp.transpose(gates[:, :3 * NSA_HEADS].reshape(b, t, 3, NSA_G, hpg), (0, 4, 3, 1, 2))
    gates_r = jnp.pad(g5.reshape(b, r_n, 3), ((0, 0), (0, 0), (0, LANES - 3)))
    o = _nsa_slcwin_past(pt, slc_cache, key_mask, q_ext, rows[1], rows[2], win_buf, new_flag, gates_r, ocmp, t)
    o5 = o.reshape(b, hpg, NSA_G, t, NSA_G, NSA_DH)
    o = jnp.stack([o5[:, :, gi, :, gi, :] for gi in range(NSA_G)], axis=2)
    o = jnp.transpose(o, (0, 3, 2, 1, 4)).reshape(m, NSA_HEADS * NSA_DH).astype(BF16)
    h = _mm(o, w['out'], res=h)
    win = jnp.concatenate([win_buf, rows[2]], axis=1)[:, -win_buf.shape[1]:]
    return h, rows, win


def _ml_past(h, g, w, b, t, past, occ):
    z = _to_slab(_mm(h, w['main'], g=g, epi=_sig_tile2), b, t)
    gc = _mm(h, w['gate'], g=g).reshape(b, t, LANES)
    gr = jnp.swapaxes(gc[:, :, :2 * ML_HEADS], 1, 2)
    hs, c_f, n_f, m_f = _mlstm(z, gc, gr, w['gate_b'],
                               state=(past['ml_C'][occ], past['ml_n'][occ], past['ml_m'][occ]))
    return _mm(_from_slab(hs, b, t), w['out'], res=h), (c_f, n_f, m_f)


def _mla_past(h, g, w, b, t, tabs, past, occ):
    cos, sin = tabs[MLA_ROPE // 2]
    m = b * t
    rope = _rope_aux(cos, sin, min(1024, m), None)
    lat = _mm(h, w['w_in'], g=g, epi=_mla_in_epi,
              aux=[_const_aux(w['q_norm']), _const_aux(w['kv_norm'])] + rope)
    qn = _mm(lat, w['uq_nope'], x_cols=(MLA_Q_LORA, 0), out_dtype=BF16)
    qr = _mm(lat, w['uq_rope'], x_cols=(MLA_Q_LORA, 0), epi=_rope32_epi, aux=rope, out_dtype=BF16)
    ql = _headmm(qn, w['ukt'])
    new_lat = lat[:, MLA_Q_LORA:MLA_Q_LORA + MLA_KV_LORA + MLA_ROPE].reshape(b, t, -1)
    hd = lambda a: jnp.transpose(a.reshape(b, t, MLA_HEADS, -1), (0, 2, 1, 3)).reshape(b, MLA_HEADS * t, -1)
    cache = past['mla'][occ]
    ol = _mla_past_attn(past['page_table'], cache, hd(ql), hd(qr), new_lat, t)
    ol = jnp.transpose(ol.reshape(b, MLA_HEADS, t, -1), (0, 2, 1, 3)).reshape(m, -1)
    o = _headmm(ol, w['uv'])
    return _mm(o, w['out'], res=h), new_lat


def _hg_past(h, g, w_in, lower, f_b, g_norm, w_out, b, t, state):
    z = _to_slab(_mm(h, w_in, g=g), b, t)
    os_, s_f = _hgrn(z, lower, f_b, g_norm, state=state)
    return _mm(_from_slab(os_, b, t), w_out, res=h), s_f


def _prepare(prm):
    depth = prm['norm_mix'].shape[0]
    sm = jax.nn.softmax(prm['hg_lb_logits'].astype(F32), axis=0)
    lower = jnp.cumsum(sm, axis=0) - sm[0]
    w = dict(depth=depth, lower=lower)
    w['nsa'] = [_prep_nsa(prm['nsa_w_in'][o], prm['nsa_cmp_pe'][o], prm['nsa_cmp_w1'][o], prm['nsa_cmp_w2'][o],
                          prm['nsa_w_out'][o]) for o in range(prm['nsa_w_in'].shape[0])]
    w['ml'] = [_prep_ml(prm['ml_w_in'][o], prm['ml_gate_b'][o], prm['ml_w_out'][o])
               for o in range(prm['ml_w_in'].shape[0])]
    w['mla'] = [_prep_mla(prm['mla_w_in'][o], prm['mla_q_norm'][o], prm['mla_kv_norm'][o], prm['mla_w_uq'][o],
                          prm['mla_w_uk'][o], prm['mla_w_uv'][o], prm['mla_w_out'][o])
                for o in range(prm['mla_w_in'].shape[0])]
    w['hg'] = [dict(w_in=prm['hg_w_in'][o].astype(BF16), f_b=prm['hg_f_b'][o], norm=prm['hg_norm'][o],
                    out=prm['hg_w_out'][o].astype(BF16)) for o in range(prm['hg_w_in'].shape[0])]
    w['ffn'] = [dict(up=prm['ffn_w_up'][i].astype(BF16),
                     conv4=jnp.concatenate([prm['ffn_conv_w'][i], prm['ffn_conv_b'][i][None]], axis=0).astype(F32),
                     down=prm['ffn_w_down'][i].astype(BF16)) for i in range(depth)]
    w['ple'] = [dict(proj=prm['ple_w_proj'][i].astype(BF16), gate=prm['ple_w_gate'][i].astype(BF16))
                for i in range(depth)]
    return w


def _trunk(x, p, start, past, prm, w):
    b, t, d = x.shape
    depth = w['depth']
    pos = start + jnp.arange(t, dtype=jnp.int32)
    if past is None:
        tabs = {hf: _rope_tables(pos, hf) for hf in (NSA_DH // 2, MLA_ROPE // 2)}
    else:
        tabs = {hf: tuple(jnp.tile(a, (b, 1)) for a in _rope_tables(pos, hf)) for hf in (NSA_DH // 2, MLA_ROPE // 2)}
    new = {}
    h = x.reshape(b * t, d)
    for i in range(depth):
        kind, occ = i % 4, i // 4
        g = prm['norm_mix'][i]
        if kind == 0:
            if past is None:
                h, rows = _nsa_fresh(h, g, w['nsa'][occ], b, t, tabs)
                win = rows[2][:, -min(WINDOW, t):]
            else:
                h, rows, win = _nsa_past(h, g, w['nsa'][occ], b, t, tabs, start, past, occ)
            for name, r in zip(('nsa_cmp', 'nsa_slc'), rows[:2]):
                new.setdefault(name, []).append(r.reshape(b, t, 2, NSA_G, NSA_DH))
            new.setdefault('nsa_win', []).append(win.reshape(b, win.shape[1], 2, NSA_G, NSA_DH))
        elif kind == 1:
            if past is None:
                h, st = _ml_fresh(h, g, w['ml'][occ], b, t)
            else:
                h, st = _ml_past(h, g, w['ml'][occ], b, t, past, occ)
            for name, s in zip(('ml_C', 'ml_n', 'ml_m'), st):
                new.setdefault(name, []).append(s)
        elif kind == 2:
            if past is None:
                h, lat = _mla_fresh(h, g, w['mla'][occ], b, t, tabs)
            else:
                h, lat = _mla_past(h, g, w['mla'][occ], b, t, tabs, past, occ)
            new.setdefault('mla', []).append(lat)
        else:
            hw = w['hg'][occ]
            if past is None:
                h, s_f = _hg_fresh(h, g, hw['w_in'], w['lower'][i], hw['f_b'], hw['norm'], hw['out'], b, t)
            else:
                h, s_f = _hg_past(h, g, hw['w_in'], w['lower'][i], hw['f_b'], hw['norm'], hw['out'], b, t,
                                  past['hg_S'][occ])
            new.setdefault('hg_S', []).append(s_f)
        fw = w['ffn'][i]
        f = fw['down'].shape[0]
        if past is None:
            h, sa, sg = _ffn(h, prm['norm_ffn'][i], fw['up'], fw['conv4'], fw['down'], t)
            new.setdefault('ffn_conv', []).append(jnp.concatenate([sa, sg], axis=-1))
        else:
            buf = past['ffn_conv'][i]
            zero = jnp.zeros((b, t - 1, 2 * f), F32)
            p1 = jnp.concatenate([buf[:, 1:2], zero], axis=1).reshape(b * t, 2 * f)
            p2 = jnp.concatenate([buf, zero[:, 1:]], axis=1).reshape(b * t, 2 * f)
            h, ua, ug = _ffn(h, prm['norm_ffn'][i], fw['up'], fw['conv4'], fw['down'], t, prev=(p1, p2))
            u = jnp.concatenate([ua, ug], axis=-1).reshape(b, t, 2 * f)
            new.setdefault('ffn_conv', []).append(u[:, -(CONV_W - 1):])
        pw = w['ple'][i]
        h = _ple(h, prm['norm_ple'][i], pw['gate'], p[i].reshape(b * t, -1), pw['proj'],
                 final_g=prm['norm_final'] if i == depth - 1 else None)
    return h.reshape(b, t, d), {k: jnp.stack(v) for k, v in new.items()}


def kernel(x_prompt, x_sample, cache_nsa_cmp_kv, cache_nsa_slc_kv, state_nsa_win_kv, cache_mla_latent,
           state_mlstm_C, state_mlstm_n, state_mlstm_m, state_hgrn_S, state_ffn_conv, page_table,
           p_prompt, p_sample, norm_mix, norm_ffn, norm_ple, norm_final, nsa_w_in, nsa_cmp_pe, nsa_cmp_w1,
           nsa_cmp_w2, nsa_w_out, ml_w_in, ml_gate_b, ml_w_out, mla_w_in, mla_q_norm, mla_kv_norm, mla_w_uq,
           mla_w_uk, mla_w_uv, mla_w_out, hg_w_in, hg_f_b, hg_lb_logits, hg_norm, hg_w_out, ffn_w_up,
           ffn_conv_w, ffn_conv_b, ffn_w_down, ple_w_proj, ple_w_gate):
    prm = {
        'norm_mix': norm_mix, 'norm_ffn': norm_ffn, 'norm_ple': norm_ple, 'norm_final': norm_final,
        'nsa_w_in': nsa_w_in, 'nsa_cmp_pe': nsa_cmp_pe, 'nsa_cmp_w1': nsa_cmp_w1, 'nsa_cmp_w2': nsa_cmp_w2,
        'nsa_w_out': nsa_w_out, 'ml_w_in': ml_w_in, 'ml_gate_b': ml_gate_b, 'ml_w_out': ml_w_out,
        'mla_w_in': mla_w_in, 'mla_q_norm': mla_q_norm, 'mla_kv_norm': mla_kv_norm, 'mla_w_uq': mla_w_uq,
        'mla_w_uk': mla_w_uk, 'mla_w_uv': mla_w_uv, 'mla_w_out': mla_w_out, 'hg_w_in': hg_w_in,
        'hg_f_b': hg_f_b, 'hg_lb_logits': hg_lb_logits, 'hg_norm': hg_norm, 'hg_w_out': hg_w_out,
        'ffn_w_up': ffn_w_up, 'ffn_conv_w': ffn_conv_w, 'ffn_conv_b': ffn_conv_b, 'ffn_w_down': ffn_w_down,
        'ple_w_proj': ple_w_proj, 'ple_w_gate': ple_w_gate,
    }
    past = {
        'nsa_cmp': cache_nsa_cmp_kv, 'nsa_slc': cache_nsa_slc_kv, 'nsa_win': state_nsa_win_kv,
        'mla': cache_mla_latent, 'ml_C': state_mlstm_C, 'ml_n': state_mlstm_n, 'ml_m': state_mlstm_m,
        'hg_S': state_hgrn_S, 'ffn_conv': state_ffn_conv, 'page_table': page_table,
    }
    w = _prepare(prm)
    past_len = page_table.shape[1] * PAGE
    y_p, sp = _trunk(x_prompt, p_prompt, 0, None, prm, w)
    y_s, ss = _trunk(x_sample, p_sample, past_len, past, prm, w)
    return (y_p, y_s,
            sp['nsa_cmp'], ss['nsa_cmp'], sp['nsa_slc'], ss['nsa_slc'], sp['nsa_win'], ss['nsa_win'],
            sp['mla'], ss['mla'], sp['ml_C'], ss['ml_C'], sp['ml_n'], ss['ml_n'], sp['ml_m'], ss['ml_m'],
            sp['hg_S'], ss['hg_S'], sp['ffn_conv'], ss['ffn_conv'])
```

```python
import functools
import math

import numpy as np
import jax
import jax.numpy as jnp
from jax import lax
from jax.experimental import pallas as pl
from jax.experimental.pallas import tpu as pltpu

F32 = jnp.float32
BF16 = jnp.bfloat16
NEG = -1e30
NORM_EPS = 1e-6
ROPE_THETA = 10000.0
FORCE_BONUS = 1e4

LANES = 128
VMEM_LIMIT = 56 * 1024 * 1024

PAGE = 128
NSA_HEADS, NSA_G, NSA_DH = 16, 2, 64
CMP_LEN, CMP_STRIDE, CMP_HID = 32, 16, 256
SLC_BLOCK, SLC_TOPN, WINDOW = 64, 16, 512
ML_HEADS, ML_DK, ML_DV = 4, 128, 256
MLA_HEADS, MLA_Q_LORA, MLA_KV_LORA, MLA_NOPE, MLA_ROPE, MLA_DV = 16, 512, 256, 64, 32, 64
HG_HEADS, HG_DK, HG_DV = 8, 128, 128
CONV_W = 3


def _cparams(*sem):
    return pltpu.CompilerParams(dimension_semantics=sem, vmem_limit_bytes=VMEM_LIMIT)


def _dot(a, b):
    return jnp.dot(a.astype(BF16), b.astype(BF16), preferred_element_type=F32)


def _dot_nt(a, b):
    return lax.dot_general(a.astype(BF16), b.astype(BF16), (((1,), (1,)), ((), ())),
                           preferred_element_type=F32)


def _dot_f32(a, b):
    return jnp.dot(a, b, precision=lax.Precision.HIGHEST, preferred_element_type=F32)


def _dot_nt_f32(a, b):
    return lax.dot_general(a, b, (((1,), (1,)), ((), ())), precision=lax.Precision.HIGHEST,
                           preferred_element_type=F32)


def _rms(x, g):
    return x * lax.rsqrt(jnp.mean(x * x, axis=-1, keepdims=True) + NORM_EPS) * g


def _sigmoid(x):
    return 1.0 / (1.0 + jnp.exp(-x))


def _silu(x):
    return x * _sigmoid(x)


def _rope_tile(z, cos, sin, half):
    n = z.shape[1]
    reps = n // LANES
    c = jnp.concatenate([cos] * reps, axis=1) if reps > 1 else cos
    s = jnp.concatenate([sin] * reps, axis=1) if reps > 1 else sin
    lane = lax.broadcasted_iota(jnp.int32, z.shape, 1)
    lower = (lane & (2 * half - 1)) < half
    partner = jnp.where(lower, pltpu.roll(z, n - half, 1), pltpu.roll(z, half, 1))
    return z * c + partner * s


def _rope_tables(pos, half):
    inv = jnp.power(ROPE_THETA, -jnp.arange(half, dtype=F32) / half)
    ang = pos.astype(F32)[:, None] * inv[None, :]
    cos, sin = jnp.cos(ang), jnp.sin(ang)
    reps = LANES // (2 * half)
    return (jnp.tile(jnp.concatenate([cos, cos], axis=1), (1, reps)),
            jnp.tile(jnp.concatenate([-sin, sin], axis=1), (1, reps)))


def _mm_kernel(*refs, norm, res, epi, n_aux, x_slabs, out_slabs):
    x_ref = refs[0]
    pos = 1
    g_ref = None
    if norm:
        g_ref = refs[pos]
        pos += 1
    w_ref = refs[pos]
    pos += 1
    aux = refs[pos:pos + n_aux]
    pos += n_aux
    r_ref = None
    if res:
        r_ref = refs[pos]
        pos += 1
    o_ref = refs[pos]
    j = pl.program_id(1)
    if norm:
        xn_ref = refs[pos + 1]

        @pl.when(j == 0)
        def _():
            xn_ref[...] = _rms(x_ref[...], g_ref[...]).astype(BF16)

        x = xn_ref[...]
    elif x_slabs:
        x = jnp.concatenate([x_ref[s] for s in range(x_slabs)], axis=1)
    else:
        x = x_ref[...]
    z = _dot(x, w_ref[...])
    if epi is not None:
        z = epi(z, j, *[a[...] for a in aux])
    if res:
        z = z + r_ref[...]
    if out_slabs:
        for s in range(out_slabs):
            o_ref[s] = z[:, s * LANES:(s + 1) * LANES].astype(o_ref.dtype)
    else:
        o_ref[...] = z.astype(o_ref.dtype)


def _mm(x, w, *, g=None, res=None, epi=None, aux=(), out_dtype=F32, layout='flat',
        seq=None, tm=1024, tn=1024, x_cols=None, x_slab_seq=None):
    k, n = w.shape
    if x_slab_seq is not None:
        b, ks, t, _ = x.shape
        m = b * t
    else:
        m = x.shape[0]
        t = seq
    tm = min(tm, m if t is None else t)
    tn = min(tn, n)
    assert m % tm == 0 and n % tn == 0
    ni, nj = m // tm, n // tn
    tps = None if t is None else t // tm
    in_specs, args = [], []
    if x_slab_seq is not None:
        in_specs.append(pl.BlockSpec((None, ks, tm, LANES), lambda i, j: (i // tps, 0, i % tps, 0)))
    elif x_cols is not None:
        in_specs.append(pl.BlockSpec((tm, x_cols[0]), lambda i, j: (i, x_cols[1])))
    else:
        in_specs.append(pl.BlockSpec((tm, k), lambda i, j: (i, 0)))
    args.append(x)
    if g is not None:
        in_specs.append(pl.BlockSpec((1, k), lambda i, j: (0, 0)))
        args.append(g.reshape(1, k).astype(F32))
    in_specs.append(pl.BlockSpec((k, tn), lambda i, j: (0, j)))
    args.append(w)
    for a, bs, im in aux:
        in_specs.append(pl.BlockSpec(bs, im))
        args.append(a)
    if res is not None:
        in_specs.append(pl.BlockSpec((tm, tn), lambda i, j: (i, j)))
        args.append(res)
    if layout == 'flat':
        out_shape = jax.ShapeDtypeStruct((m, n), out_dtype)
        out_spec = pl.BlockSpec((tm, tn), lambda i, j: (i, j))
        out_slabs = 0
    elif layout == 'tiles':
        out_shape = jax.ShapeDtypeStruct((nj, m, tn), out_dtype)
        out_spec = pl.BlockSpec((None, tm, tn), lambda i, j: (j, i, 0))
        out_slabs = 0
    else:
        out_slabs = tn // LANES
        out_shape = jax.ShapeDtypeStruct((m // t, n // LANES, t, LANES), out_dtype)
        out_spec = pl.BlockSpec((None, out_slabs, tm, LANES), lambda i, j: (i // tps, j, i % tps, 0))
    scratch = [pltpu.VMEM((tm, k), BF16)] if g is not None else []
    kern = functools.partial(_mm_kernel, norm=g is not None, res=res is not None, epi=epi,
                             n_aux=len(aux), x_slabs=(ks if x_slab_seq is not None else 0),
                             out_slabs=out_slabs)
    return pl.pallas_call(
        kern, grid=(ni, nj), in_specs=in_specs, out_specs=out_spec, out_shape=out_shape,
        scratch_shapes=scratch, compiler_params=_cparams('parallel', 'arbitrary'),
        name='mm')(*args)


def _rope_aux(cos, sin, tm, period_blocks):
    if period_blocks is None:
        im = lambda i, j: (i, 0)
    else:
        im = lambda i, j: (i % period_blocks, 0)
    return [(cos, (tm, LANES), im), (sin, (tm, LANES), im)]


def _headmm_kernel(x_ref, w_ref, o_ref, *, heads, a, c):
    for h in range(heads):
        o_ref[:, h * c:(h + 1) * c] = _dot(x_ref[:, h * a:(h + 1) * a], w_ref[h]).astype(o_ref.dtype)


def _headmm(x, w, out_dtype=BF16, tm=512):
    m = x.shape[0]
    heads, a, c = w.shape
    tm = min(tm, m)
    return pl.pallas_call(
        functools.partial(_headmm_kernel, heads=heads, a=a, c=c), grid=(m // tm,),
        in_specs=[pl.BlockSpec((tm, heads * a), lambda i: (i, 0)),
                  pl.BlockSpec((heads, a, c), lambda i: (0, 0, 0))],
        out_specs=pl.BlockSpec((tm, heads * c), lambda i: (i, 0)),
        out_shape=jax.ShapeDtypeStruct((m, heads * c), out_dtype),
        compiler_params=_cparams('parallel'), name='headmm')(x, w)


def _ffn_kernel(*refs, tps, seq_len, has_prev):
    if has_prev:
        (h_ref, g_ref, wa_ref, wg_ref, ca_ref, cg_ref, wd_ref, p1a_ref, p1g_ref, p2a_ref, p2g_ref,
         o_ref, ua_ref, ug_ref, hn_ref) = refs
    else:
        (h_ref, g_ref, wa_ref, wg_ref, ca_ref, cg_ref, wd_ref,
         o_ref, sa_ref, sg_ref, hn_ref, carry_ref) = refs
    i, j = pl.program_id(0), pl.program_id(1)
    tm = h_ref.shape[0]
    ch = min(64, tm)

    @pl.when(j == 0)
    def _():
        x = h_ref[...]
        hn_ref[...] = _rms(x, g_ref[...]).astype(BF16)
        o_ref[...] = x

    hn = hn_ref[...]
    ua = _dot(hn, wa_ref[...])
    ug = _dot(hn, wg_ref[...])
    if has_prev:
        assert ch % seq_len == 0
        ua_ref[...] = ua
        ug_ref[...] = ug
        prevs = (None, None)
    else:
        @pl.when(i % tps == 0)
        def _():
            carry_ref[j] = jnp.zeros(carry_ref.shape[1:], F32)

        prevs = (carry_ref[j, 0], carry_ref[j, 1])
        carry_ref[j, 0] = ua[tm - 8:]
        carry_ref[j, 1] = ug[tm - 8:]
        sa_ref[...] = pltpu.roll(ua[tm - 8:], CONV_W - 1, 0)[0:CONV_W - 1]
        sg_ref[...] = pltpu.roll(ug[tm - 8:], CONV_W - 1, 0)[0:CONV_W - 1]

    def conv_chunk(u, cw, prev8, p1_ref, p2_ref, r0):
        cur = u[r0:r0 + ch]
        if has_prev:
            t = lax.broadcasted_iota(jnp.int32, cur.shape, 0) % seq_len
            s1 = jnp.where(t >= 1, pltpu.roll(cur, 1, 0), p1_ref[r0:r0 + ch, :])
            s2 = jnp.where(t >= 2, pltpu.roll(cur, 2, 0), p2_ref[r0:r0 + ch, :])
        else:
            x = jnp.concatenate([prev8 if r0 == 0 else u[r0 - 8:r0], cur], axis=0)
            s1 = pltpu.roll(x, 1, 0)[8:]
            s2 = pltpu.roll(x, 2, 0)[8:]
        return cw[0:1] * s2 + cw[1:2] * s1 + cw[2:3] * cur + cw[3:4]

    cwa, cwg = ca_ref[...], cg_ref[...]
    acts = []
    for c in range(tm // ch):
        ca = conv_chunk(ua, cwa, prevs[0], p1a_ref if has_prev else None, p2a_ref if has_prev else None, c * ch)
        cg = conv_chunk(ug, cwg, prevs[1], p1g_ref if has_prev else None, p2g_ref if has_prev else None, c * ch)
        acts.append((_silu(ca) * cg).astype(BF16))
    act = jnp.concatenate(acts, axis=0) if len(acts) > 1 else acts[0]
    o_ref[...] += _dot(act, wd_ref[...])


def _ffn(h, g, w_up, conv4, w_down, seq_len, prev=None, tn=256):
    m, d = h.shape
    f = w_down.shape[0]
    nj = f // tn
    has_prev = prev is not None
    tm = m if has_prev else min(1024, seq_len)
    tps = max(seq_len // tm, 1)
    ni = m // tm
    specs = [pl.BlockSpec((tm, d), lambda i, j: (i, 0)),
             pl.BlockSpec((1, d), lambda i, j: (0, 0)),
             pl.BlockSpec((d, tn), lambda i, j: (0, j)),
             pl.BlockSpec((d, tn), lambda i, j: (0, j + nj)),
             pl.BlockSpec((4, tn), lambda i, j: (0, j)),
             pl.BlockSpec((4, tn), lambda i, j: (0, j + nj)),
             pl.BlockSpec((tn, d), lambda i, j: (j, 0))]
    args = [h, g.reshape(1, d).astype(F32), w_up, w_up, conv4, conv4, w_down]
    scratch = [pltpu.VMEM((tm, d), BF16)]
    if has_prev:
        p1, p2 = prev
        specs += [pl.BlockSpec((tm, tn), lambda i, j: (i, j)), pl.BlockSpec((tm, tn), lambda i, j: (i, j + nj)),
                  pl.BlockSpec((tm, tn), lambda i, j: (i, j)), pl.BlockSpec((tm, tn), lambda i, j: (i, j + nj))]
        args += [p1, p1, p2, p2]
        out_shape = [jax.ShapeDtypeStruct((m, d), F32), jax.ShapeDtypeStruct((m, f), F32),
                     jax.ShapeDtypeStruct((m, f), F32)]
        out_specs = [pl.BlockSpec((tm, d), lambda i, j: (i, 0)), pl.BlockSpec((tm, tn), lambda i, j: (i, j)),
                     pl.BlockSpec((tm, tn), lambda i, j: (i, j))]
    else:
        out_shape = [jax.ShapeDtypeStruct((m, d), F32), jax.ShapeDtypeStruct((ni, CONV_W - 1, f), F32),
                     jax.ShapeDtypeStruct((ni, CONV_W - 1, f), F32)]
        out_specs = [pl.BlockSpec((tm, d), lambda i, j: (i, 0)),
                     pl.BlockSpec((None, CONV_W - 1, tn), lambda i, j: (i, 0, j)),
                     pl.BlockSpec((None, CONV_W - 1, tn), lambda i, j: (i, 0, j))]
        scratch.append(pltpu.VMEM((nj, 2, 8, tn), F32))
    outs = pl.pallas_call(
        functools.partial(_ffn_kernel, tps=tps, seq_len=seq_len, has_prev=has_prev),
        grid=(ni, nj), in_specs=specs, out_specs=out_specs, out_shape=out_shape,
        scratch_shapes=scratch, compiler_params=_cparams('arbitrary', 'arbitrary'), name='ffn')(*args)
    if has_prev:
        return outs
    return outs[0], outs[1][tps - 1::tps], outs[2][tps - 1::tps]


def _ple_kernel(*refs, final):
    if final:
        h_ref, g_ref, wg_ref, p_ref, wp_ref, gf_ref, o_ref = refs
    else:
        h_ref, g_ref, wg_ref, p_ref, wp_ref, o_ref = refs
    h = h_ref[...]
    gate = _sigmoid(_dot(_rms(h, g_ref[...]), wg_ref[...]))
    y = h + gate * _dot(p_ref[...], wp_ref[...])
    if final:
        y = _rms(y, gf_ref[...])
    o_ref[...] = y


def _ple(h, g, w_gate, p, w_proj, final_g=None, tm=1024):
    m, d = h.shape
    pd = p.shape[1]
    tm = min(tm, m)
    specs = [pl.BlockSpec((tm, d), lambda i: (i, 0)), pl.BlockSpec((1, d), lambda i: (0, 0)),
             pl.BlockSpec((d, d), lambda i: (0, 0)), pl.BlockSpec((tm, pd), lambda i: (i, 0)),
             pl.BlockSpec((pd, d), lambda i: (0, 0))]
    args = [h, g.reshape(1, d).astype(F32), w_gate, p, w_proj]
    if final_g is not None:
        specs.append(pl.BlockSpec((1, d), lambda i: (0, 0)))
        args.append(final_g.reshape(1, d).astype(F32))
    return pl.pallas_call(
        functools.partial(_ple_kernel, final=final_g is not None), grid=(m // tm,),
        in_specs=specs, out_specs=pl.BlockSpec((tm, d), lambda i: (i, 0)),
        out_shape=jax.ShapeDtypeStruct((m, d), F32), compiler_params=_cparams('parallel'),
        name='ple')(*args)


def _softmax_tile(s, mask):
    s = jnp.where(mask, s, NEG)
    e = jnp.where(mask, jnp.exp(s - jnp.max(s, axis=-1, keepdims=True)), 0.0)
    return e / jnp.maximum(jnp.sum(e, axis=-1, keepdims=True), 1e-30)


def _online_init(m_ref, l_ref, acc_ref):
    m_ref[...] = jnp.full(m_ref.shape, NEG, F32)
    l_ref[...] = jnp.zeros(l_ref.shape, F32)
    acc_ref[...] = jnp.zeros(acc_ref.shape, F32)


def _online_update(s, mask, v, m_ref, l_ref, acc_ref, v_t=False):
    if mask is not None:
        s = jnp.where(mask, s, NEG)
    m_old = m_ref[...]
    m_new = jnp.maximum(m_old, jnp.max(s, axis=-1, keepdims=True))
    p = jnp.exp(s - m_new)
    if mask is not None:
        p = jnp.where(mask, p, 0.0)
    alpha = jnp.exp(m_old - m_new)
    l_ref[...] = alpha * l_ref[...] + jnp.sum(p, axis=-1, keepdims=True)
    acc_ref[...] = alpha * acc_ref[...] + (_dot_nt(p, v) if v_t else _dot(p, v))
    m_ref[...] = m_new


def _online_result(l_ref, acc_ref):
    return acc_ref[...] / jnp.maximum(l_ref[...], 1e-30)


def _online_update_t(st, mask, v_t, m_ref, l_ref, acc_ref):
    if mask is not None:
        st = jnp.where(mask, st, NEG)
    m_old = m_ref[...]
    m_new = jnp.maximum(m_old, jnp.max(st, axis=0, keepdims=True))
    p = jnp.exp(st - m_new)
    if mask is not None:
        p = jnp.where(mask, p, 0.0)
    alpha = jnp.exp(m_old - m_new)
    l_ref[...] = alpha * l_ref[...] + jnp.sum(p, axis=0, keepdims=True)
    acc_ref[...] = alpha * acc_ref[...] + _dot(v_t, p)
    m_ref[...] = m_new


def _softmax_tile_t(st, mask):
    st = jnp.where(mask, st, NEG)
    e = jnp.where(mask, jnp.exp(st - jnp.max(st, axis=0, keepdims=True)), 0.0)
    return e / jnp.maximum(jnp.sum(e, axis=0, keepdims=True), 1e-30)


def _cumsum(x, axis):
    n = x.shape[axis]
    idx = lax.broadcasted_iota(jnp.int32, x.shape, axis)
    k = 1
    while k < n:
        x = x + jnp.where(idx >= k, pltpu.roll(x, k, axis), 0.0)
        k *= 2
    return x


def _log_sigmoid(x):
    return jnp.minimum(x, 0.0) - jnp.log(1.0 + jnp.exp(-jnp.abs(x)))


def _compress_rows(k_ref, v_ref, n_seg, pe_ref, w1_ref, w2p_ref):
    outs = []
    seg_w = CMP_STRIDE * NSA_DH
    lane = lax.broadcasted_iota(jnp.int32, (n_seg, LANES), 1)
    for kind, rows_ref in enumerate((k_ref, v_ref)):
        xs = [rows_ref[pl.ds(s, n_seg, stride=CMP_STRIDE), :] for s in range(CMP_STRIDE)]
        o = None
        for g in range(NSA_G):
            pieces = []
            for a in range(CMP_STRIDE // 2):
                ev, od = xs[2 * a], xs[2 * a + 1]
                if g == 0:
                    pieces.append(jnp.where(lane < NSA_DH, ev, pltpu.roll(od, NSA_DH, 1)))
                else:
                    pieces.append(jnp.where(lane < NSA_DH, pltpu.roll(ev, NSA_DH, 1), od))
            seg = jnp.concatenate(pieces, axis=1)
            pre = None
            for r in range(CMP_LEN // CMP_STRIDE):
                acc = _dot(seg + pe_ref[kind, r:r + 1, :], w1_ref[kind, r * seg_w:(r + 1) * seg_w, :])
                pre = acc if r == 0 else pre + pltpu.roll(acc, n_seg - r, 0)
            t = _dot(_silu(pre), w2p_ref[kind, g])
            o = t if o is None else o + t
        outs.append(o)
    return outs


def _nsa_cmp_kernel(k_ref, v_ref, pe_ref, w1_ref, w2p_ref, kc_ref, vc_ref, *, n_seg):
    kc, vc = _compress_rows(k_ref, v_ref, n_seg, pe_ref, w1_ref, w2p_ref)
    kc_ref[...] = kc
    vc_ref[...] = vc


def _nsa_compress(rows, pe, w1, w2p):
    b, t, _ = rows.shape
    n_seg = t // CMP_STRIDE
    full = lambda shp: pl.BlockSpec(shp, lambda i: (0,) * len(shp))
    return pl.pallas_call(
        functools.partial(_nsa_cmp_kernel, n_seg=n_seg), grid=(b,),
        in_specs=[pl.BlockSpec((None, t, LANES), lambda i: (i, 0, 0)),
                  pl.BlockSpec((None, t, LANES), lambda i: (i, 0, 1)), full(pe.shape), full(w1.shape),
                  full(w2p.shape)],
        out_specs=[pl.BlockSpec((None, n_seg, LANES), lambda i: (i, 0, 0))] * 2,
        out_shape=[jax.ShapeDtypeStruct((b, n_seg, LANES), F32)] * 2,
        compiler_params=_cparams('parallel'), name='nsa_compress')(rows, rows, pe, w1, w2p)


def _overlap_t(nc, ns, nc_pad, ns_pad):
    cs = np.arange(nc)[None, :] * CMP_STRIDE
    ss = np.arange(ns)[:, None] * SLC_BLOCK
    ov = np.maximum(np.minimum(cs + CMP_LEN, ss + SLC_BLOCK) - np.maximum(cs, ss), 0) / CMP_LEN
    out = np.zeros((ns_pad, nc_pad), np.float32)
    out[:ns, :nc] = ov
    return jnp.asarray(out)


def _select_blocks(sc, valid, blk, n_cand, n_sel):
    rank = jnp.zeros(sc.shape, F32)
    for j in range(n_cand):
        rj = sc[j:j + 1, :]
        beats = (rj > sc) | ((rj == sc) & (blk > j))
        rank = rank + jnp.where(beats, 1.0, 0.0)
    return jnp.where(valid & (rank < n_sel), 1.0, 0.0)


def _nsa_attn_kernel(q_ref, gt_ref, kc_ref, vc_ref, ks_ref, vs_ref, kw_ref, vw_ref, ovt_ref, et_ref,
                     o_ref, ocmp_ref, msk_ref, wmsk_ref, qt_ref, vst_ref, vwt_ref, m_ref, l_ref, acc_ref,
                     *, tq, t_len, nc, n_sel):
    qi = pl.program_id(1)
    q0 = qi * tq
    scale = NSA_DH ** -0.5
    kt = 256
    ns = t_len // SLC_BLOCK
    n_pairs = NSA_HEADS // 2
    m2 = 2 * tq

    @pl.when(qi == 0)
    def _():
        for jt in range(t_len // kt):
            vst_ref[jt] = vs_ref[jt * kt:(jt + 1) * kt, :].T.astype(BF16)
            vwt_ref[jt] = vw_ref[jt * kt:(jt + 1) * kt, :].T.astype(BF16)

    zeros_half = jnp.zeros((NSA_DH, tq), F32)
    for hp in range(n_pairs):
        g = hp // (n_pairs // NSA_G)
        slab_t = q_ref[hp].astype(F32).T * scale
        halves = []
        for part in (slab_t[:NSA_DH], slab_t[NSA_DH:]):
            halves.append(jnp.concatenate([part, zeros_half] if g == 0 else [zeros_half, part], axis=0))
        qt_ref[hp] = jnp.concatenate(halves, axis=1).astype(BF16)

    kc = kc_ref[...]
    vc_t = vc_ref[...].T
    ncp = kc.shape[0]
    cidx = lax.broadcasted_iota(jnp.int32, (ncp, m2), 0)
    qpos_c = q0 + (lax.broadcasted_iota(jnp.int32, (ncp, m2), 1) & (tq - 1))
    cmask = (cidx < nc) & (cidx * CMP_STRIDE + (CMP_LEN - 1) <= qpos_c)
    imp = [jnp.zeros((ncp, tq), F32) for _ in range(NSA_G)]
    for hp in range(n_pairs):
        g = hp // (n_pairs // NSA_G)
        p = _softmax_tile_t(_dot(kc, qt_ref[hp]), cmask)
        ocmp_ref[hp] = _dot(vc_t, p)
        imp[g] = imp[g] + p[:, :tq] + p[:, tq:]

    nsp = ovt_ref.shape[0]
    blk = lax.broadcasted_iota(jnp.int32, (ns, tq), 0)
    cur = (q0 + lax.broadcasted_iota(jnp.int32, (ns, tq), 1)) // SLC_BLOCK
    forced = (blk == 0) | (blk == cur) | (blk == cur - 1)
    valid = blk <= cur
    krow1 = lax.broadcasted_iota(jnp.int32, (kt, tq), 0)
    qpos1 = q0 + lax.broadcasted_iota(jnp.int32, (kt, tq), 1)
    for g in range(NSA_G):
        sc = _dot_f32(ovt_ref[...], imp[g])[:ns]
        sc = jnp.where(valid, sc + jnp.where(forced, FORCE_BONUS, 0.0), NEG)
        sel_t = _select_blocks(sc, valid, blk, ns, n_sel)
        if nsp > ns:
            sel_t = jnp.concatenate([sel_t, jnp.zeros((nsp - ns, tq), F32)], axis=0)
        for jt in range(t_len // kt):
            hit = _dot(et_ref[jt * kt:(jt + 1) * kt, :], sel_t)
            msk_ref[g, jt] = jnp.where(jt * kt + krow1 <= qpos1, hit, 0.0)
    win_tiles = list(range(-WINDOW, tq, kt))
    for wi, d in enumerate(win_tiles):
        kpos = q0 + d + krow1
        wmsk_ref[wi] = jnp.where((kpos <= qpos1) & (kpos > qpos1 - WINDOW), 1.0, 0.0)

    gt_t = gt_ref[...].T
    n_par = m_ref.shape[0]
    for hp0 in range(0, n_pairs, n_par):
        g = hp0 // (n_pairs // NSA_G)
        stats = [(m_ref.at[u], l_ref.at[u], acc_ref.at[u]) for u in range(n_par)]
        qs = [qt_ref[hp0 + u] for u in range(n_par)]

        for st in stats:
            _online_init(*st)

        def slc_body(jt, carry):
            start = pl.multiple_of(jt * kt, kt)
            mf = msk_ref[g, jt]
            mask = jnp.concatenate([mf, mf], axis=1) > 0.5
            k = ks_ref[pl.ds(start, kt), :].astype(BF16)
            v_t = vst_ref[jt]
            for q2, st in zip(qs, stats):
                _online_update_t(_dot(k, q2), mask, v_t, *st)
            return carry

        lax.fori_loop(0, (q0 + tq) // kt, slc_body, 0)
        o_slc = [_online_result(st[1], st[2]) for st in stats]

        for st in stats:
            _online_init(*st)
        for wi, d in enumerate(win_tiles):
            @pl.when(q0 + d >= 0)
            def _():
                start = pl.multiple_of(q0 + d, kt)
                mf = wmsk_ref[wi]
                mask = jnp.concatenate([mf, mf], axis=1) > 0.5
                k = kw_ref[pl.ds(start, kt), :].astype(BF16)
                v_t = vwt_ref[(q0 + d) // kt]
                for q2, st in zip(qs, stats):
                    _online_update_t(_dot(k, q2), mask, v_t, *st)
        o_win = [_online_result(st[1], st[2]) for st in stats]

        for u in range(n_par):
            hp = hp0 + u

            def gate(c):
                row = c * NSA_HEADS + 2 * hp
                return jnp.concatenate([gt_t[row:row + 1, :], gt_t[row + 1:row + 2, :]], axis=1)

            o = gate(0) * ocmp_ref[hp] + gate(1) * o_slc[u] + gate(2) * o_win[u]
            o = o[g * NSA_DH:(g + 1) * NSA_DH]
            o_ref[hp] = jnp.concatenate([o[:, :tq], o[:, tq:]], axis=0).T.astype(o_ref.dtype)


def _nsa_attn(q_slab, gates, kc, vc, slc_rows, win_rows, tq=256):
    b, n_pairs, t, _ = q_slab.shape
    tq = min(tq, t)
    n_seg = kc.shape[1]
    nc = n_seg - 1
    ns = t // SLC_BLOCK
    nsp = LANES
    ovt = _overlap_t(nc, ns, n_seg, nsp)
    e = jnp.asarray((np.arange(t)[:, None] // SLC_BLOCK == np.arange(nsp)[None, :]).astype(np.float32), BF16)
    kern = functools.partial(_nsa_attn_kernel, tq=tq, t_len=t, nc=nc, n_sel=min(SLC_TOPN, ns))
    seq = lambda c: pl.BlockSpec((None, t, LANES), lambda i, j: (i, 0, c))
    return pl.pallas_call(
        kern, grid=(b, t // tq),
        in_specs=[pl.BlockSpec((None, n_pairs, tq, LANES), lambda i, j: (i, 0, j, 0)),
                  pl.BlockSpec((None, tq, LANES), lambda i, j: (i, j, 0)),
                  pl.BlockSpec((None, n_seg, LANES), lambda i, j: (i, 0, 0)),
                  pl.BlockSpec((None, n_seg, LANES), lambda i, j: (i, 0, 0)),
                  seq(0), seq(1), seq(0), seq(1),
                  pl.BlockSpec(ovt.shape, lambda i, j: (0, 0)),
                  pl.BlockSpec(e.shape, lambda i, j: (0, 0))],
        out_specs=pl.BlockSpec((None, n_pairs, tq, LANES), lambda i, j: (i, 0, j, 0)),
        out_shape=jax.ShapeDtypeStruct((b, n_pairs, t, LANES), BF16),
        scratch_shapes=[pltpu.VMEM((n_pairs, LANES, 2 * tq), F32),
                        pltpu.VMEM((NSA_G, t // 256, 256, tq), F32),
                        pltpu.VMEM((len(range(-WINDOW, tq, 256)), 256, tq), F32),
                        pltpu.VMEM((n_pairs, LANES, 2 * tq), BF16),
                        pltpu.VMEM((t // 256, LANES, 256), BF16), pltpu.VMEM((t // 256, LANES, 256), BF16),
                        pltpu.VMEM((2, 1, 2 * tq), F32), pltpu.VMEM((2, 1, 2 * tq), F32),
                        pltpu.VMEM((2, LANES, 2 * tq), F32)],
        compiler_params=_cparams('arbitrary', 'arbitrary'), name='nsa_attn')(
            q_slab, gates, kc, vc, slc_rows, slc_rows, win_rows, win_rows, ovt, e)


def _mlstm_kernel(*refs, L, t_real, has_state, pad):
    it = iter(refs)
    z_ref, gc_ref, gr_ref, bc_ref, br_ref = next(it), next(it), next(it), next(it), next(it)
    if has_state:
        c0_ref, n0_ref, m0_ref = next(it), next(it), next(it)
    h_ref, c_out, n_out, m_out = next(it), next(it), next(it), next(it)
    c_scr, n_scr, m_scr = next(it), next(it), next(it)
    if pad:
        zp_ref, gcp_ref, grp_ref = next(it), next(it), next(it)
    ci = pl.program_id(1)
    nchunks = pl.num_programs(1)

    @pl.when(ci == 0)
    def _():
        if has_state:
            c_scr[...] = c0_ref[...]
            n_scr[...] = n0_ref[...]
            m_scr[...] = m0_ref[...]
        else:
            c_scr[...] = jnp.zeros(c_scr.shape, F32)
            n_scr[...] = jnp.zeros(n_scr.shape, F32)
            m_scr[...] = jnp.zeros(m_scr.shape, F32)

    if pad:
        @pl.when((pl.program_id(0) == 0) & (ci == 0))
        def _():
            zp_ref[...] = jnp.zeros(zp_ref.shape, F32)
            gcp_ref[...] = jnp.zeros(gcp_ref.shape, F32)
            grp_ref[...] = jnp.zeros(grp_ref.shape, F32)

        zp_ref[:, 0:t_real, :] = z_ref[...]
        gcp_ref[0:t_real, :] = gc_ref[...]
        grp_ref[:, 0:t_real] = gr_ref[...]
        z_ref, gc_ref, gr_ref = zp_ref, gcp_ref, grp_ref

    H = ML_HEADS
    gcol = gc_ref[...] + bc_ref[...]
    grow = gr_ref[...] + br_ref[...]
    lf_c = _log_sigmoid(gcol)
    lf_r = _log_sigmoid(grow)
    ig_c, ig_r = gcol, grow
    if t_real < L:
        rv = lax.broadcasted_iota(jnp.int32, gcol.shape, 0) < t_real
        lv = lax.broadcasted_iota(jnp.int32, grow.shape, 1) < t_real
        lf_c, ig_c = jnp.where(rv, lf_c, 0.0), jnp.where(rv, ig_c, NEG)
        lf_r, ig_r = jnp.where(lv, lf_r, 0.0), jnp.where(lv, ig_r, NEG)
    bcum_c = _cumsum(lf_c, 0)
    bcum_r = _cumsum(lf_r, 1)
    tri = lax.broadcasted_iota(jnp.int32, (L, L), 0) >= lax.broadcasted_iota(jnp.int32, (L, L), 1)
    for h in range(H):
        q = z_ref[h]
        k = z_ref[H + h] * (ML_DK ** -0.5)
        v = jnp.concatenate([z_ref[2 * H + 2 * h], z_ref[2 * H + 2 * h + 1]], axis=1)
        og = jnp.concatenate([z_ref[4 * H + 2 * h], z_ref[4 * H + 2 * h + 1]], axis=1)
        bc_t = bcum_c[:, H + h:H + h + 1]
        bc_s = bcum_r[H + h:H + h + 1, :]
        ig_s = ig_r[h:h + 1, :]
        ig_t = ig_c[:, h:h + 1]
        m_old = m_scr[h:h + 1, 0:1]
        dmat = jnp.where(tri, bc_t - bc_s + ig_s, NEG)
        inter = bc_t + m_old
        mt = jnp.maximum(jnp.max(dmat, axis=1, keepdims=True), inter)
        w_intra = jnp.exp(dmat - mt)
        w_state = jnp.exp(inter - mt)
        sc = _dot_nt(q, k) * w_intra
        c_st = c_scr[h]
        n_st = n_scr[h:h + 1, :]
        num = _dot(sc, v) + w_state * _dot_nt(q, c_st)
        den = jnp.sum(sc, axis=1, keepdims=True) + w_state * jnp.sum(q * n_st, axis=1, keepdims=True)
        hh = num / jnp.maximum(jnp.abs(den), jnp.exp(-mt)) * og
        h_ref[2 * h] = hh[0:h_ref.shape[1], 0:LANES].astype(h_ref.dtype)
        h_ref[2 * h + 1] = hh[0:h_ref.shape[1], LANES:].astype(h_ref.dtype)
        bl = bc_s[:, L - 1:L]
        dl_r = bl - bc_s + ig_s
        dl_t = bl - bc_t + ig_t
        m_new = jnp.maximum(bl + m_old, jnp.max(dl_r, axis=1, keepdims=True))
        ws_t = jnp.exp(dl_t - m_new)
        wc = jnp.exp(bl + m_old - m_new)
        c_scr[h] = wc * c_st + _dot((v * ws_t).T, k)
        n_scr[h:h + 1, :] = wc * n_st + jnp.sum(ws_t * k, axis=0, keepdims=True)
        m_scr[h:h + 1, :] = jnp.broadcast_to(m_new, (1, LANES))

    @pl.when(ci == nchunks - 1)
    def _():
        c_out[...] = c_scr[...]
        n_out[...] = n_scr[...]
        m_out[...] = m_scr[...]


def _mlstm(z_slab, gates_col, gates_row, gate_b, state=None, L=256):
    b, ns, t, _ = z_slab.shape
    pad = t < 8
    L = 128 if pad else min(L, t)
    lr = t if pad else L
    nch = 1 if pad else t // L
    H = ML_HEADS
    bcol = jnp.zeros((1, LANES), F32).at[0, :2 * H].set(gate_b.reshape(-1).astype(F32))
    brow = gate_b.reshape(2 * H, 1).astype(F32)
    specs = [pl.BlockSpec((None, ns, lr, LANES), lambda i, c: (i, 0, c, 0)),
             pl.BlockSpec((None, lr, LANES), lambda i, c: (i, c, 0)),
             pl.BlockSpec((None, 2 * H, lr), lambda i, c: (i, 0, c)),
             pl.BlockSpec((1, LANES), lambda i, c: (0, 0)),
             pl.BlockSpec((2 * H, 1), lambda i, c: (0, 0))]
    args = [z_slab, gates_col, gates_row, bcol, brow]
    if state is not None:
        c0, n0, m0 = state
        n0p = jnp.zeros((b, 8, LANES), F32).at[:, :H].set(n0.astype(F32))
        m0p = jnp.zeros((b, 8, LANES), F32).at[:, :H].set(jnp.broadcast_to(m0.astype(F32)[..., None], (b, H, LANES)))
        specs += [pl.BlockSpec((None, H, ML_DV, ML_DK), lambda i, c: (i, 0, 0, 0)),
                  pl.BlockSpec((None, 8, LANES), lambda i, c: (i, 0, 0)),
                  pl.BlockSpec((None, 8, LANES), lambda i, c: (i, 0, 0))]
        args += [c0.astype(F32), n0p, m0p]
    scratch = [pltpu.VMEM((H, ML_DV, ML_DK), F32), pltpu.VMEM((8, LANES), F32), pltpu.VMEM((8, LANES), F32)]
    if pad:
        scratch += [pltpu.VMEM((ns, L, LANES), F32), pltpu.VMEM((L, LANES), F32), pltpu.VMEM((2 * H, L), F32)]
    kern = functools.partial(_mlstm_kernel, L=L, t_real=t if pad else L, has_state=state is not None, pad=pad)
    h_slab, c_f, n_f, m_f = pl.pallas_call(
        kern, grid=(b, nch), in_specs=specs,
        out_specs=[pl.BlockSpec((None, 2 * H, lr, LANES), lambda i, c: (i, 0, c, 0)),
                   pl.BlockSpec((None, H, ML_DV, ML_DK), lambda i, c: (i, 0, 0, 0)),
                   pl.BlockSpec((None, 8, LANES), lambda i, c: (i, 0, 0)),
                   pl.BlockSpec((None, 8, LANES), lambda i, c: (i, 0, 0))],
        out_shape=[jax.ShapeDtypeStruct((b, 2 * H, t, LANES), BF16),
                   jax.ShapeDtypeStruct((b, H, ML_DV, ML_DK), F32),
                   jax.ShapeDtypeStruct((b, 8, LANES), F32), jax.ShapeDtypeStruct((b, 8, LANES), F32)],
        scratch_shapes=scratch, compiler_params=_cparams('arbitrary', 'arbitrary'), name='mlstm')(*args)
    return h_slab, c_f, n_f[:, :H], m_f[:, :H, 0]


def _hgrn_levels(L):
    n_lev = int(math.log2(L))
    t = np.arange(L)
    pall = np.zeros((n_lev * L, L), np.float32)
    lmask = np.zeros((n_lev, L, L), np.float32)
    for lev in range(n_lev):
        w = L >> lev
        mid = (t // w) * w + w // 2
        pall[lev * L + t, mid - 1] = 1.0
        same = (t[:, None] // w) == (t[None, :] // w)
        lmask[lev] = same & ((t[:, None] % w) >= w // 2) & ((t[None, :] % w) < w // 2)
    return jnp.asarray(pall), jnp.asarray(lmask)


def _hgrn_kernel(*refs, L, t_real, has_state, pad):
    it = iter(refs)
    z_ref, lf_ref, gn_ref, pall_ref, lmask_ref = next(it), next(it), next(it), next(it), next(it)
    if has_state:
        s0_ref = next(it)
    o_ref, s_out = next(it), next(it)
    st_scr = next(it)
    if pad:
        zp_ref = next(it)
    ci = pl.program_id(1)
    nchunks = pl.num_programs(1)
    H = HG_HEADS
    n_lev = lmask_ref.shape[0]

    @pl.when(ci == 0)
    def _():
        for h in range(H):
            st_scr[h] = s0_ref[h].T if has_state else jnp.zeros((HG_DV, HG_DK), F32)

    if pad:
        @pl.when((pl.program_id(0) == 0) & (ci == 0))
        def _():
            zp_ref[...] = jnp.zeros(zp_ref.shape, F32)

        zp_ref[:, 0:t_real, :] = z_ref[...]
        z_ref = zp_ref

    rows = lax.broadcasted_iota(jnp.int32, (L, LANES), 0)
    eye = lax.broadcasted_iota(jnp.int32, (L, L), 0) == lax.broadcasted_iota(jnp.int32, (L, L), 1)
    gn = gn_ref[...]

    def head(h, carry):
        q = z_ref[h]
        zf = z_ref[H + h]
        v = z_ref[2 * H + h]
        gate = z_ref[3 * H + h]
        lower = lf_ref[pl.ds(h, 1), :]
        fb = lf_ref[pl.ds(H + h, 1), :]
        f = lower + (1.0 - lower) * _sigmoid(zf + fb)
        lf = jnp.log(f)
        k = 1.0 - f
        if t_real < L:
            lf = jnp.where(rows < t_real, lf, 0.0)
            k = jnp.where(rows < t_real, k, 0.0)
        bcum = _cumsum(lf, 0)
        lev0 = n_lev - max(1, (t_real - 1).bit_length())
        refs_all = _dot_f32(pall_ref[lev0 * L:, :], bcum)
        att = jnp.where(eye, jnp.sum(q * k, axis=1, keepdims=True), 0.0)
        for lev in range(lev0, n_lev):
            r = refs_all[(lev - lev0) * L:(lev - lev0 + 1) * L]
            qt = q * jnp.exp(jnp.minimum(bcum - r, 0.0))
            kt = k * jnp.exp(jnp.minimum(r - bcum, 0.0))
            att = att + _dot_nt(qt, kt) * lmask_ref[lev]
        s_t = st_scr[h]
        o = _dot(att, v) + _dot_nt(q * jnp.exp(bcum), s_t)
        bl = bcum[L - 1:L, :]
        st_scr[h] = jnp.exp(bl) * s_t + _dot(v.T, k * jnp.exp(bl - bcum))
        on = _rms(o, gn) * _silu(gate)
        o_ref[h] = on[0:o_ref.shape[1]].astype(o_ref.dtype)
        return carry

    lax.fori_loop(0, H, head, 0, unroll=2)

    @pl.when(ci == nchunks - 1)
    def _():
        for h in range(H):
            s_out[h] = st_scr[h].T


def _hgrn(z_slab, lower, f_b, g_norm, state=None, L=128):
    b, ns, t, _ = z_slab.shape
    pad = t < 8
    L = L if pad else min(L, t)
    lr = t if pad else L
    nch = 1 if pad else t // L
    H = HG_HEADS
    pall, lmask = _hgrn_levels(L)
    lowfb = jnp.concatenate([lower.reshape(H, HG_DK), f_b.reshape(H, HG_DK)], axis=0).astype(F32)
    specs = [pl.BlockSpec((None, ns, lr, LANES), lambda i, c: (i, 0, c, 0)),
             pl.BlockSpec((2 * H, LANES), lambda i, c: (0, 0)),
             pl.BlockSpec((1, LANES), lambda i, c: (0, 0)),
             pl.BlockSpec(pall.shape, lambda i, c: (0, 0)),
             pl.BlockSpec(lmask.shape, lambda i, c: (0, 0, 0))]
    args = [z_slab, lowfb, g_norm.reshape(1, HG_DV).astype(F32), pall, lmask]
    if state is not None:
        specs.append(pl.BlockSpec((None, H, HG_DK, HG_DV), lambda i, c: (i, 0, 0, 0)))
        args.append(state.astype(F32))
    scratch = [pltpu.VMEM((H, HG_DV, HG_DK), F32)]
    if pad:
        scratch.append(pltpu.VMEM((ns, L, LANES), F32))
    kern = functools.partial(_hgrn_kernel, L=L, t_real=t if pad else L, has_state=state is not None, pad=pad)
    return pl.pallas_call(
        kern, grid=(b, nch), in_specs=specs,
        out_specs=[pl.BlockSpec((None, H, lr, LANES), lambda i, c: (i, 0, c, 0)),
                   pl.BlockSpec((None, H, HG_DK, HG_DV), lambda i, c: (i, 0, 0, 0))],
        out_shape=[jax.ShapeDtypeStruct((b, H, t, LANES), BF16),
                   jax.ShapeDtypeStruct((b, H, HG_DK, HG_DV), F32)],
        scratch_shapes=scratch, compiler_params=_cparams('arbitrary', 'arbitrary'), name='hgrn')(*args)


def _mla_attn_kernel(ql_ref, qr_ref, lc_ref, lr_ref, o_ref, lct_ref, m_ref, l_ref, acc_ref, *, tq, t_len):
    qi = pl.program_id(1)
    q0 = qi * tq
    kt = 256
    hg_n = 4
    scale = (MLA_NOPE + MLA_ROPE) ** -0.5
    mq = hg_n * tq
    krow = lax.broadcasted_iota(jnp.int32, (kt, mq), 0)
    qpos = q0 + (lax.broadcasted_iota(jnp.int32, (kt, mq), 1) & (tq - 1))
    per_slab = LANES // MLA_ROPE

    @pl.when(qi == 0)
    def _():
        for jt in range(t_len // kt):
            lct_ref[jt] = lc_ref[jt * kt:(jt + 1) * kt, :].T.astype(BF16)

    zeros_r = jnp.zeros((LANES - MLA_ROPE, tq), F32)
    n_par = m_ref.shape[0]
    n_full = q0 // kt
    for hg0 in range(0, MLA_HEADS // hg_n, n_par):
        groups = []
        for u in range(n_par):
            hg = hg0 + u
            heads = [hg * hg_n + i for i in range(hg_n)]
            q_lat = jnp.concatenate(
                [ql_ref[:, h * MLA_KV_LORA:(h + 1) * MLA_KV_LORA].astype(F32).T for h in heads],
                axis=1).astype(BF16)
            slab_t = qr_ref[:, hg * LANES:(hg + 1) * LANES].astype(F32).T
            q_rope = jnp.concatenate(
                [jnp.concatenate([slab_t[(h % per_slab) * MLA_ROPE:(h % per_slab + 1) * MLA_ROPE], zeros_r],
                                 axis=0) for h in heads], axis=1).astype(BF16)
            stats = (m_ref.at[u], l_ref.at[u], acc_ref.at[u])
            _online_init(*stats)
            groups.append((heads, q_lat, q_rope, stats))

        def tile(j, masked):
            start = pl.multiple_of(j * kt, kt)
            kc = lc_ref[pl.ds(start, kt), :].astype(BF16)
            kr = lr_ref[pl.ds(start, kt), :].astype(BF16)
            v_t = lct_ref[j]
            mask = (start + krow <= qpos) if masked else None
            for _, q_lat, q_rope, stats in groups:
                _online_update_t((_dot(kc, q_lat) + _dot(kr, q_rope)) * scale, mask, v_t, *stats)

        def body(j, carry):
            tile(j, False)
            return carry

        lax.fori_loop(0, n_full, body, 0)
        tile(n_full, True)
        for heads, _, _, stats in groups:
            o = _online_result(stats[1], stats[2])
            for i, h in enumerate(heads):
                o_ref[:, h * MLA_KV_LORA:(h + 1) * MLA_KV_LORA] = o[:, i * tq:(i + 1) * tq].T.astype(o_ref.dtype)


def _mla_attn(q_lat, q_rope, lat, b, t, tq=128):
    tq = min(tq, t)
    wl = MLA_HEADS * MLA_KV_LORA
    wr = MLA_HEADS * MLA_ROPE
    lat3 = lat.reshape(b, t, lat.shape[-1])
    c_blk = MLA_Q_LORA // MLA_KV_LORA
    r_blk = (MLA_Q_LORA + MLA_KV_LORA) // LANES
    return pl.pallas_call(
        functools.partial(_mla_attn_kernel, tq=tq, t_len=t), grid=(b, t // tq),
        in_specs=[pl.BlockSpec((None, tq, wl), lambda i, j: (i, j, 0)),
                  pl.BlockSpec((None, tq, wr), lambda i, j: (i, j, 0)),
                  pl.BlockSpec((None, t, MLA_KV_LORA), lambda i, j: (i, 0, c_blk)),
                  pl.BlockSpec((None, t, LANES), lambda i, j: (i, 0, r_blk))],
        out_specs=pl.BlockSpec((None, tq, wl), lambda i, j: (i, j, 0)),
        out_shape=jax.ShapeDtypeStruct((b, t, wl), BF16),
        scratch_shapes=[pltpu.VMEM((t // 256, MLA_KV_LORA, 256), BF16),
                        pltpu.VMEM((2, 1, 4 * tq), F32), pltpu.VMEM((2, 1, 4 * tq), F32),
                        pltpu.VMEM((2, MLA_KV_LORA, 4 * tq), F32)],
        compiler_params=_cparams('arbitrary', 'arbitrary'), name='mla_attn')(
            q_lat.reshape(b, t, wl), q_rope.reshape(b, t, wr), lat3, lat3).reshape(b * t, wl)


def _pad_cols(w, n):
    return jnp.pad(w, ((0, 0), (0, n - w.shape[1])))


def _prep_nsa(w_in, pe, w1, w2, w_out):
    nq = NSA_HEADS * NSA_DH
    kvw = 2 * NSA_G * NSA_DH
    w2 = w2.astype(BF16)
    z = jnp.zeros_like(w2)
    w2p = jnp.stack([jnp.concatenate([w2, z], axis=-1), jnp.concatenate([z, w2], axis=-1)], axis=1)
    return dict(q=w_in[:, :nq].astype(BF16), kv=w_in[:, nq:nq + 3 * kvw].astype(BF16),
                g=_pad_cols(w_in[:, nq + 3 * kvw:], LANES).astype(BF16),
                pe=pe.astype(F32).reshape(2, CMP_LEN // CMP_STRIDE, CMP_STRIDE * NSA_DH),
                w1=w1.astype(BF16), w2p=w2p, out=w_out.astype(BF16))


def _prep_ml(w_in, gate_b, w_out):
    a = 2 * ML_HEADS * ML_DK + ML_HEADS * ML_DV
    main = jnp.concatenate([w_in[:, :a], w_in[:, a + 2 * ML_HEADS:]], axis=1)
    return dict(main=main.astype(BF16), gate=_pad_cols(w_in[:, a:a + 2 * ML_HEADS], LANES).astype(BF16),
                gate_b=gate_b, out=w_out.astype(BF16))


def _prep_mla(w_in, q_norm, kv_norm, w_uq, w_uk, w_uv, w_out):
    uq = w_uq.reshape(MLA_Q_LORA, MLA_HEADS, MLA_NOPE + MLA_ROPE)
    return dict(w_in=_pad_cols(w_in, 7 * LANES).astype(BF16),
                q_norm=q_norm.reshape(1, -1).astype(F32), kv_norm=kv_norm.reshape(1, -1).astype(F32),
                uq_nope=uq[:, :, :MLA_NOPE].reshape(MLA_Q_LORA, -1).astype(BF16),
                uq_rope=uq[:, :, MLA_NOPE:].reshape(MLA_Q_LORA, -1).astype(BF16),
                ukt=jnp.transpose(w_uk, (1, 2, 0)).astype(BF16),
                uv=jnp.transpose(w_uv, (1, 0, 2)).astype(BF16), out=w_out.astype(BF16))


def _sig_tile2(z, j):
    return jnp.where(j == 2, _sigmoid(z), z)


def _rope64_epi(z, j, c, s):
    return _rope_tile(z, c, s, NSA_DH // 2)


def _rope_kv_epi(z, j, c, s):
    return jnp.concatenate([_rope_tile(z[:, :LANES], c, s, NSA_DH // 2), z[:, LANES:]], axis=1)


def _rope32_epi(z, j, c, s):
    return _rope_tile(z, c, s, MLA_ROPE // 2)


def _sigmoid_epi(z, j):
    return _sigmoid(z)


def _mla_in_epi(z, j, qn, kvn, c, s):
    a, bnd = MLA_Q_LORA, MLA_Q_LORA + MLA_KV_LORA
    return jnp.concatenate([_rms(z[:, :a], qn), _rms(z[:, a:bnd], kvn),
                            _rope_tile(z[:, bnd:], c, s, MLA_ROPE // 2)], axis=1)


def _const_aux(a):
    return (a, a.shape, lambda i, j: (0,) * a.ndim)


def _nsa_fresh(h, g, w, b, t, tabs):
    cos, sin = tabs[NSA_DH // 2]
    tm = min(1024, t)
    rope = _rope_aux(cos, sin, tm, t // tm)
    q = _mm(h, w['q'], g=g, epi=_rope64_epi, aux=rope, out_dtype=BF16, layout='slab', seq=t)
    kv = _mm(h, w['kv'], g=g, epi=_rope_kv_epi, aux=rope, layout='tiles', seq=t, tn=256)
    gates = _mm(h, w['g'], g=g, epi=_sigmoid_epi, seq=t)
    rows = [kv[i].reshape(b, t, 256) for i in range(3)]
    kc, vc = _nsa_compress(rows[0], w['pe'], w['w1'], w['w2p'])
    o = _nsa_attn(q, gates.reshape(b, t, LANES), kc, vc, rows[1], rows[2])
    h = _mm(o, w['out'], res=h, x_slab_seq=t)
    return h, rows


def _ml_fresh(h, g, w, b, t):
    z = _mm(h, w['main'], g=g, epi=_sig_tile2, layout='slab', seq=t)
    gc = _mm(h, w['gate'], g=g, seq=t).reshape(b, t, LANES)
    gr = jnp.swapaxes(gc[:, :, :2 * ML_HEADS], 1, 2)
    hs, c_f, n_f, m_f = _mlstm(z, gc, gr, w['gate_b'])
    return _mm(hs, w['out'], res=h, x_slab_seq=t), (c_f, n_f, m_f)


def _mla_fresh(h, g, w, b, t, tabs):
    cos, sin = tabs[MLA_ROPE // 2]
    tm = min(1024, t)
    rope = _rope_aux(cos, sin, tm, t // tm)
    lat = _mm(h, w['w_in'], g=g, epi=_mla_in_epi,
              aux=[_const_aux(w['q_norm']), _const_aux(w['kv_norm'])] + rope, seq=t)
    qn = _mm(lat, w['uq_nope'], x_cols=(MLA_Q_LORA, 0), out_dtype=BF16, seq=t)
    qr = _mm(lat, w['uq_rope'], x_cols=(MLA_Q_LORA, 0), epi=_rope32_epi, aux=rope, out_dtype=BF16, seq=t)
    ql = _headmm(qn, w['ukt'])
    ol = _mla_attn(ql, qr, lat, b, t)
    o = _headmm(ol, w['uv'])
    h = _mm(o, w['out'], res=h)
    new_lat = lat[:, MLA_Q_LORA:MLA_Q_LORA + MLA_KV_LORA + MLA_ROPE].reshape(b, t, -1)
    return h, new_lat


def _hg_fresh(h, g, w_in, lower, f_b, g_norm, w_out, b, t):
    z = _mm(h, w_in, g=g, layout='slab', seq=t)
    os_, s_f = _hgrn(z, lower, f_b, g_norm)
    return _mm(os_, w_out, res=h, x_slab_seq=t), s_f


PAGES_PER_STEP = 8


def _page_specs(width, n):
    return [pl.BlockSpec((None, width, PAGE), functools.partial(
        lambda i, s, pt, k: (pt[i, s * n + k], 0, 0), k=k)) for k in range(n)]


def _pages_feature_major(pool):
    nd = pool.ndim
    return jnp.transpose(pool, (0,) + tuple(range(2, nd)) + (1,)).reshape(pool.shape[0], -1, pool.shape[1])


def _stack8(x):
    return jnp.concatenate([x] * (NSA_HEADS // NSA_G), axis=0)


def _nsa_cmp_past_kernel(pt_ref, *refs, n_pg, t, start, nc):
    pages = refs[:n_pg]
    q_ref, pe_ref, w1_ref, w2p_ref, ocmp_ref, imp_ref, k_scr, v_scr = refs[n_pg:]
    s_i = pl.program_id(1)
    for k in range(n_pg):
        off = pl.multiple_of((s_i * n_pg + k) * PAGE, PAGE)
        rows = pages[k][...].T
        k_scr[pl.ds(off, PAGE), :] = rows[:, 0:LANES]
        v_scr[pl.ds(off, PAGE), :] = rows[:, LANES:2 * LANES]

    @pl.when(s_i == pl.num_programs(1) - 1)
    def _():
        n_seg = k_scr.shape[0] // CMP_STRIDE
        kc, vc = _compress_rows(k_scr, v_scr, n_seg, pe_ref, w1_ref, w2p_ref)
        q = q_ref[...]
        r_n = q.shape[0]
        cidx = lax.broadcasted_iota(jnp.int32, (r_n, n_seg), 1)
        qpos = start + lax.rem(lax.broadcasted_iota(jnp.int32, (r_n, n_seg), 0), t)
        mask = (cidx < nc) & (cidx * CMP_STRIDE + (CMP_LEN - 1) <= qpos)
        p = _softmax_tile(_dot_nt(q, kc) * (NSA_DH ** -0.5), mask)
        ocmp_ref[...] = _dot(p, vc)
        gt = NSA_G * t
        imp = p[0:gt]
        for hh in range(1, r_n // gt):
            imp = imp + p[hh * gt:(hh + 1) * gt]
        imp_ref[...] = imp


def _nsa_cmp_past(page_table, cache, q_ext, w, t, start):
    b, npg = page_table.shape
    n_pg = min(PAGES_PER_STEP, npg)
    r_n = q_ext.shape[1]
    n_seg = npg * PAGE // CMP_STRIDE
    nc = (start + t) // CMP_STRIDE - CMP_LEN // CMP_STRIDE + 1
    full = lambda a: pl.BlockSpec(a.shape, lambda i, s, pt: (0,) * a.ndim)
    gs = pltpu.PrefetchScalarGridSpec(
        num_scalar_prefetch=1, grid=(b, npg // n_pg),
        in_specs=_page_specs(256, n_pg) + [pl.BlockSpec((None, r_n, LANES), lambda i, s, pt: (i, 0, 0)),
                                           full(w['pe']), full(w['w1']), full(w['w2p'])],
        out_specs=[pl.BlockSpec((None, r_n, LANES), lambda i, s, pt: (i, 0, 0)),
                   pl.BlockSpec((None, NSA_G * t, n_seg), lambda i, s, pt: (i, 0, 0))],
        scratch_shapes=[pltpu.VMEM((npg * PAGE, LANES), F32), pltpu.VMEM((npg * PAGE, LANES), F32)])
    return pl.pallas_call(
        functools.partial(_nsa_cmp_past_kernel, n_pg=n_pg, t=t, start=start, nc=nc), grid_spec=gs,
        out_shape=[jax.ShapeDtypeStruct((b, r_n, LANES), F32),
                   jax.ShapeDtypeStruct((b, NSA_G * t, n_seg), F32)],
        compiler_params=_cparams('arbitrary', 'arbitrary'), name='nsa_cmp_past')(
            page_table, *([cache] * n_pg), q_ext, w['pe'], w['w1'], w['w2p']), nc


def _nsa_select_kernel(imp_ref, ovt_ref, pos_ref, sel_ref, sc_scr, rank_scr, *, ns, n_sel):
    shape = sc_scr.shape
    blk = lax.broadcasted_iota(jnp.int32, shape, 0)
    cur = pos_ref[...] // SLC_BLOCK
    forced = (blk == 0) | (blk == cur) | (blk == cur - 1)
    valid = blk <= cur
    sc = _dot_nt_f32(ovt_ref[...], imp_ref[...])
    sc_scr[...] = jnp.where(valid, sc + jnp.where(forced, FORCE_BONUS, 0.0), NEG)
    rank_scr[...] = jnp.zeros(shape, F32)

    def body(j, carry):
        sc_all = sc_scr[...]
        rj = sc_scr[pl.ds(j, 1), :]
        beats = (rj > sc_all) | ((rj == sc_all) & (blk > j))
        rank_scr[...] += jnp.where(beats, 1.0, 0.0)
        return carry

    lax.fori_loop(0, ns, body, 0)
    sel_ref[...] = jnp.where(valid & (rank_scr[...] < n_sel), 1.0, 0.0)


def _nsa_select_past(imp, nc, t, start):
    rows, n_seg = imp.shape
    ns = -(-(start + t) // SLC_BLOCK)
    nsp = -(-ns // 8) * 8
    ovt = _overlap_t(nc, ns, n_seg, nsp)
    pos = (start + jnp.arange(rows, dtype=jnp.int32) % t).reshape(1, rows)
    return pl.pallas_call(
        functools.partial(_nsa_select_kernel, ns=ns, n_sel=min(SLC_TOPN, ns)),
        out_shape=jax.ShapeDtypeStruct((nsp, rows), F32),
        scratch_shapes=[pltpu.VMEM((nsp, rows), F32), pltpu.VMEM((nsp, rows), F32)],
        compiler_params=pltpu.CompilerParams(vmem_limit_bytes=VMEM_LIMIT), name='nsa_select')(imp, ovt, pos)


def _nsa_slcwin_past_kernel(pt_ref, *refs, n_pg, t, wl):
    pages = refs[:n_pg]
    (msk_ref, q_ref, ns_ref, nw_ref, wb_ref, nf_ref, gt_ref, ocmp_ref, o_ref,
     m_ref, l_ref, acc_ref, oslc_scr, pad_scr) = refs[n_pg:]
    s_i = pl.program_id(1)
    scale = NSA_DH ** -0.5
    q = q_ref[...]
    r_n = q.shape[0]

    @pl.when(s_i == 0)
    def _():
        _online_init(m_ref, l_ref, acc_ref)

    s = jnp.concatenate([_dot(q, pg[0:LANES, :]) * scale for pg in pages], axis=1)
    v_t = jnp.concatenate([pg[LANES:2 * LANES, :] for pg in pages], axis=1)
    mask = _stack8(msk_ref[...].astype(F32)) > 0.5
    _online_update(s, mask, v_t, m_ref, l_ref, acc_ref, v_t=True)

    @pl.when(s_i == pl.num_programs(1) - 1)
    def _():
        col = lax.broadcasted_iota(jnp.int32, (r_n, PAGE), 1)
        tq = lax.rem(lax.broadcasted_iota(jnp.int32, (r_n, PAGE), 0), t)
        new_ok = (col < t) & (col <= tq)

        def padded(ref):
            pad_scr[...] = jnp.zeros(pad_scr.shape, F32)
            pad_scr[0:t, :] = ref[...]
            return pad_scr[...]

        rows = padded(ns_ref)
        _online_update(_dot_nt(q, rows[:, 0:LANES]) * scale, new_ok & (_stack8(nf_ref[...]) > 0.5),
                       rows[:, LANES:], m_ref, l_ref, acc_ref)
        oslc_scr[...] = _online_result(l_ref, acc_ref)

        _online_init(m_ref, l_ref, acc_ref)
        colw = lax.broadcasted_iota(jnp.int32, (r_n, wl), 1)
        tqw = lax.rem(lax.broadcasted_iota(jnp.int32, (r_n, wl), 0), t)
        _online_update(_dot(q, wb_ref[0:LANES, :]) * scale, colw > tqw + (wl - WINDOW),
                       wb_ref[LANES:2 * LANES, :], m_ref, l_ref, acc_ref, v_t=True)
        rows = padded(nw_ref)
        _online_update(_dot_nt(q, rows[:, 0:LANES]) * scale, new_ok, rows[:, LANES:], m_ref, l_ref, acc_ref)
        o_win = _online_result(l_ref, acc_ref)
        gt = gt_ref[...]
        o_ref[...] = gt[:, 0:1] * ocmp_ref[...] + gt[:, 1:2] * oslc_scr[...] + gt[:, 2:3] * o_win


def _nsa_slcwin_past(page_table, cache, key_mask, q_ext, new_slc, new_win, win_buf, new_flag, gates_r, ocmp, t):
    b, npg = page_table.shape
    n_pg = min(PAGES_PER_STEP, npg)
    r_n = q_ext.shape[1]
    gtn = NSA_G * t
    wl = win_buf.shape[2]
    per_b = lambda shp: pl.BlockSpec((None,) + shp, lambda i, s, pt: (i,) + (0,) * len(shp))
    gs = pltpu.PrefetchScalarGridSpec(
        num_scalar_prefetch=1, grid=(b, npg // n_pg),
        in_specs=_page_specs(256, n_pg) + [
            pl.BlockSpec((None, gtn, n_pg * PAGE), lambda i, s, pt: (i, 0, s)),
            per_b((r_n, LANES)), per_b((t, 256)), per_b((t, 256)), per_b((256, wl)),
            per_b((gtn, LANES)), per_b((r_n, LANES)), per_b((r_n, LANES))],
        out_specs=per_b((r_n, LANES)),
        scratch_shapes=[pltpu.VMEM((r_n, 1), F32), pltpu.VMEM((r_n, 1), F32), pltpu.VMEM((r_n, LANES), F32),
                        pltpu.VMEM((r_n, LANES), F32), pltpu.VMEM((PAGE, 256), F32)])
    return pl.pallas_call(
        functools.partial(_nsa_slcwin_past_kernel, n_pg=n_pg, t=t, wl=wl), grid_spec=gs,
        out_shape=jax.ShapeDtypeStruct((b, r_n, LANES), F32),
        compiler_params=_cparams('arbitrary', 'arbitrary'), name='nsa_slcwin_past')(
            page_table, *([cache] * n_pg), key_mask, q_ext, new_slc, new_win, win_buf, new_flag, gates_r, ocmp)


def _mla_past_kernel(pt_ref, *refs, n_pg, t):
    pages = refs[:n_pg]
    ql_ref, qr_ref, new_ref, o_ref, m_ref, l_ref, acc_ref, pad_scr = refs[n_pg:]
    s_i = pl.program_id(1)
    scale = (MLA_NOPE + MLA_ROPE) ** -0.5
    ql = ql_ref[...]
    qr = qr_ref[...]
    r_n = ql.shape[0]

    @pl.when(s_i == 0)
    def _():
        _online_init(m_ref, l_ref, acc_ref)

    def scores(rows):
        kc = rows[:, 0:MLA_KV_LORA].astype(BF16)
        kr = rows[:, MLA_KV_LORA:MLA_KV_LORA + MLA_ROPE]
        return (_dot_nt(ql, kc) + _dot_nt(qr, kr)) * scale, kc

    kc_t = [pg[0:MLA_KV_LORA, :].astype(BF16) for pg in pages]
    s = [(_dot(ql, kc) + _dot(qr, pg[MLA_KV_LORA:MLA_KV_LORA + MLA_ROPE, :])) * scale
         for kc, pg in zip(kc_t, pages)]
    _online_update(jnp.concatenate(s, axis=1), None, jnp.concatenate(kc_t, axis=1),
                   m_ref, l_ref, acc_ref, v_t=True)

    @pl.when(s_i == pl.num_programs(1) - 1)
    def _():
        pad_scr[...] = jnp.zeros(pad_scr.shape, F32)
        pad_scr[0:t, :] = new_ref[...]
        s, kc = scores(pad_scr[...])
        col = lax.broadcasted_iota(jnp.int32, (r_n, PAGE), 1)
        tq = lax.rem(lax.broadcasted_iota(jnp.int32, (r_n, PAGE), 0), t)
        _online_update(s, (col < t) & (col <= tq), kc, m_ref, l_ref, acc_ref)
        o_ref[...] = _online_result(l_ref, acc_ref).astype(o_ref.dtype)


def _mla_past_attn(page_table, cache, ql, qr, new_lat, t):
    b, npg = page_table.shape
    n_pg = min(PAGES_PER_STEP, npg)
    r_n = ql.shape[1]
    width = cache.shape[1]
    per_b = lambda shp: pl.BlockSpec((None,) + shp, lambda i, s, pt: (i,) + (0,) * len(shp))
    gs = pltpu.PrefetchScalarGridSpec(
        num_scalar_prefetch=1, grid=(b, npg // n_pg),
        in_specs=_page_specs(width, n_pg) + [per_b((r_n, MLA_KV_LORA)), per_b((r_n, MLA_ROPE)), per_b((t, width))],
        out_specs=per_b((r_n, MLA_KV_LORA)),
        scratch_shapes=[pltpu.VMEM((r_n, 1), F32), pltpu.VMEM((r_n, 1), F32), pltpu.VMEM((r_n, MLA_KV_LORA), F32),
                        pltpu.VMEM((PAGE, width), F32)])
    return pl.pallas_call(
        functools.partial(_mla_past_kernel, n_pg=n_pg, t=t), grid_spec=gs,
        out_shape=jax.ShapeDtypeStruct((b, r_n, MLA_KV_LORA), BF16),
        compiler_params=_cparams('arbitrary', 'arbitrary'), name='mla_past')(
            page_table, *([cache] * n_pg), ql, qr, new_lat)


def _to_slab(z, b, t):
    return jnp.transpose(z.reshape(b, t, -1, LANES), (0, 2, 1, 3))


def _from_slab(s, b, t):
    return jnp.transpose(s, (0, 2, 1, 3)).reshape(b * t, -1)


def _nsa_past(h, g, w, b, t, tabs, start, past, occ):
    assert t < CMP_STRIDE and start % SLC_BLOCK == 0
    cos, sin = tabs[NSA_DH // 2]
    m = b * t
    rope = _rope_aux(cos, sin, min(1024, m), None)
    q = _mm(h, w['q'], g=g, epi=_rope64_epi, aux=rope, out_dtype=BF16)
    kv = _mm(h, w['kv'], g=g, epi=_rope_kv_epi, aux=rope, layout='tiles', tn=256)
    gates = _mm(h, w['g'], g=g, epi=_sigmoid_epi)
    rows = [kv[i].reshape(b, t, 256) for i in range(3)]
    hpg = NSA_HEADS // NSA_G
    r_n = hpg * NSA_G * t
    q5 = jnp.transpose(q.reshape(b, t, NSA_G, hpg, NSA_DH), (0, 3, 2, 1, 4))
    eye = jnp.eye(NSA_G, dtype=q5.dtype)
    q_ext = (q5[:, :, :, :, None, :] * eye[None, None, :, None, :, None]).reshape(b, r_n, NSA_G * NSA_DH)
    pt = past['page_table']
    width = 2 * NSA_G * NSA_DH
    cmp_cache = _pages_feature_major(past['nsa_cmp'][occ])
    slc_cache = _pages_feature_major(past['nsa_slc'][occ])
    win_buf = past['nsa_win'][occ].reshape(b, -1, width)
    win_buf_t = _pages_feature_major(past['nsa_win'][occ])
    (ocmp, imp), nc = _nsa_cmp_past(pt, cmp_cache, q_ext, w, t, start)
    gtn = NSA_G * t
    sel = _nsa_select_past(imp.reshape(b * gtn, -1), nc, t, start)
    sel = jnp.transpose(sel).reshape(b, gtn, -1)
    n_past_blk = start // SLC_BLOCK
    key_mask = jnp.repeat(sel[:, :, :n_past_blk], SLC_BLOCK, axis=-1).astype(BF16)
    new_flag = jnp.broadcast_to(sel[:, :, n_past_blk:n_past_blk + 1], (b, gtn, LANES))
    g5 = jnp.transpose(gates[:, :3 * NSA_HEADS].reshape(b, t, 3, NSA_G, hpg), (0, 4, 3, 1, 2))
    gates_r = jnp.pad(g5.reshape(b, r_n, 3), ((0, 0), (0, 0), (0, LANES - 3)))
    o = _nsa_slcwin_past(pt, slc_cache, key_mask, q_ext, rows[1], rows[2], win_buf_t, new_flag, gates_r, ocmp, t)
    o5 = o.reshape(b, hpg, NSA_G, t, NSA_G, NSA_DH)
    o = jnp.stack([o5[:, :, gi, :, gi, :] for gi in range(NSA_G)], axis=2)
    o = jnp.transpose(o, (0, 3, 2, 1, 4)).reshape(m, NSA_HEADS * NSA_DH).astype(BF16)
    h = _mm(o, w['out'], res=h)
    win = jnp.concatenate([win_buf, rows[2]], axis=1)[:, -win_buf.shape[1]:]
    return h, rows, win


def _ml_past(h, g, w, b, t, past, occ):
    z = _to_slab(_mm(h, w['main'], g=g, epi=_sig_tile2), b, t)
    gc = _mm(h, w['gate'], g=g).reshape(b, t, LANES)
    gr = jnp.swapaxes(gc[:, :, :2 * ML_HEADS], 1, 2)
    hs, c_f, n_f, m_f = _mlstm(z, gc, gr, w['gate_b'],
                               state=(past['ml_C'][occ], past['ml_n'][occ], past['ml_m'][occ]))
    return _mm(_from_slab(hs, b, t), w['out'], res=h), (c_f, n_f, m_f)


def _mla_past(h, g, w, b, t, tabs, past, occ):
    cos, sin = tabs[MLA_ROPE // 2]
    m = b * t
    rope = _rope_aux(cos, sin, min(1024, m), None)
    lat = _mm(h, w['w_in'], g=g, epi=_mla_in_epi,
              aux=[_const_aux(w['q_norm']), _const_aux(w['kv_norm'])] + rope)
    qn = _mm(lat, w['uq_nope'], x_cols=(MLA_Q_LORA, 0), out_dtype=BF16)
    qr = _mm(lat, w['uq_rope'], x_cols=(MLA_Q_LORA, 0), epi=_rope32_epi, aux=rope, out_dtype=BF16)
    ql = _headmm(qn, w['ukt'])
    new_lat = lat[:, MLA_Q_LORA:MLA_Q_LORA + MLA_KV_LORA + MLA_ROPE].reshape(b, t, -1)
    hd = lambda a: jnp.transpose(a.reshape(b, t, MLA_HEADS, -1), (0, 2, 1, 3)).reshape(b, MLA_HEADS * t, -1)
    cache = _pages_feature_major(past['mla'][occ])
    ol = _mla_past_attn(past['page_table'], cache, hd(ql), hd(qr), new_lat, t)
    ol = jnp.transpose(ol.reshape(b, MLA_HEADS, t, -1), (0, 2, 1, 3)).reshape(m, -1)
    o = _headmm(ol, w['uv'])
    return _mm(o, w['out'], res=h), new_lat


def _hg_past(h, g, w_in, lower, f_b, g_norm, w_out, b, t, state):
    z = _to_slab(_mm(h, w_in, g=g), b, t)
    os_, s_f = _hgrn(z, lower, f_b, g_norm, state=state)
    return _mm(_from_slab(os_, b, t), w_out, res=h), s_f


def _prepare(prm):
    depth = prm['norm_mix'].shape[0]
    sm = jax.nn.softmax(prm['hg_lb_logits'].astype(F32), axis=0)
    lower = jnp.cumsum(sm, axis=0) - sm[0]
    w = dict(depth=depth, lower=lower)
    w['nsa'] = [_prep_nsa(prm['nsa_w_in'][o], prm['nsa_cmp_pe'][o], prm['nsa_cmp_w1'][o], prm['nsa_cmp_w2'][o],
                          prm['nsa_w_out'][o]) for o in range(prm['nsa_w_in'].shape[0])]
    w['ml'] = [_prep_ml(prm['ml_w_in'][o], prm['ml_gate_b'][o], prm['ml_w_out'][o])
               for o in range(prm['ml_w_in'].shape[0])]
    w['mla'] = [_prep_mla(prm['mla_w_in'][o], prm['mla_q_norm'][o], prm['mla_kv_norm'][o], prm['mla_w_uq'][o],
                          prm['mla_w_uk'][o], prm['mla_w_uv'][o], prm['mla_w_out'][o])
                for o in range(prm['mla_w_in'].shape[0])]
    w['hg'] = [dict(w_in=prm['hg_w_in'][o].astype(BF16), f_b=prm['hg_f_b'][o], norm=prm['hg_norm'][o],
                    out=prm['hg_w_out'][o].astype(BF16)) for o in range(prm['hg_w_in'].shape[0])]
    w['ffn'] = [dict(up=prm['ffn_w_up'][i].astype(BF16),
                     conv4=jnp.concatenate([prm['ffn_conv_w'][i], prm['ffn_conv_b'][i][None]], axis=0).astype(F32),
                     down=prm['ffn_w_down'][i].astype(BF16)) for i in range(depth)]
    w['ple'] = [dict(proj=prm['ple_w_proj'][i].astype(BF16), gate=prm['ple_w_gate'][i].astype(BF16))
                for i in range(depth)]
    return w


def _trunk(x, p, start, past, prm, w):
    b, t, d = x.shape
    depth = w['depth']
    pos = start + jnp.arange(t, dtype=jnp.int32)
    if past is None:
        tabs = {hf: _rope_tables(pos, hf) for hf in (NSA_DH // 2, MLA_ROPE // 2)}
    else:
        tabs = {hf: tuple(jnp.tile(a, (b, 1)) for a in _rope_tables(pos, hf)) for hf in (NSA_DH // 2, MLA_ROPE // 2)}
    new = {}
    h = x.reshape(b * t, d)
    for i in range(depth):
        kind, occ = i % 4, i // 4
        g = prm['norm_mix'][i]
        if kind == 0:
            if past is None:
                h, rows = _nsa_fresh(h, g, w['nsa'][occ], b, t, tabs)
                win = rows[2][:, -min(WINDOW, t):]
            else:
                h, rows, win = _nsa_past(h, g, w['nsa'][occ], b, t, tabs, start, past, occ)
            for name, r in zip(('nsa_cmp', 'nsa_slc'), rows[:2]):
                new.setdefault(name, []).append(r.reshape(b, t, 2, NSA_G, NSA_DH))
            new.setdefault('nsa_win', []).append(win.reshape(b, win.shape[1], 2, NSA_G, NSA_DH))
        elif kind == 1:
            if past is None:
                h, st = _ml_fresh(h, g, w['ml'][occ], b, t)
            else:
                h, st = _ml_past(h, g, w['ml'][occ], b, t, past, occ)
            for name, s in zip(('ml_C', 'ml_n', 'ml_m'), st):
                new.setdefault(name, []).append(s)
        elif kind == 2:
            if past is None:
                h, lat = _mla_fresh(h, g, w['mla'][occ], b, t, tabs)
            else:
                h, lat = _mla_past(h, g, w['mla'][occ], b, t, tabs, past, occ)
            new.setdefault('mla', []).append(lat)
        else:
            hw = w['hg'][occ]
            if past is None:
                h, s_f = _hg_fresh(h, g, hw['w_in'], w['lower'][i], hw['f_b'], hw['norm'], hw['out'], b, t)
            else:
                h, s_f = _hg_past(h, g, hw['w_in'], w['lower'][i], hw['f_b'], hw['norm'], hw['out'], b, t,
                                  past['hg_S'][occ])
            new.setdefault('hg_S', []).append(s_f)
        fw = w['ffn'][i]
        f = fw['down'].shape[0]
        if past is None:
            h, sa, sg = _ffn(h, prm['norm_ffn'][i], fw['up'], fw['conv4'], fw['down'], t)
            new.setdefault('ffn_conv', []).append(jnp.concatenate([sa, sg], axis=-1))
        else:
            buf = past['ffn_conv'][i]
            zero = jnp.zeros((b, t - 1, 2 * f), F32)
            p1 = jnp.concatenate([buf[:, 1:2], zero], axis=1).reshape(b * t, 2 * f)
            p2 = jnp.concatenate([buf, zero[:, 1:]], axis=1).reshape(b * t, 2 * f)
            h, ua, ug = _ffn(h, prm['norm_ffn'][i], fw['up'], fw['conv4'], fw['down'], t, prev=(p1, p2))
            u = jnp.concatenate([ua, ug], axis=-1).reshape(b, t, 2 * f)
            new.setdefault('ffn_conv', []).append(u[:, -(CONV_W - 1):])
        pw = w['ple'][i]
        h = _ple(h, prm['norm_ple'][i], pw['gate'], p[i].reshape(b * t, -1), pw['proj'],
                 final_g=prm['norm_final'] if i == depth - 1 else None)
    return h.reshape(b, t, d), {k: jnp.stack(v) for k, v in new.items()}


def kernel(x_prompt, x_sample, cache_nsa_cmp_kv, cache_nsa_slc_kv, state_nsa_win_kv, cache_mla_latent,
           state_mlstm_C, state_mlstm_n, state_mlstm_m, state_hgrn_S, state_ffn_conv, page_table,
           p_prompt, p_sample, norm_mix, norm_ffn, norm_ple, norm_final, nsa_w_in, nsa_cmp_pe, nsa_cmp_w1,
           nsa_cmp_w2, nsa_w_out, ml_w_in, ml_gate_b, ml_w_out, mla_w_in, mla_q_norm, mla_kv_norm, mla_w_uq,
           mla_w_uk, mla_w_uv, mla_w_out, hg_w_in, hg_f_b, hg_lb_logits, hg_norm, hg_w_out, ffn_w_up,
           ffn_conv_w, ffn_conv_b, ffn_w_down, ple_w_proj, ple_w_gate):
    prm = {
        'norm_mix': norm_mix, 'norm_ffn': norm_ffn, 'norm_ple': norm_ple, 'norm_final': norm_final,
        'nsa_w_in': nsa_w_in, 'nsa_cmp_pe': nsa_cmp_pe, 'nsa_cmp_w1': nsa_cmp_w1, 'nsa_cmp_w2': nsa_cmp_w2,
        'nsa_w_out': nsa_w_out, 'ml_w_in': ml_w_in, 'ml_gate_b': ml_gate_b, 'ml_w_out': ml_w_out,
        'mla_w_in': mla_w_in, 'mla_q_norm': mla_q_norm, 'mla_kv_norm': mla_kv_norm, 'mla_w_uq': mla_w_uq,
        'mla_w_uk': mla_w_uk, 'mla_w_uv': mla_w_uv, 'mla_w_out': mla_w_out, 'hg_w_in': hg_w_in,
        'hg_f_b': hg_f_b, 'hg_lb_logits': hg_lb_logits, 'hg_norm': hg_norm, 'hg_w_out': hg_w_out,
        'ffn_w_up': ffn_w_up, 'ffn_conv_w': ffn_conv_w, 'ffn_conv_b': ffn_conv_b, 'ffn_w_down': ffn_w_down,
        'ple_w_proj': ple_w_proj, 'ple_w_gate': ple_w_gate,
    }
    past = {
        'nsa_cmp': cache_nsa_cmp_kv, 'nsa_slc': cache_nsa_slc_kv, 'nsa_win': state_nsa_win_kv,
        'mla': cache_mla_latent, 'ml_C': state_mlstm_C, 'ml_n': state_mlstm_n, 'ml_m': state_mlstm_m,
        'hg_S': state_hgrn_S, 'ffn_conv': state_ffn_conv, 'page_table': page_table,
    }
    w = _prepare(prm)
    past_len = page_table.shape[1] * PAGE
    y_p, sp = _trunk(x_prompt, p_prompt, 0, None, prm, w)
    y_s, ss = _trunk(x_sample, p_sample, past_len, past, prm, w)
    return (y_p, y_s,
            sp['nsa_cmp'], ss['nsa_cmp'], sp['nsa_slc'], ss['nsa_slc'], sp['nsa_win'], ss['nsa_win'],
            sp['mla'], ss['mla'], sp['ml_C'], ss['ml_C'], sp['ml_n'], ss['ml_n'], sp['ml_m'], ss['ml_m'],
            sp['hg_S'], ss['hg_S'], sp['ffn_conv'], ss['ffn_conv'])
```

```python
import functools
import math

import numpy as np
import jax
import jax.numpy as jnp
from jax import lax
from jax.experimental import pallas as pl
from jax.experimental.pallas import tpu as pltpu

F32 = jnp.float32
BF16 = jnp.bfloat16
NEG = -1e30
NORM_EPS = 1e-6
ROPE_THETA = 10000.0
FORCE_BONUS = 1e4

LANES = 128
VMEM_LIMIT = 56 * 1024 * 1024

PAGE = 128
NSA_HEADS, NSA_G, NSA_DH = 16, 2, 64
CMP_LEN, CMP_STRIDE, CMP_HID = 32, 16, 256
SLC_BLOCK, SLC_TOPN, WINDOW = 64, 16, 512
ML_HEADS, ML_DK, ML_DV = 4, 128, 256
MLA_HEADS, MLA_Q_LORA, MLA_KV_LORA, MLA_NOPE, MLA_ROPE, MLA_DV = 16, 512, 256, 64, 32, 64
HG_HEADS, HG_DK, HG_DV = 8, 128, 128
CONV_W = 3


def _cparams(*sem):
    return pltpu.CompilerParams(dimension_semantics=sem, vmem_limit_bytes=VMEM_LIMIT)


def _dot(a, b):
    return jnp.dot(a.astype(BF16), b.astype(BF16), preferred_element_type=F32)


def _dot_nt(a, b):
    return lax.dot_general(a.astype(BF16), b.astype(BF16), (((1,), (1,)), ((), ())),
                           preferred_element_type=F32)


def _dot_f32(a, b):
    return jnp.dot(a, b, precision=lax.Precision.HIGHEST, preferred_element_type=F32)


def _dot_nt_f32(a, b):
    return lax.dot_general(a, b, (((1,), (1,)), ((), ())), precision=lax.Precision.HIGHEST,
                           preferred_element_type=F32)


def _rms(x, g):
    return x * lax.rsqrt(jnp.mean(x * x, axis=-1, keepdims=True) + NORM_EPS) * g


def _sigmoid(x):
    return 1.0 / (1.0 + jnp.exp(-x))


def _silu(x):
    return x * _sigmoid(x)


def _rope_tile(z, cos, sin, half):
    n = z.shape[1]
    reps = n // LANES
    c = jnp.concatenate([cos] * reps, axis=1) if reps > 1 else cos
    s = jnp.concatenate([sin] * reps, axis=1) if reps > 1 else sin
    lane = lax.broadcasted_iota(jnp.int32, z.shape, 1)
    lower = (lane & (2 * half - 1)) < half
    partner = jnp.where(lower, pltpu.roll(z, n - half, 1), pltpu.roll(z, half, 1))
    return z * c + partner * s


def _rope_tables(pos, half):
    inv = jnp.power(ROPE_THETA, -jnp.arange(half, dtype=F32) / half)
    ang = pos.astype(F32)[:, None] * inv[None, :]
    cos, sin = jnp.cos(ang), jnp.sin(ang)
    reps = LANES // (2 * half)
    return (jnp.tile(jnp.concatenate([cos, cos], axis=1), (1, reps)),
            jnp.tile(jnp.concatenate([-sin, sin], axis=1), (1, reps)))


def _mm_kernel(*refs, norm, res, epi, n_aux, x_slabs, out_slabs):
    x_ref = refs[0]
    pos = 1
    g_ref = None
    if norm:
        g_ref = refs[pos]
        pos += 1
    w_ref = refs[pos]
    pos += 1
    aux = refs[pos:pos + n_aux]
    pos += n_aux
    r_ref = None
    if res:
        r_ref = refs[pos]
        pos += 1
    o_ref = refs[pos]
    j = pl.program_id(1)
    if norm:
        xn_ref = refs[pos + 1]

        @pl.when(j == 0)
        def _():
            xn_ref[...] = _rms(x_ref[...], g_ref[...]).astype(BF16)

        x = xn_ref[...]
    elif x_slabs:
        x = jnp.concatenate([x_ref[s] for s in range(x_slabs)], axis=1)
    else:
        x = x_ref[...]
    z = _dot(x, w_ref[...])
    if epi is not None:
        z = epi(z, j, *[a[...] for a in aux])
    if res:
        z = z + r_ref[...]
    if out_slabs:
        for s in range(out_slabs):
            o_ref[s] = z[:, s * LANES:(s + 1) * LANES].astype(o_ref.dtype)
    else:
        o_ref[...] = z.astype(o_ref.dtype)


def _mm(x, w, *, g=None, res=None, epi=None, aux=(), out_dtype=F32, layout='flat',
        seq=None, tm=1024, tn=1024, x_cols=None, x_slab_seq=None):
    k, n = w.shape
    if x_slab_seq is not None:
        b, ks, t, _ = x.shape
        m = b * t
    else:
        m = x.shape[0]
        t = seq
    tm = min(tm, m if t is None else t)
    tn = min(tn, n)
    assert m % tm == 0 and n % tn == 0
    ni, nj = m // tm, n // tn
    tps = None if t is None else t // tm
    in_specs, args = [], []
    if x_slab_seq is not None:
        in_specs.append(pl.BlockSpec((None, ks, tm, LANES), lambda i, j: (i // tps, 0, i % tps, 0)))
    elif x_cols is not None:
        in_specs.append(pl.BlockSpec((tm, x_cols[0]), lambda i, j: (i, x_cols[1])))
    else:
        in_specs.append(pl.BlockSpec((tm, k), lambda i, j: (i, 0)))
    args.append(x)
    if g is not None:
        in_specs.append(pl.BlockSpec((1, k), lambda i, j: (0, 0)))
        args.append(g.reshape(1, k).astype(F32))
    in_specs.append(pl.BlockSpec((k, tn), lambda i, j: (0, j)))
    args.append(w)
    for a, bs, im in aux:
        in_specs.append(pl.BlockSpec(bs, im))
        args.append(a)
    if res is not None:
        in_specs.append(pl.BlockSpec((tm, tn), lambda i, j: (i, j)))
        args.append(res)
    if layout == 'flat':
        out_shape = jax.ShapeDtypeStruct((m, n), out_dtype)
        out_spec = pl.BlockSpec((tm, tn), lambda i, j: (i, j))
        out_slabs = 0
    elif layout == 'tiles':
        out_shape = jax.ShapeDtypeStruct((nj, m, tn), out_dtype)
        out_spec = pl.BlockSpec((None, tm, tn), lambda i, j: (j, i, 0))
        out_slabs = 0
    else:
        out_slabs = tn // LANES
        out_shape = jax.ShapeDtypeStruct((m // t, n // LANES, t, LANES), out_dtype)
        out_spec = pl.BlockSpec((None, out_slabs, tm, LANES), lambda i, j: (i // tps, j, i % tps, 0))
    scratch = [pltpu.VMEM((tm, k), BF16)] if g is not None else []
    kern = functools.partial(_mm_kernel, norm=g is not None, res=res is not None, epi=epi,
                             n_aux=len(aux), x_slabs=(ks if x_slab_seq is not None else 0),
                             out_slabs=out_slabs)
    return pl.pallas_call(
        kern, grid=(ni, nj), in_specs=in_specs, out_specs=out_spec, out_shape=out_shape,
        scratch_shapes=scratch, compiler_params=_cparams('parallel', 'arbitrary'),
        name='mm')(*args)


def _rope_aux(cos, sin, tm, period_blocks):
    if period_blocks is None:
        im = lambda i, j: (i, 0)
    else:
        im = lambda i, j: (i % period_blocks, 0)
    return [(cos, (tm, LANES), im), (sin, (tm, LANES), im)]


def _headmm_kernel(x_ref, w_ref, o_ref, *, heads, a, c):
    for h in range(heads):
        o_ref[:, h * c:(h + 1) * c] = _dot(x_ref[:, h * a:(h + 1) * a], w_ref[h]).astype(o_ref.dtype)


def _headmm(x, w, out_dtype=BF16, tm=512):
    m = x.shape[0]
    heads, a, c = w.shape
    tm = min(tm, m)
    return pl.pallas_call(
        functools.partial(_headmm_kernel, heads=heads, a=a, c=c), grid=(m // tm,),
        in_specs=[pl.BlockSpec((tm, heads * a), lambda i: (i, 0)),
                  pl.BlockSpec((heads, a, c), lambda i: (0, 0, 0))],
        out_specs=pl.BlockSpec((tm, heads * c), lambda i: (i, 0)),
        out_shape=jax.ShapeDtypeStruct((m, heads * c), out_dtype),
        compiler_params=_cparams('parallel'), name='headmm')(x, w)


def _ffn_kernel(*refs, tps, seq_len, has_prev):
    if has_prev:
        (h_ref, g_ref, wa_ref, wg_ref, ca_ref, cg_ref, wd_ref, p1a_ref, p1g_ref, p2a_ref, p2g_ref,
         o_ref, ua_ref, ug_ref, hn_ref) = refs
    else:
        (h_ref, g_ref, wa_ref, wg_ref, ca_ref, cg_ref, wd_ref,
         o_ref, sa_ref, sg_ref, hn_ref, carry_ref) = refs
    i, j = pl.program_id(0), pl.program_id(1)
    tm = h_ref.shape[0]
    ch = min(64, tm)

    @pl.when(j == 0)
    def _():
        x = h_ref[...]
        hn_ref[...] = _rms(x, g_ref[...]).astype(BF16)
        o_ref[...] = x

    hn = hn_ref[...]
    ua = _dot(hn, wa_ref[...])
    ug = _dot(hn, wg_ref[...])
    if has_prev:
        assert ch % seq_len == 0
        ua_ref[...] = ua
        ug_ref[...] = ug
        prevs = (None, None)
    else:
        @pl.when(i % tps == 0)
        def _():
            carry_ref[j] = jnp.zeros(carry_ref.shape[1:], F32)

        prevs = (carry_ref[j, 0], carry_ref[j, 1])
        carry_ref[j, 0] = ua[tm - 8:]
        carry_ref[j, 1] = ug[tm - 8:]
        sa_ref[...] = pltpu.roll(ua[tm - 8:], CONV_W - 1, 0)[0:CONV_W - 1]
        sg_ref[...] = pltpu.roll(ug[tm - 8:], CONV_W - 1, 0)[0:CONV_W - 1]

    def conv_chunk(u, cw, prev8, p1_ref, p2_ref, r0):
        cur = u[r0:r0 + ch]
        if has_prev:
            t = lax.broadcasted_iota(jnp.int32, cur.shape, 0) % seq_len
            s1 = jnp.where(t >= 1, pltpu.roll(cur, 1, 0), p1_ref[r0:r0 + ch, :])
            s2 = jnp.where(t >= 2, pltpu.roll(cur, 2, 0), p2_ref[r0:r0 + ch, :])
        else:
            x = jnp.concatenate([prev8 if r0 == 0 else u[r0 - 8:r0], cur], axis=0)
            s1 = pltpu.roll(x, 1, 0)[8:]
            s2 = pltpu.roll(x, 2, 0)[8:]
        return cw[0:1] * s2 + cw[1:2] * s1 + cw[2:3] * cur + cw[3:4]

    cwa, cwg = ca_ref[...], cg_ref[...]
    acts = []
    for c in range(tm // ch):
        ca = conv_chunk(ua, cwa, prevs[0], p1a_ref if has_prev else None, p2a_ref if has_prev else None, c * ch)
        cg = conv_chunk(ug, cwg, prevs[1], p1g_ref if has_prev else None, p2g_ref if has_prev else None, c * ch)
        acts.append((_silu(ca) * cg).astype(BF16))
    act = jnp.concatenate(acts, axis=0) if len(acts) > 1 else acts[0]
    o_ref[...] += _dot(act, wd_ref[...])


def _ffn(h, g, w_up, conv4, w_down, seq_len, prev=None, tn=256):
    m, d = h.shape
    f = w_down.shape[0]
    nj = f // tn
    has_prev = prev is not None
    tm = m if has_prev else min(1024, seq_len)
    tps = max(seq_len // tm, 1)
    ni = m // tm
    specs = [pl.BlockSpec((tm, d), lambda i, j: (i, 0)),
             pl.BlockSpec((1, d), lambda i, j: (0, 0)),
             pl.BlockSpec((d, tn), lambda i, j: (0, j)),
             pl.BlockSpec((d, tn), lambda i, j: (0, j + nj)),
             pl.BlockSpec((4, tn), lambda i, j: (0, j)),
             pl.BlockSpec((4, tn), lambda i, j: (0, j + nj)),
             pl.BlockSpec((tn, d), lambda i, j: (j, 0))]
    args = [h, g.reshape(1, d).astype(F32), w_up, w_up, conv4, conv4, w_down]
    scratch = [pltpu.VMEM((tm, d), BF16)]
    if has_prev:
        p1, p2 = prev
        specs += [pl.BlockSpec((tm, tn), lambda i, j: (i, j)), pl.BlockSpec((tm, tn), lambda i, j: (i, j + nj)),
                  pl.BlockSpec((tm, tn), lambda i, j: (i, j)), pl.BlockSpec((tm, tn), lambda i, j: (i, j + nj))]
        args += [p1, p1, p2, p2]
        out_shape = [jax.ShapeDtypeStruct((m, d), F32), jax.ShapeDtypeStruct((m, f), F32),
                     jax.ShapeDtypeStruct((m, f), F32)]
        out_specs = [pl.BlockSpec((tm, d), lambda i, j: (i, 0)), pl.BlockSpec((tm, tn), lambda i, j: (i, j)),
                     pl.BlockSpec((tm, tn), lambda i, j: (i, j))]
    else:
        out_shape = [jax.ShapeDtypeStruct((m, d), F32), jax.ShapeDtypeStruct((ni, CONV_W - 1, f), F32),
                     jax.ShapeDtypeStruct((ni, CONV_W - 1, f), F32)]
        out_specs = [pl.BlockSpec((tm, d), lambda i, j: (i, 0)),
                     pl.BlockSpec((None, CONV_W - 1, tn), lambda i, j: (i, 0, j)),
                     pl.BlockSpec((None, CONV_W - 1, tn), lambda i, j: (i, 0, j))]
        scratch.append(pltpu.VMEM((nj, 2, 8, tn), F32))
    outs = pl.pallas_call(
        functools.partial(_ffn_kernel, tps=tps, seq_len=seq_len, has_prev=has_prev),
        grid=(ni, nj), in_specs=specs, out_specs=out_specs, out_shape=out_shape,
        scratch_shapes=scratch, compiler_params=_cparams('arbitrary', 'arbitrary'), name='ffn')(*args)
    if has_prev:
        return outs
    return outs[0], outs[1][tps - 1::tps], outs[2][tps - 1::tps]


def _ple_kernel(*refs, final):
    if final:
        h_ref, g_ref, wg_ref, p_ref, wp_ref, gf_ref, o_ref = refs
    else:
        h_ref, g_ref, wg_ref, p_ref, wp_ref, o_ref = refs
    h = h_ref[...]
    gate = _sigmoid(_dot(_rms(h, g_ref[...]), wg_ref[...]))
    y = h + gate * _dot(p_ref[...], wp_ref[...])
    if final:
        y = _rms(y, gf_ref[...])
    o_ref[...] = y


def _ple(h, g, w_gate, p, w_proj, final_g=None, tm=1024):
    m, d = h.shape
    pd = p.shape[1]
    tm = min(tm, m)
    specs = [pl.BlockSpec((tm, d), lambda i: (i, 0)), pl.BlockSpec((1, d), lambda i: (0, 0)),
             pl.BlockSpec((d, d), lambda i: (0, 0)), pl.BlockSpec((tm, pd), lambda i: (i, 0)),
             pl.BlockSpec((pd, d), lambda i: (0, 0))]
    args = [h, g.reshape(1, d).astype(F32), w_gate, p, w_proj]
    if final_g is not None:
        specs.append(pl.BlockSpec((1, d), lambda i: (0, 0)))
        args.append(final_g.reshape(1, d).astype(F32))
    return pl.pallas_call(
        functools.partial(_ple_kernel, final=final_g is not None), grid=(m // tm,),
        in_specs=specs, out_specs=pl.BlockSpec((tm, d), lambda i: (i, 0)),
        out_shape=jax.ShapeDtypeStruct((m, d), F32), compiler_params=_cparams('parallel'),
        name='ple')(*args)


def _softmax_tile(s, mask):
    s = jnp.where(mask, s, NEG)
    e = jnp.where(mask, jnp.exp(s - jnp.max(s, axis=-1, keepdims=True)), 0.0)
    return e / jnp.maximum(jnp.sum(e, axis=-1, keepdims=True), 1e-30)


def _online_init(m_ref, l_ref, acc_ref):
    m_ref[...] = jnp.full(m_ref.shape, NEG, F32)
    l_ref[...] = jnp.zeros(l_ref.shape, F32)
    acc_ref[...] = jnp.zeros(acc_ref.shape, F32)


def _online_update(s, mask, v, m_ref, l_ref, acc_ref, v_t=False):
    if mask is not None:
        s = jnp.where(mask, s, NEG)
    m_old = m_ref[...]
    m_new = jnp.maximum(m_old, jnp.max(s, axis=-1, keepdims=True))
    p = jnp.exp(s - m_new)
    if mask is not None:
        p = jnp.where(mask, p, 0.0)
    alpha = jnp.exp(m_old - m_new)
    l_ref[...] = alpha * l_ref[...] + jnp.sum(p, axis=-1, keepdims=True)
    acc_ref[...] = alpha * acc_ref[...] + (_dot_nt(p, v) if v_t else _dot(p, v))
    m_ref[...] = m_new


def _online_result(l_ref, acc_ref):
    return acc_ref[...] / jnp.maximum(l_ref[...], 1e-30)


def _online_update_t(st, bias, v_t, m_ref, l_ref, acc_ref):
    if bias is not None:
        st = st + bias
    m_old = m_ref[...]
    m_new = jnp.maximum(m_old, jnp.max(st, axis=0, keepdims=True))
    p = jnp.exp(st - m_new)
    alpha = jnp.exp(m_old - m_new)
    l_ref[...] = alpha * l_ref[...] + jnp.sum(p, axis=0, keepdims=True)
    acc_ref[...] = alpha * acc_ref[...] + _dot(v_t, p)
    m_ref[...] = m_new


def _softmax_tile_t(st, mask):
    st = jnp.where(mask, st, NEG)
    e = jnp.where(mask, jnp.exp(st - jnp.max(st, axis=0, keepdims=True)), 0.0)
    return e / jnp.maximum(jnp.sum(e, axis=0, keepdims=True), 1e-30)


def _cumsum(x, axis):
    n = x.shape[axis]
    idx = lax.broadcasted_iota(jnp.int32, x.shape, axis)
    k = 1
    while k < n:
        x = x + jnp.where(idx >= k, pltpu.roll(x, k, axis), 0.0)
        k *= 2
    return x


def _log_sigmoid(x):
    return jnp.minimum(x, 0.0) - jnp.log(1.0 + jnp.exp(-jnp.abs(x)))


def _compress_rows(k_ref, v_ref, n_seg, pe_ref, w1_ref, w2p_ref):
    outs = []
    seg_w = CMP_STRIDE * NSA_DH
    lane = lax.broadcasted_iota(jnp.int32, (n_seg, LANES), 1)
    for kind, rows_ref in enumerate((k_ref, v_ref)):
        xs = [rows_ref[pl.ds(s, n_seg, stride=CMP_STRIDE), :] for s in range(CMP_STRIDE)]
        o = None
        for g in range(NSA_G):
            pieces = []
            for a in range(CMP_STRIDE // 2):
                ev, od = xs[2 * a], xs[2 * a + 1]
                if g == 0:
                    pieces.append(jnp.where(lane < NSA_DH, ev, pltpu.roll(od, NSA_DH, 1)))
                else:
                    pieces.append(jnp.where(lane < NSA_DH, pltpu.roll(ev, NSA_DH, 1), od))
            seg = jnp.concatenate(pieces, axis=1)
            pre = None
            for r in range(CMP_LEN // CMP_STRIDE):
                acc = _dot(seg + pe_ref[kind, r:r + 1, :], w1_ref[kind, r * seg_w:(r + 1) * seg_w, :])
                pre = acc if r == 0 else pre + pltpu.roll(acc, n_seg - r, 0)
            t = _dot(_silu(pre), w2p_ref[kind, g])
            o = t if o is None else o + t
        outs.append(o)
    return outs


def _nsa_cmp_kernel(k_ref, v_ref, pe_ref, w1_ref, w2p_ref, kc_ref, vc_ref, *, n_seg):
    kc, vc = _compress_rows(k_ref, v_ref, n_seg, pe_ref, w1_ref, w2p_ref)
    kc_ref[...] = kc
    vc_ref[...] = vc


def _nsa_compress(rows, pe, w1, w2p):
    b, t, _ = rows.shape
    n_seg = t // CMP_STRIDE
    full = lambda shp: pl.BlockSpec(shp, lambda i: (0,) * len(shp))
    return pl.pallas_call(
        functools.partial(_nsa_cmp_kernel, n_seg=n_seg), grid=(b,),
        in_specs=[pl.BlockSpec((None, t, LANES), lambda i: (i, 0, 0)),
                  pl.BlockSpec((None, t, LANES), lambda i: (i, 0, 1)), full(pe.shape), full(w1.shape),
                  full(w2p.shape)],
        out_specs=[pl.BlockSpec((None, n_seg, LANES), lambda i: (i, 0, 0))] * 2,
        out_shape=[jax.ShapeDtypeStruct((b, n_seg, LANES), F32)] * 2,
        compiler_params=_cparams('parallel'), name='nsa_compress')(rows, rows, pe, w1, w2p)


def _overlap_t(nc, ns, nc_pad, ns_pad):
    cs = np.arange(nc)[None, :] * CMP_STRIDE
    ss = np.arange(ns)[:, None] * SLC_BLOCK
    ov = np.maximum(np.minimum(cs + CMP_LEN, ss + SLC_BLOCK) - np.maximum(cs, ss), 0) / CMP_LEN
    out = np.zeros((ns_pad, nc_pad), np.float32)
    out[:ns, :nc] = ov
    return jnp.asarray(out)


def _select_blocks(sc, valid, blk, n_cand, n_sel):
    rank = jnp.zeros(sc.shape, F32)
    for j in range(n_cand):
        rj = sc[j:j + 1, :]
        beats = (rj > sc) | ((rj == sc) & (blk > j))
        rank = rank + jnp.where(beats, 1.0, 0.0)
    return jnp.where(valid & (rank < n_sel), 1.0, 0.0)


def _nsa_attn_kernel(q_ref, gt_ref, kc_ref, vc_ref, ks_ref, vs_ref, kw_ref, vw_ref, ovt_ref, et_ref,
                     o_ref, ocmp_ref, msk_ref, wmsk_ref, qt_ref, vst_ref, vwt_ref, m_ref, l_ref, acc_ref,
                     *, tq, t_len, nc, n_sel):
    qi = pl.program_id(1)
    q0 = qi * tq
    scale = NSA_DH ** -0.5
    kt = 256
    ns = t_len // SLC_BLOCK
    n_pairs = NSA_HEADS // 2

    @pl.when(qi == 0)
    def _():
        for jt in range(t_len // kt):
            vst_ref[jt] = vs_ref[jt * kt:(jt + 1) * kt, :].T.astype(BF16)
            vwt_ref[jt] = vw_ref[jt * kt:(jt + 1) * kt, :].T.astype(BF16)

    hpg = NSA_HEADS // NSA_G
    pairs_g = n_pairs // NSA_G
    wq = hpg * tq
    zeros_half = jnp.zeros((NSA_DH, tq), F32)
    for g in range(NSA_G):
        pieces = []
        for j in range(pairs_g):
            slab_t = q_ref[g * pairs_g + j].astype(F32).T * scale
            for part in (slab_t[:NSA_DH], slab_t[NSA_DH:]):
                pieces.append(jnp.concatenate([part, zeros_half] if g == 0 else [zeros_half, part], axis=0))
        qt_ref[g] = jnp.concatenate(pieces, axis=1).astype(BF16)

    kc = kc_ref[...]
    vc_t = vc_ref[...].T
    ncp = kc.shape[0]
    cidx = lax.broadcasted_iota(jnp.int32, (ncp, wq), 0)
    qpos_c = q0 + (lax.broadcasted_iota(jnp.int32, (ncp, wq), 1) & (tq - 1))
    cmask = (cidx < nc) & (cidx * CMP_STRIDE + (CMP_LEN - 1) <= qpos_c)
    imp = []
    for g in range(NSA_G):
        p = _softmax_tile_t(_dot(kc, qt_ref[g]), cmask)
        ocmp_ref[g] = _dot(vc_t, p)
        tot = p[:, 0:tq]
        for hh in range(1, hpg):
            tot = tot + p[:, hh * tq:(hh + 1) * tq]
        imp.append(tot)

    nsp = ovt_ref.shape[0]
    blk = lax.broadcasted_iota(jnp.int32, (ns, tq), 0)
    cur = (q0 + lax.broadcasted_iota(jnp.int32, (ns, tq), 1)) // SLC_BLOCK
    forced = (blk == 0) | (blk == cur) | (blk == cur - 1)
    valid = blk <= cur
    krow1 = lax.broadcasted_iota(jnp.int32, (kt, tq), 0)
    qpos1 = q0 + lax.broadcasted_iota(jnp.int32, (kt, tq), 1)
    for g in range(NSA_G):
        sc = _dot_f32(ovt_ref[...], imp[g])[:ns]
        sc = jnp.where(valid, sc + jnp.where(forced, FORCE_BONUS, 0.0), NEG)
        sel_t = _select_blocks(sc, valid, blk, ns, n_sel)
        if nsp > ns:
            sel_t = jnp.concatenate([sel_t, jnp.zeros((nsp - ns, tq), F32)], axis=0)
        for jt in range(t_len // kt):
            hit = _dot(et_ref[jt * kt:(jt + 1) * kt, :], sel_t)
            msk_ref[g, jt] = jnp.where((jt * kt + krow1 <= qpos1) & (hit > 0.5), 0.0, NEG)
    win_tiles = list(range(-WINDOW, tq, kt))[::-1]
    for wi, d in enumerate(win_tiles):
        kpos = q0 + d + krow1
        wmsk_ref[wi] = jnp.where((kpos <= qpos1) & (kpos > qpos1 - WINDOW), 0.0, NEG)

    gt_t = gt_ref[...].T
    for g in range(NSA_G):
        q_g = qt_ref[g]
        _online_init(m_ref, l_ref, acc_ref)

        def slc_body(jt, carry):
            start = pl.multiple_of(jt * kt, kt)
            bias = jnp.concatenate([msk_ref[g, jt]] * hpg, axis=1)
            _online_update_t(_dot(ks_ref[pl.ds(start, kt), :], q_g), bias, vst_ref[jt], m_ref, l_ref, acc_ref)
            return carry

        lax.fori_loop(0, (q0 + tq) // kt, slc_body, 0)
        o_slc = _online_result(l_ref, acc_ref)

        _online_init(m_ref, l_ref, acc_ref)
        for wi, d in enumerate(win_tiles):
            @pl.when(q0 + d >= 0)
            def _():
                start = pl.multiple_of(q0 + d, kt)
                bias = jnp.concatenate([wmsk_ref[wi]] * hpg, axis=1)
                _online_update_t(_dot(kw_ref[pl.ds(start, kt), :], q_g), bias, vwt_ref[(q0 + d) // kt],
                                 m_ref, l_ref, acc_ref)
        o_win = _online_result(l_ref, acc_ref)

        def gate(c):
            row = c * NSA_HEADS + g * hpg
            return jnp.concatenate([gt_t[row + hh:row + hh + 1, :] for hh in range(hpg)], axis=1)

        o = gate(0) * ocmp_ref[g] + gate(1) * o_slc + gate(2) * o_win
        o = o[g * NSA_DH:(g + 1) * NSA_DH]
        for j in range(pairs_g):
            o_ref[g * pairs_g + j] = jnp.concatenate(
                [o[:, 2 * j * tq:(2 * j + 1) * tq], o[:, (2 * j + 1) * tq:(2 * j + 2) * tq]],
                axis=0).T.astype(o_ref.dtype)


def _nsa_attn(q_slab, gates, kc, vc, slc_rows, win_rows, tq=256):
    b, n_pairs, t, _ = q_slab.shape
    tq = min(tq, t)
    n_seg = kc.shape[1]
    nc = n_seg - 1
    ns = t // SLC_BLOCK
    nsp = LANES
    ovt = _overlap_t(nc, ns, n_seg, nsp)
    e = jnp.asarray((np.arange(t)[:, None] // SLC_BLOCK == np.arange(nsp)[None, :]).astype(np.float32), BF16)
    kern = functools.partial(_nsa_attn_kernel, tq=tq, t_len=t, nc=nc, n_sel=min(SLC_TOPN, ns))
    wq = (NSA_HEADS // NSA_G) * tq
    seq = lambda c: pl.BlockSpec((None, t, LANES), lambda i, j: (i, 0, c))
    return pl.pallas_call(
        kern, grid=(b, t // tq),
        in_specs=[pl.BlockSpec((None, n_pairs, tq, LANES), lambda i, j: (i, 0, j, 0)),
                  pl.BlockSpec((None, tq, LANES), lambda i, j: (i, j, 0)),
                  pl.BlockSpec((None, n_seg, LANES), lambda i, j: (i, 0, 0)),
                  pl.BlockSpec((None, n_seg, LANES), lambda i, j: (i, 0, 0)),
                  seq(0), seq(1), seq(0), seq(1),
                  pl.BlockSpec(ovt.shape, lambda i, j: (0, 0)),
                  pl.BlockSpec(e.shape, lambda i, j: (0, 0))],
        out_specs=pl.BlockSpec((None, n_pairs, tq, LANES), lambda i, j: (i, 0, j, 0)),
        out_shape=jax.ShapeDtypeStruct((b, n_pairs, t, LANES), BF16),
        scratch_shapes=[pltpu.VMEM((NSA_G, LANES, wq), F32),
                        pltpu.VMEM((NSA_G, t // 256, 256, tq), F32),
                        pltpu.VMEM((len(range(-WINDOW, tq, 256)), 256, tq), F32),
                        pltpu.VMEM((NSA_G, LANES, wq), BF16),
                        pltpu.VMEM((t // 256, LANES, 256), BF16), pltpu.VMEM((t // 256, LANES, 256), BF16),
                        pltpu.VMEM((1, wq), F32), pltpu.VMEM((1, wq), F32), pltpu.VMEM((LANES, wq), F32)],
        compiler_params=_cparams('arbitrary', 'arbitrary'), name='nsa_attn')(
            q_slab, gates, kc, vc, slc_rows, slc_rows, win_rows, win_rows, ovt, e)


def _mlstm_kernel(*refs, L, t_real, has_state, pad):
    it = iter(refs)
    z_ref, gc_ref, gr_ref, bc_ref, br_ref = next(it), next(it), next(it), next(it), next(it)
    if has_state:
        c0_ref, n0_ref, m0_ref = next(it), next(it), next(it)
    h_ref, c_out, n_out, m_out = next(it), next(it), next(it), next(it)
    c_scr, n_scr, m_scr = next(it), next(it), next(it)
    if pad:
        zp_ref, gcp_ref, grp_ref = next(it), next(it), next(it)
    ci = pl.program_id(1)
    nchunks = pl.num_programs(1)

    @pl.when(ci == 0)
    def _():
        if has_state:
            c_scr[...] = c0_ref[...]
            n_scr[...] = n0_ref[...]
            m_scr[...] = m0_ref[...]
        else:
            c_scr[...] = jnp.zeros(c_scr.shape, F32)
            n_scr[...] = jnp.zeros(n_scr.shape, F32)
            m_scr[...] = jnp.zeros(m_scr.shape, F32)

    if pad:
        @pl.when((pl.program_id(0) == 0) & (ci == 0))
        def _():
            zp_ref[...] = jnp.zeros(zp_ref.shape, F32)
            gcp_ref[...] = jnp.zeros(gcp_ref.shape, F32)
            grp_ref[...] = jnp.zeros(grp_ref.shape, F32)

        zp_ref[:, 0:t_real, :] = z_ref[...]
        gcp_ref[0:t_real, :] = gc_ref[...]
        grp_ref[:, 0:t_real] = gr_ref[...]
        z_ref, gc_ref, gr_ref = zp_ref, gcp_ref, grp_ref

    H = ML_HEADS
    gcol = gc_ref[...] + bc_ref[...]
    grow = gr_ref[...] + br_ref[...]
    lf_c = _log_sigmoid(gcol)
    lf_r = _log_sigmoid(grow)
    ig_c, ig_r = gcol, grow
    if t_real < L:
        rv = lax.broadcasted_iota(jnp.int32, gcol.shape, 0) < t_real
        lv = lax.broadcasted_iota(jnp.int32, grow.shape, 1) < t_real
        lf_c, ig_c = jnp.where(rv, lf_c, 0.0), jnp.where(rv, ig_c, NEG)
        lf_r, ig_r = jnp.where(lv, lf_r, 0.0), jnp.where(lv, ig_r, NEG)
    bcum_c = _cumsum(lf_c, 0)
    bcum_r = _cumsum(lf_r, 1)
    tri = lax.broadcasted_iota(jnp.int32, (L, L), 0) >= lax.broadcasted_iota(jnp.int32, (L, L), 1)
    for h in range(H):
        q = z_ref[h]
        k = z_ref[H + h] * (ML_DK ** -0.5)
        v = jnp.concatenate([z_ref[2 * H + 2 * h], z_ref[2 * H + 2 * h + 1]], axis=1)
        og = jnp.concatenate([z_ref[4 * H + 2 * h], z_ref[4 * H + 2 * h + 1]], axis=1)
        bc_t = bcum_c[:, H + h:H + h + 1]
        bc_s = bcum_r[H + h:H + h + 1, :]
        ig_s = ig_r[h:h + 1, :]
        ig_t = ig_c[:, h:h + 1]
        m_old = m_scr[h:h + 1, 0:1]
        dmat = jnp.where(tri, bc_t - bc_s + ig_s, NEG)
        inter = bc_t + m_old
        mt = jnp.maximum(jnp.max(dmat, axis=1, keepdims=True), inter)
        w_intra = jnp.exp(dmat - mt)
        w_state = jnp.exp(inter - mt)
        sc = _dot_nt(q, k) * w_intra
        c_st = c_scr[h]
        n_st = n_scr[h:h + 1, :]
        num = _dot(sc, v) + w_state * _dot_nt(q, c_st)
        den = jnp.sum(sc, axis=1, keepdims=True) + w_state * jnp.sum(q * n_st, axis=1, keepdims=True)
        hh = num / jnp.maximum(jnp.abs(den), jnp.exp(-mt)) * og
        h_ref[2 * h] = hh[0:h_ref.shape[1], 0:LANES].astype(h_ref.dtype)
        h_ref[2 * h + 1] = hh[0:h_ref.shape[1], LANES:].astype(h_ref.dtype)
        bl = bc_s[:, L - 1:L]
        dl_r = bl - bc_s + ig_s
        dl_t = bl - bc_t + ig_t
        m_new = jnp.maximum(bl + m_old, jnp.max(dl_r, axis=1, keepdims=True))
        ws_t = jnp.exp(dl_t - m_new)
        wc = jnp.exp(bl + m_old - m_new)
        c_scr[h] = wc * c_st + _dot((v * ws_t).T, k)
        n_scr[h:h + 1, :] = wc * n_st + jnp.sum(ws_t * k, axis=0, keepdims=True)
        m_scr[h:h + 1, :] = jnp.broadcast_to(m_new, (1, LANES))

    @pl.when(ci == nchunks - 1)
    def _():
        c_out[...] = c_scr[...]
        n_out[...] = n_scr[...]
        m_out[...] = m_scr[...]


def _mlstm(z_slab, gates_col, gates_row, gate_b, state=None, L=256):
    b, ns, t, _ = z_slab.shape
    pad = t < 8
    L = 128 if pad else min(L, t)
    lr = t if pad else L
    nch = 1 if pad else t // L
    H = ML_HEADS
    bcol = jnp.zeros((1, LANES), F32).at[0, :2 * H].set(gate_b.reshape(-1).astype(F32))
    brow = gate_b.reshape(2 * H, 1).astype(F32)
    specs = [pl.BlockSpec((None, ns, lr, LANES), lambda i, c: (i, 0, c, 0)),
             pl.BlockSpec((None, lr, LANES), lambda i, c: (i, c, 0)),
             pl.BlockSpec((None, 2 * H, lr), lambda i, c: (i, 0, c)),
             pl.BlockSpec((1, LANES), lambda i, c: (0, 0)),
             pl.BlockSpec((2 * H, 1), lambda i, c: (0, 0))]
    args = [z_slab, gates_col, gates_row, bcol, brow]
    if state is not None:
        c0, n0, m0 = state
        n0p = jnp.zeros((b, 8, LANES), F32).at[:, :H].set(n0.astype(F32))
        m0p = jnp.zeros((b, 8, LANES), F32).at[:, :H].set(jnp.broadcast_to(m0.astype(F32)[..., None], (b, H, LANES)))
        specs += [pl.BlockSpec((None, H, ML_DV, ML_DK), lambda i, c: (i, 0, 0, 0)),
                  pl.BlockSpec((None, 8, LANES), lambda i, c: (i, 0, 0)),
                  pl.BlockSpec((None, 8, LANES), lambda i, c: (i, 0, 0))]
        args += [c0.astype(F32), n0p, m0p]
    scratch = [pltpu.VMEM((H, ML_DV, ML_DK), F32), pltpu.VMEM((8, LANES), F32), pltpu.VMEM((8, LANES), F32)]
    if pad:
        scratch += [pltpu.VMEM((ns, L, LANES), F32), pltpu.VMEM((L, LANES), F32), pltpu.VMEM((2 * H, L), F32)]
    kern = functools.partial(_mlstm_kernel, L=L, t_real=t if pad else L, has_state=state is not None, pad=pad)
    h_slab, c_f, n_f, m_f = pl.pallas_call(
        kern, grid=(b, nch), in_specs=specs,
        out_specs=[pl.BlockSpec((None, 2 * H, lr, LANES), lambda i, c: (i, 0, c, 0)),
                   pl.BlockSpec((None, H, ML_DV, ML_DK), lambda i, c: (i, 0, 0, 0)),
                   pl.BlockSpec((None, 8, LANES), lambda i, c: (i, 0, 0)),
                   pl.BlockSpec((None, 8, LANES), lambda i, c: (i, 0, 0))],
        out_shape=[jax.ShapeDtypeStruct((b, 2 * H, t, LANES), BF16),
                   jax.ShapeDtypeStruct((b, H, ML_DV, ML_DK), F32),
                   jax.ShapeDtypeStruct((b, 8, LANES), F32), jax.ShapeDtypeStruct((b, 8, LANES), F32)],
        scratch_shapes=scratch, compiler_params=_cparams('arbitrary', 'arbitrary'), name='mlstm')(*args)
    return h_slab, c_f, n_f[:, :H], m_f[:, :H, 0]


def _hgrn_levels(L):
    n_lev = int(math.log2(L))
    t = np.arange(L)
    pall = np.zeros((n_lev * L, L), np.float32)
    lmask = np.zeros((n_lev, L, L), np.float32)
    for lev in range(n_lev):
        w = L >> lev
        mid = (t // w) * w + w // 2
        pall[lev * L + t, mid - 1] = 1.0
        same = (t[:, None] // w) == (t[None, :] // w)
        lmask[lev] = same & ((t[:, None] % w) >= w // 2) & ((t[None, :] % w) < w // 2)
    return jnp.asarray(pall, BF16), jnp.asarray(lmask)


def _hgrn_kernel(*refs, L, t_real, has_state, pad):
    it = iter(refs)
    z_ref, lf_ref, gn_ref, pall_ref, lmask_ref = next(it), next(it), next(it), next(it), next(it)
    if has_state:
        s0_ref = next(it)
    o_ref, s_out = next(it), next(it)
    st_scr = next(it)
    if pad:
        zp_ref = next(it)
    ci = pl.program_id(1)
    nchunks = pl.num_programs(1)
    H = HG_HEADS
    n_lev = lmask_ref.shape[0]

    @pl.when(ci == 0)
    def _():
        for h in range(H):
            st_scr[h] = s0_ref[h].T if has_state else jnp.zeros((HG_DV, HG_DK), F32)

    if pad:
        @pl.when((pl.program_id(0) == 0) & (ci == 0))
        def _():
            zp_ref[...] = jnp.zeros(zp_ref.shape, F32)

        zp_ref[:, 0:t_real, :] = z_ref[...]
        z_ref = zp_ref

    rows = lax.broadcasted_iota(jnp.int32, (L, LANES), 0)
    eye = lax.broadcasted_iota(jnp.int32, (L, L), 0) == lax.broadcasted_iota(jnp.int32, (L, L), 1)
    gn = gn_ref[...]

    def head(h, carry):
        q = z_ref[h]
        zf = z_ref[H + h]
        v = z_ref[2 * H + h]
        gate = z_ref[3 * H + h]
        lower = lf_ref[pl.ds(h, 1), :]
        fb = lf_ref[pl.ds(H + h, 1), :]
        f = lower + (1.0 - lower) * _sigmoid(zf + fb)
        lf = jnp.log(f)
        k = 1.0 - f
        if t_real < L:
            lf = jnp.where(rows < t_real, lf, 0.0)
            k = jnp.where(rows < t_real, k, 0.0)
        bcum = _cumsum(lf, 0)
        lev0 = n_lev - max(1, (t_real - 1).bit_length())
        hi = bcum.astype(BF16)
        rem = bcum - hi.astype(F32)
        mid = rem.astype(BF16)
        low = (rem - mid.astype(F32)).astype(BF16)
        picked = jnp.dot(pall_ref[lev0 * L:, :], jnp.concatenate([hi, mid, low], axis=1),
                         preferred_element_type=F32)
        refs_all = picked[:, :HG_DK] + picked[:, HG_DK:2 * HG_DK] + picked[:, 2 * HG_DK:]
        att = jnp.where(eye, jnp.sum(q * k, axis=1, keepdims=True), 0.0)
        for lev in range(lev0, n_lev):
            r = refs_all[(lev - lev0) * L:(lev - lev0 + 1) * L]
            qt = q * jnp.exp(jnp.minimum(bcum - r, 0.0))
            kt = k * jnp.exp(jnp.minimum(r - bcum, 0.0))
            att = att + _dot_nt(qt, kt) * lmask_ref[lev]
        s_t = st_scr[h]
        o = _dot(att, v) + _dot_nt(q * jnp.exp(bcum), s_t)
        bl = bcum[L - 1:L, :]
        st_scr[h] = jnp.exp(bl) * s_t + _dot(v.T, k * jnp.exp(bl - bcum))
        on = _rms(o, gn) * _silu(gate)
        o_ref[h] = on[0:o_ref.shape[1]].astype(o_ref.dtype)
        return carry

    lax.fori_loop(0, H, head, 0, unroll=4)

    @pl.when(ci == nchunks - 1)
    def _():
        for h in range(H):
            s_out[h] = st_scr[h].T


def _hgrn(z_slab, lower, f_b, g_norm, state=None, L=128):
    b, ns, t, _ = z_slab.shape
    pad = t < 8
    L = L if pad else min(L, t)
    lr = t if pad else L
    nch = 1 if pad else t // L
    H = HG_HEADS
    pall, lmask = _hgrn_levels(L)
    lowfb = jnp.concatenate([lower.reshape(H, HG_DK), f_b.reshape(H, HG_DK)], axis=0).astype(F32)
    specs = [pl.BlockSpec((None, ns, lr, LANES), lambda i, c: (i, 0, c, 0)),
             pl.BlockSpec((2 * H, LANES), lambda i, c: (0, 0)),
             pl.BlockSpec((1, LANES), lambda i, c: (0, 0)),
             pl.BlockSpec(pall.shape, lambda i, c: (0, 0)),
             pl.BlockSpec(lmask.shape, lambda i, c: (0, 0, 0))]
    args = [z_slab, lowfb, g_norm.reshape(1, HG_DV).astype(F32), pall, lmask]
    if state is not None:
        specs.append(pl.BlockSpec((None, H, HG_DK, HG_DV), lambda i, c: (i, 0, 0, 0)))
        args.append(state.astype(F32))
    scratch = [pltpu.VMEM((H, HG_DV, HG_DK), F32)]
    if pad:
        scratch.append(pltpu.VMEM((ns, L, LANES), F32))
    kern = functools.partial(_hgrn_kernel, L=L, t_real=t if pad else L, has_state=state is not None, pad=pad)
    return pl.pallas_call(
        kern, grid=(b, nch), in_specs=specs,
        out_specs=[pl.BlockSpec((None, H, lr, LANES), lambda i, c: (i, 0, c, 0)),
                   pl.BlockSpec((None, H, HG_DK, HG_DV), lambda i, c: (i, 0, 0, 0))],
        out_shape=[jax.ShapeDtypeStruct((b, H, t, LANES), BF16),
                   jax.ShapeDtypeStruct((b, H, HG_DK, HG_DV), F32)],
        scratch_shapes=scratch, compiler_params=_cparams('arbitrary', 'arbitrary'), name='hgrn')(*args)


def _mla_attn_kernel(ql_ref, qr_ref, lc_ref, lr_ref, o_ref, lct_ref, qlt_ref, qrt_ref, m_ref, l_ref, acc_ref,
                     *, tq, t_len):
    qi = pl.program_id(1)
    q0 = qi * tq
    kt = 256
    scale = (MLA_NOPE + MLA_ROPE) ** -0.5
    mq = MLA_HEADS * tq
    per_slab = LANES // MLA_ROPE

    @pl.when(qi == 0)
    def _():
        for jt in range(t_len // kt):
            lct_ref[jt] = lc_ref[jt * kt:(jt + 1) * kt, :].T.astype(BF16)

    zeros_r = jnp.zeros((LANES - MLA_ROPE, tq), F32)
    for h in range(MLA_HEADS):
        qlt_ref[:, h * tq:(h + 1) * tq] = ql_ref[:, h * MLA_KV_LORA:(h + 1) * MLA_KV_LORA].astype(F32).T.astype(BF16)
        if h % per_slab == 0:
            slab_t = qr_ref[:, (h // per_slab) * LANES:(h // per_slab + 1) * LANES].astype(F32).T
        rope_t = slab_t[(h % per_slab) * MLA_ROPE:(h % per_slab + 1) * MLA_ROPE]
        qrt_ref[:, h * tq:(h + 1) * tq] = jnp.concatenate([rope_t, zeros_r], axis=0).astype(BF16)
    q_lat = qlt_ref[...]
    q_rope = qrt_ref[...]
    _online_init(m_ref, l_ref, acc_ref)

    def tile(j, masked):
        start = pl.multiple_of(j * kt, kt)
        st = (_dot(lc_ref[pl.ds(start, kt), :], q_lat) + _dot(lr_ref[pl.ds(start, kt), :], q_rope)) * scale
        if masked:
            krow = lax.broadcasted_iota(jnp.int32, (kt, mq), 0)
            qpos = q0 + (lax.broadcasted_iota(jnp.int32, (kt, mq), 1) & (tq - 1))
            bias = jnp.where(start + krow <= qpos, 0.0, NEG)
        else:
            bias = None
        _online_update_t(st, bias, lct_ref[j], m_ref, l_ref, acc_ref)

    def body(j, carry):
        tile(j, False)
        return carry

    n_full = q0 // kt
    lax.fori_loop(0, n_full, body, 0)
    tile(n_full, True)
    o = _online_result(l_ref, acc_ref)
    for h in range(MLA_HEADS):
        o_ref[:, h * MLA_KV_LORA:(h + 1) * MLA_KV_LORA] = o[:, h * tq:(h + 1) * tq].T.astype(o_ref.dtype)


def _mla_attn(q_lat, q_rope, lat, b, t, tq=128):
    tq = min(tq, t)
    wl = MLA_HEADS * MLA_KV_LORA
    wr = MLA_HEADS * MLA_ROPE
    lat3 = lat.reshape(b, t, lat.shape[-1])
    c_blk = MLA_Q_LORA // MLA_KV_LORA
    r_blk = (MLA_Q_LORA + MLA_KV_LORA) // LANES
    return pl.pallas_call(
        functools.partial(_mla_attn_kernel, tq=tq, t_len=t), grid=(b, t // tq),
        in_specs=[pl.BlockSpec((None, tq, wl), lambda i, j: (i, j, 0)),
                  pl.BlockSpec((None, tq, wr), lambda i, j: (i, j, 0)),
                  pl.BlockSpec((None, t, MLA_KV_LORA), lambda i, j: (i, 0, c_blk)),
                  pl.BlockSpec((None, t, LANES), lambda i, j: (i, 0, r_blk))],
        out_specs=pl.BlockSpec((None, tq, wl), lambda i, j: (i, j, 0)),
        out_shape=jax.ShapeDtypeStruct((b, t, wl), BF16),
        scratch_shapes=[pltpu.VMEM((t // 256, MLA_KV_LORA, 256), BF16),
                        pltpu.VMEM((MLA_KV_LORA, MLA_HEADS * tq), BF16), pltpu.VMEM((LANES, MLA_HEADS * tq), BF16),
                        pltpu.VMEM((1, MLA_HEADS * tq), F32), pltpu.VMEM((1, MLA_HEADS * tq), F32),
                        pltpu.VMEM((MLA_KV_LORA, MLA_HEADS * tq), F32)],
        compiler_params=_cparams('arbitrary', 'arbitrary'), name='mla_attn')(
            q_lat.reshape(b, t, wl), q_rope.reshape(b, t, wr), lat3, lat3).reshape(b * t, wl)


def _pad_cols(w, n):
    return jnp.pad(w, ((0, 0), (0, n - w.shape[1])))


def _prep_nsa(w_in, pe, w1, w2, w_out):
    nq = NSA_HEADS * NSA_DH
    kvw = 2 * NSA_G * NSA_DH
    w2 = w2.astype(BF16)
    z = jnp.zeros_like(w2)
    w2p = jnp.stack([jnp.concatenate([w2, z], axis=-1), jnp.concatenate([z, w2], axis=-1)], axis=1)
    return dict(q=w_in[:, :nq].astype(BF16), kv=w_in[:, nq:nq + 3 * kvw].astype(BF16),
                g=_pad_cols(w_in[:, nq + 3 * kvw:], LANES).astype(BF16),
                pe=pe.astype(F32).reshape(2, CMP_LEN // CMP_STRIDE, CMP_STRIDE * NSA_DH),
                w1=w1.astype(BF16), w2p=w2p, out=w_out.astype(BF16))


def _prep_ml(w_in, gate_b, w_out):
    a = 2 * ML_HEADS * ML_DK + ML_HEADS * ML_DV
    main = jnp.concatenate([w_in[:, :a], w_in[:, a + 2 * ML_HEADS:]], axis=1)
    return dict(main=main.astype(BF16), gate=_pad_cols(w_in[:, a:a + 2 * ML_HEADS], LANES).astype(BF16),
                gate_b=gate_b, out=w_out.astype(BF16))


def _prep_mla(w_in, q_norm, kv_norm, w_uq, w_uk, w_uv, w_out):
    uq = w_uq.reshape(MLA_Q_LORA, MLA_HEADS, MLA_NOPE + MLA_ROPE)
    return dict(w_in=_pad_cols(w_in, 7 * LANES).astype(BF16),
                q_norm=q_norm.reshape(1, -1).astype(F32), kv_norm=kv_norm.reshape(1, -1).astype(F32),
                uq_nope=uq[:, :, :MLA_NOPE].reshape(MLA_Q_LORA, -1).astype(BF16),
                uq_rope=uq[:, :, MLA_NOPE:].reshape(MLA_Q_LORA, -1).astype(BF16),
                ukt=jnp.transpose(w_uk, (1, 2, 0)).astype(BF16),
                uv=jnp.transpose(w_uv, (1, 0, 2)).astype(BF16), out=w_out.astype(BF16))


def _sig_tile2(z, j):
    return jnp.where(j == 2, _sigmoid(z), z)


def _rope64_epi(z, j, c, s):
    return _rope_tile(z, c, s, NSA_DH // 2)


def _rope_kv_epi(z, j, c, s):
    return jnp.concatenate([_rope_tile(z[:, :LANES], c, s, NSA_DH // 2), z[:, LANES:]], axis=1)


def _rope32_epi(z, j, c, s):
    return _rope_tile(z, c, s, MLA_ROPE // 2)


def _sigmoid_epi(z, j):
    return _sigmoid(z)


def _mla_in_epi(z, j, qn, kvn, c, s):
    a, bnd = MLA_Q_LORA, MLA_Q_LORA + MLA_KV_LORA
    return jnp.concatenate([_rms(z[:, :a], qn), _rms(z[:, a:bnd], kvn),
                            _rope_tile(z[:, bnd:], c, s, MLA_ROPE // 2)], axis=1)


def _const_aux(a):
    return (a, a.shape, lambda i, j: (0,) * a.ndim)


def _nsa_fresh(h, g, w, b, t, tabs):
    cos, sin = tabs[NSA_DH // 2]
    tm = min(1024, t)
    rope = _rope_aux(cos, sin, tm, t // tm)
    q = _mm(h, w['q'], g=g, epi=_rope64_epi, aux=rope, out_dtype=BF16, layout='slab', seq=t)
    kv = _mm(h, w['kv'], g=g, epi=_rope_kv_epi, aux=rope, layout='tiles', seq=t, tn=256)
    gates = _mm(h, w['g'], g=g, epi=_sigmoid_epi, seq=t)
    rows = [kv[i].reshape(b, t, 256) for i in range(3)]
    kc, vc = _nsa_compress(rows[0], w['pe'], w['w1'], w['w2p'])
    o = _nsa_attn(q, gates.reshape(b, t, LANES), kc, vc, rows[1], rows[2])
    h = _mm(o, w['out'], res=h, x_slab_seq=t)
    return h, rows


def _ml_fresh(h, g, w, b, t):
    z = _mm(h, w['main'], g=g, epi=_sig_tile2, layout='slab', seq=t)
    gc = _mm(h, w['gate'], g=g, seq=t).reshape(b, t, LANES)
    gr = jnp.swapaxes(gc[:, :, :2 * ML_HEADS], 1, 2)
    hs, c_f, n_f, m_f = _mlstm(z, gc, gr, w['gate_b'])
    return _mm(hs, w['out'], res=h, x_slab_seq=t), (c_f, n_f, m_f)


def _mla_fresh(h, g, w, b, t, tabs):
    cos, sin = tabs[MLA_ROPE // 2]
    tm = min(1024, t)
    rope = _rope_aux(cos, sin, tm, t // tm)
    lat = _mm(h, w['w_in'], g=g, epi=_mla_in_epi,
              aux=[_const_aux(w['q_norm']), _const_aux(w['kv_norm'])] + rope, seq=t)
    qn = _mm(lat, w['uq_nope'], x_cols=(MLA_Q_LORA, 0), out_dtype=BF16, seq=t)
    qr = _mm(lat, w['uq_rope'], x_cols=(MLA_Q_LORA, 0), epi=_rope32_epi, aux=rope, out_dtype=BF16, seq=t)
    ql = _headmm(qn, w['ukt'])
    ol = _mla_attn(ql, qr, lat, b, t)
    o = _headmm(ol, w['uv'])
    h = _mm(o, w['out'], res=h)
    new_lat = lat[:, MLA_Q_LORA:MLA_Q_LORA + MLA_KV_LORA + MLA_ROPE].reshape(b, t, -1)
    return h, new_lat


def _hg_fresh(h, g, w_in, lower, f_b, g_norm, w_out, b, t):
    z = _mm(h, w_in, g=g, layout='slab', seq=t)
    os_, s_f = _hgrn(z, lower, f_b, g_norm)
    return _mm(os_, w_out, res=h, x_slab_seq=t), s_f


PAGES_PER_STEP = 16


def _page_specs(width, n):
    return [pl.BlockSpec((None, width, PAGE), functools.partial(
        lambda i, s, pt, k: (pt[i, s * n + k], 0, 0), k=k)) for k in range(n)]


def _pages_feature_major(pool):
    nd = pool.ndim
    return jnp.transpose(pool, (0,) + tuple(range(2, nd)) + (1,)).reshape(pool.shape[0], -1, pool.shape[1])


def _stack8(x):
    return jnp.concatenate([x] * (NSA_HEADS // NSA_G), axis=0)


def _nsa_cmp_past_kernel(pt_ref, *refs, n_pg, t, start, nc):
    pages = refs[:n_pg]
    q_ref, pe_ref, w1_ref, w2p_ref, ocmp_ref, imp_ref, k_scr, v_scr = refs[n_pg:]
    s_i = pl.program_id(1)
    for k in range(n_pg):
        off = pl.multiple_of((s_i * n_pg + k) * PAGE, PAGE)
        rows = pages[k][...].T
        k_scr[pl.ds(off, PAGE), :] = rows[:, 0:LANES]
        v_scr[pl.ds(off, PAGE), :] = rows[:, LANES:2 * LANES]

    @pl.when(s_i == pl.num_programs(1) - 1)
    def _():
        n_seg = k_scr.shape[0] // CMP_STRIDE
        kc, vc = _compress_rows(k_scr, v_scr, n_seg, pe_ref, w1_ref, w2p_ref)
        q = q_ref[...]
        r_n = q.shape[0]
        cidx = lax.broadcasted_iota(jnp.int32, (r_n, n_seg), 1)
        qpos = start + lax.rem(lax.broadcasted_iota(jnp.int32, (r_n, n_seg), 0), t)
        mask = (cidx < nc) & (cidx * CMP_STRIDE + (CMP_LEN - 1) <= qpos)
        p = _softmax_tile(_dot_nt(q, kc) * (NSA_DH ** -0.5), mask)
        ocmp_ref[...] = _dot(p, vc)
        gt = NSA_G * t
        imp = p[0:gt]
        for hh in range(1, r_n // gt):
            imp = imp + p[hh * gt:(hh + 1) * gt]
        imp_ref[...] = imp


def _nsa_cmp_past(page_table, cache, q_ext, w, t, start):
    b, npg = page_table.shape
    n_pg = min(PAGES_PER_STEP, npg)
    r_n = q_ext.shape[1]
    n_seg = npg * PAGE // CMP_STRIDE
    nc = (start + t) // CMP_STRIDE - CMP_LEN // CMP_STRIDE + 1
    full = lambda a: pl.BlockSpec(a.shape, lambda i, s, pt: (0,) * a.ndim)
    gs = pltpu.PrefetchScalarGridSpec(
        num_scalar_prefetch=1, grid=(b, npg // n_pg),
        in_specs=_page_specs(256, n_pg) + [pl.BlockSpec((None, r_n, LANES), lambda i, s, pt: (i, 0, 0)),
                                           full(w['pe']), full(w['w1']), full(w['w2p'])],
        out_specs=[pl.BlockSpec((None, r_n, LANES), lambda i, s, pt: (i, 0, 0)),
                   pl.BlockSpec((None, NSA_G * t, n_seg), lambda i, s, pt: (i, 0, 0))],
        scratch_shapes=[pltpu.VMEM((npg * PAGE, LANES), F32), pltpu.VMEM((npg * PAGE, LANES), F32)])
    return pl.pallas_call(
        functools.partial(_nsa_cmp_past_kernel, n_pg=n_pg, t=t, start=start, nc=nc), grid_spec=gs,
        out_shape=[jax.ShapeDtypeStruct((b, r_n, LANES), F32),
                   jax.ShapeDtypeStruct((b, NSA_G * t, n_seg), F32)],
        compiler_params=_cparams('arbitrary', 'arbitrary'), name='nsa_cmp_past')(
            page_table, *([cache] * n_pg), q_ext, w['pe'], w['w1'], w['w2p']), nc


def _nsa_select_kernel(imp_ref, ovt_ref, pos_ref, sel_ref, sc_scr, rank_scr, *, ns, n_sel):
    shape = sc_scr.shape
    blk = lax.broadcasted_iota(jnp.int32, shape, 0)
    cur = pos_ref[...] // SLC_BLOCK
    forced = (blk == 0) | (blk == cur) | (blk == cur - 1)
    valid = blk <= cur
    sc = _dot_nt_f32(ovt_ref[...], imp_ref[...])
    sc_scr[...] = jnp.where(valid, sc + jnp.where(forced, FORCE_BONUS, 0.0), NEG)
    rank_scr[...] = jnp.zeros(shape, F32)

    def body(j, carry):
        sc_all = sc_scr[...]
        rj = sc_scr[pl.ds(j, 1), :]
        beats = (rj > sc_all) | ((rj == sc_all) & (blk > j))
        rank_scr[...] += jnp.where(beats, 1.0, 0.0)
        return carry

    lax.fori_loop(0, ns, body, 0)
    sel_ref[...] = jnp.where(valid & (rank_scr[...] < n_sel), 1.0, 0.0)


def _nsa_select_past(imp, nc, t, start):
    rows, n_seg = imp.shape
    ns = -(-(start + t) // SLC_BLOCK)
    nsp = -(-ns // 8) * 8
    ovt = _overlap_t(nc, ns, n_seg, nsp)
    pos = (start + jnp.arange(rows, dtype=jnp.int32) % t).reshape(1, rows)
    return pl.pallas_call(
        functools.partial(_nsa_select_kernel, ns=ns, n_sel=min(SLC_TOPN, ns)),
        out_shape=jax.ShapeDtypeStruct((nsp, rows), F32),
        scratch_shapes=[pltpu.VMEM((nsp, rows), F32), pltpu.VMEM((nsp, rows), F32)],
        compiler_params=pltpu.CompilerParams(vmem_limit_bytes=VMEM_LIMIT), name='nsa_select')(imp, ovt, pos)


def _nsa_slcwin_past_kernel(pt_ref, *refs, n_pg, t, wl):
    pages = refs[:n_pg]
    (msk_ref, q_ref, ns_ref, nw_ref, wb_ref, nf_ref, gt_ref, ocmp_ref, o_ref,
     m_ref, l_ref, acc_ref, oslc_scr, pad_scr) = refs[n_pg:]
    s_i = pl.program_id(1)
    scale = NSA_DH ** -0.5
    q = q_ref[...]
    r_n = q.shape[0]

    @pl.when(s_i == 0)
    def _():
        _online_init(m_ref, l_ref, acc_ref)

    s = jnp.concatenate([_dot(q, pg[0:LANES, :]) * scale for pg in pages], axis=1)
    v_t = jnp.concatenate([pg[LANES:2 * LANES, :] for pg in pages], axis=1)
    mask = _stack8(msk_ref[...].astype(F32)) > 0.5
    _online_update(s, mask, v_t, m_ref, l_ref, acc_ref, v_t=True)

    @pl.when(s_i == pl.num_programs(1) - 1)
    def _():
        col = lax.broadcasted_iota(jnp.int32, (r_n, PAGE), 1)
        tq = lax.rem(lax.broadcasted_iota(jnp.int32, (r_n, PAGE), 0), t)
        new_ok = (col < t) & (col <= tq)

        def padded(ref):
            pad_scr[...] = jnp.zeros(pad_scr.shape, F32)
            pad_scr[0:t, :] = ref[...]
            return pad_scr[...]

        rows = padded(ns_ref)
        _online_update(_dot_nt(q, rows[:, 0:LANES]) * scale, new_ok & (_stack8(nf_ref[...]) > 0.5),
                       rows[:, LANES:], m_ref, l_ref, acc_ref)
        oslc_scr[...] = _online_result(l_ref, acc_ref)

        _online_init(m_ref, l_ref, acc_ref)
        colw = lax.broadcasted_iota(jnp.int32, (r_n, wl), 1)
        tqw = lax.rem(lax.broadcasted_iota(jnp.int32, (r_n, wl), 0), t)
        _online_update(_dot(q, wb_ref[0:LANES, :]) * scale, colw > tqw + (wl - WINDOW),
                       wb_ref[LANES:2 * LANES, :], m_ref, l_ref, acc_ref, v_t=True)
        rows = padded(nw_ref)
        _online_update(_dot_nt(q, rows[:, 0:LANES]) * scale, new_ok, rows[:, LANES:], m_ref, l_ref, acc_ref)
        o_win = _online_result(l_ref, acc_ref)
        gt = gt_ref[...]
        o_ref[...] = gt[:, 0:1] * ocmp_ref[...] + gt[:, 1:2] * oslc_scr[...] + gt[:, 2:3] * o_win


def _nsa_slcwin_past(page_table, cache, key_mask, q_ext, new_slc, new_win, win_buf, new_flag, gates_r, ocmp, t):
    b, npg = page_table.shape
    n_pg = min(PAGES_PER_STEP, npg)
    r_n = q_ext.shape[1]
    gtn = NSA_G * t
    wl = win_buf.shape[2]
    per_b = lambda shp: pl.BlockSpec((None,) + shp, lambda i, s, pt: (i,) + (0,) * len(shp))
    gs = pltpu.PrefetchScalarGridSpec(
        num_scalar_prefetch=1, grid=(b, npg // n_pg),
        in_specs=_page_specs(256, n_pg) + [
            pl.BlockSpec((None, gtn, n_pg * PAGE), lambda i, s, pt: (i, 0, s)),
            per_b((r_n, LANES)), per_b((t, 256)), per_b((t, 256)), per_b((256, wl)),
            per_b((gtn, LANES)), per_b((r_n, LANES)), per_b((r_n, LANES))],
        out_specs=per_b((r_n, LANES)),
        scratch_shapes=[pltpu.VMEM((r_n, 1), F32), pltpu.VMEM((r_n, 1), F32), pltpu.VMEM((r_n, LANES), F32),
                        pltpu.VMEM((r_n, LANES), F32), pltpu.VMEM((PAGE, 256), F32)])
    return pl.pallas_call(
        functools.partial(_nsa_slcwin_past_kernel, n_pg=n_pg, t=t, wl=wl), grid_spec=gs,
        out_shape=jax.ShapeDtypeStruct((b, r_n, LANES), F32),
        compiler_params=_cparams('arbitrary', 'arbitrary'), name='nsa_slcwin_past')(
            page_table, *([cache] * n_pg), key_mask, q_ext, new_slc, new_win, win_buf, new_flag, gates_r, ocmp)


def _mla_past_kernel(pt_ref, *refs, n_pg, t):
    pages = refs[:n_pg]
    ql_ref, qr_ref, new_ref, o_ref, m_ref, l_ref, acc_ref, pad_scr = refs[n_pg:]
    s_i = pl.program_id(1)
    scale = (MLA_NOPE + MLA_ROPE) ** -0.5
    ql = ql_ref[...]
    qr = qr_ref[...]
    r_n = ql.shape[0]

    @pl.when(s_i == 0)
    def _():
        _online_init(m_ref, l_ref, acc_ref)

    def scores(rows):
        kc = rows[:, 0:MLA_KV_LORA].astype(BF16)
        kr = rows[:, MLA_KV_LORA:MLA_KV_LORA + MLA_ROPE]
        return (_dot_nt(ql, kc) + _dot_nt(qr, kr)) * scale, kc

    kc_t = [pg[0:MLA_KV_LORA, :].astype(BF16) for pg in pages]
    s = [(_dot(ql, kc) + _dot(qr, pg[MLA_KV_LORA:MLA_KV_LORA + MLA_ROPE, :])) * scale
         for kc, pg in zip(kc_t, pages)]
    _online_update(jnp.concatenate(s, axis=1), None, jnp.concatenate(kc_t, axis=1),
                   m_ref, l_ref, acc_ref, v_t=True)

    @pl.when(s_i == pl.num_programs(1) - 1)
    def _():
        pad_scr[...] = jnp.zeros(pad_scr.shape, F32)
        pad_scr[0:t, :] = new_ref[...]
        s, kc = scores(pad_scr[...])
        col = lax.broadcasted_iota(jnp.int32, (r_n, PAGE), 1)
        tq = lax.rem(lax.broadcasted_iota(jnp.int32, (r_n, PAGE), 0), t)
        _online_update(s, (col < t) & (col <= tq), kc, m_ref, l_ref, acc_ref)
        o_ref[...] = _online_result(l_ref, acc_ref).astype(o_ref.dtype)


def _mla_past_attn(page_table, cache, ql, qr, new_lat, t):
    b, npg = page_table.shape
    n_pg = min(PAGES_PER_STEP, npg)
    r_n = ql.shape[1]
    width = cache.shape[1]
    per_b = lambda shp: pl.BlockSpec((None,) + shp, lambda i, s, pt: (i,) + (0,) * len(shp))
    gs = pltpu.PrefetchScalarGridSpec(
        num_scalar_prefetch=1, grid=(b, npg // n_pg),
        in_specs=_page_specs(width, n_pg) + [per_b((r_n, MLA_KV_LORA)), per_b((r_n, MLA_ROPE)), per_b((t, width))],
        out_specs=per_b((r_n, MLA_KV_LORA)),
        scratch_shapes=[pltpu.VMEM((r_n, 1), F32), pltpu.VMEM((r_n, 1), F32), pltpu.VMEM((r_n, MLA_KV_LORA), F32),
                        pltpu.VMEM((PAGE, width), F32)])
    return pl.pallas_call(
        functools.partial(_mla_past_kernel, n_pg=n_pg, t=t), grid_spec=gs,
        out_shape=jax.ShapeDtypeStruct((b, r_n, MLA_KV_LORA), BF16),
        compiler_params=_cparams('arbitrary', 'arbitrary'), name='mla_past')(
            page_table, *([cache] * n_pg), ql, qr, new_lat)


def _to_slab(z, b, t):
    return jnp.transpose(z.reshape(b, t, -1, LANES), (0, 2, 1, 3))


def _from_slab(s, b, t):
    return jnp.transpose(s, (0, 2, 1, 3)).reshape(b * t, -1)


def _nsa_past(h, g, w, b, t, tabs, start, past, occ):
    assert t < CMP_STRIDE and start % SLC_BLOCK == 0
    cos, sin = tabs[NSA_DH // 2]
    m = b * t
    rope = _rope_aux(cos, sin, min(1024, m), None)
    q = _mm(h, w['q'], g=g, epi=_rope64_epi, aux=rope, out_dtype=BF16)
    kv = _mm(h, w['kv'], g=g, epi=_rope_kv_epi, aux=rope, layout='tiles', tn=256)
    gates = _mm(h, w['g'], g=g, epi=_sigmoid_epi)
    rows = [kv[i].reshape(b, t, 256) for i in range(3)]
    hpg = NSA_HEADS // NSA_G
    r_n = hpg * NSA_G * t
    q5 = jnp.transpose(q.reshape(b, t, NSA_G, hpg, NSA_DH), (0, 3, 2, 1, 4))
    eye = jnp.eye(NSA_G, dtype=q5.dtype)
    q_ext = (q5[:, :, :, :, None, :] * eye[None, None, :, None, :, None]).reshape(b, r_n, NSA_G * NSA_DH)
    pt = past['page_table']
    width = 2 * NSA_G * NSA_DH
    cmp_cache = _pages_feature_major(past['nsa_cmp'][occ])
    slc_cache = _pages_feature_major(past['nsa_slc'][occ])
    win_buf = past['nsa_win'][occ].reshape(b, -1, width)
    win_buf_t = _pages_feature_major(past['nsa_win'][occ])
    (ocmp, imp), nc = _nsa_cmp_past(pt, cmp_cache, q_ext, w, t, start)
    gtn = NSA_G * t
    sel = _nsa_select_past(imp.reshape(b * gtn, -1), nc, t, start)
    sel = jnp.transpose(sel).reshape(b, gtn, -1)
    n_past_blk = start // SLC_BLOCK
    key_mask = jnp.repeat(sel[:, :, :n_past_blk], SLC_BLOCK, axis=-1).astype(BF16)
    new_flag = jnp.broadcast_to(sel[:, :, n_past_blk:n_past_blk + 1], (b, gtn, LANES))
    g5 = jnp.transpose(gates[:, :3 * NSA_HEADS].reshape(b, t, 3, NSA_G, hpg), (0, 4, 3, 1, 2))
    gates_r = jnp.pad(g5.reshape(b, r_n, 3), ((0, 0), (0, 0), (0, LANES - 3)))
    o = _nsa_slcwin_past(pt, slc_cache, key_mask, q_ext, rows[1], rows[2], win_buf_t, new_flag, gates_r, ocmp, t)
    o5 = o.reshape(b, hpg, NSA_G, t, NSA_G, NSA_DH)
    o = jnp.stack([o5[:, :, gi, :, gi, :] for gi in range(NSA_G)], axis=2)
    o = jnp.transpose(o, (0, 3, 2, 1, 4)).reshape(m, NSA_HEADS * NSA_DH).astype(BF16)
    h = _mm(o, w['out'], res=h)
    win = jnp.concatenate([win_buf, rows[2]], axis=1)[:, -win_buf.shape[1]:]
    return h, rows, win


def _ml_past(h, g, w, b, t, past, occ):
    z = _to_slab(_mm(h, w['main'], g=g, epi=_sig_tile2), b, t)
    gc = _mm(h, w['gate'], g=g).reshape(b, t, LANES)
    gr = jnp.swapaxes(gc[:, :, :2 * ML_HEADS], 1, 2)
    hs, c_f, n_f, m_f = _mlstm(z, gc, gr, w['gate_b'],
                               state=(past['ml_C'][occ], past['ml_n'][occ], past['ml_m'][occ]))
    return _mm(_from_slab(hs, b, t), w['out'], res=h), (c_f, n_f, m_f)


def _mla_past(h, g, w, b, t, tabs, past, occ):
    cos, sin = tabs[MLA_ROPE // 2]
    m = b * t
    rope = _rope_aux(cos, sin, min(1024, m), None)
    lat = _mm(h, w['w_in'], g=g, epi=_mla_in_epi,
              aux=[_const_aux(w['q_norm']), _const_aux(w['kv_norm'])] + rope)
    qn = _mm(lat, w['uq_nope'], x_cols=(MLA_Q_LORA, 0), out_dtype=BF16)
    qr = _mm(lat, w['uq_rope'], x_cols=(MLA_Q_LORA, 0), epi=_rope32_epi, aux=rope, out_dtype=BF16)
    ql = _headmm(qn, w['ukt'])
    new_lat = lat[:, MLA_Q_LORA:MLA_Q_LORA + MLA_KV_LORA + MLA_ROPE].reshape(b, t, -1)
    hd = lambda a: jnp.transpose(a.reshape(b, t, MLA_HEADS, -1), (0, 2, 1, 3)).reshape(b, MLA_HEADS * t, -1)
    cache = _pages_feature_major(past['mla'][occ])
    ol = _mla_past_attn(past['page_table'], cache, hd(ql), hd(qr), new_lat, t)
    ol = jnp.transpose(ol.reshape(b, MLA_HEADS, t, -1), (0, 2, 1, 3)).reshape(m, -1)
    o = _headmm(ol, w['uv'])
    return _mm(o, w['out'], res=h), new_lat


def _hg_past(h, g, w_in, lower, f_b, g_norm, w_out, b, t, state):
    z = _to_slab(_mm(h, w_in, g=g), b, t)
    os_, s_f = _hgrn(z, lower, f_b, g_norm, state=state)
    return _mm(_from_slab(os_, b, t), w_out, res=h), s_f


def _prepare(prm):
    depth = prm['norm_mix'].shape[0]
    sm = jax.nn.softmax(prm['hg_lb_logits'].astype(F32), axis=0)
    lower = jnp.cumsum(sm, axis=0) - sm[0]
    w = dict(depth=depth, lower=lower)
    w['nsa'] = [_prep_nsa(prm['nsa_w_in'][o], prm['nsa_cmp_pe'][o], prm['nsa_cmp_w1'][o], prm['nsa_cmp_w2'][o],
                          prm['nsa_w_out'][o]) for o in range(prm['nsa_w_in'].shape[0])]
    w['ml'] = [_prep_ml(prm['ml_w_in'][o], prm['ml_gate_b'][o], prm['ml_w_out'][o])
               for o in range(prm['ml_w_in'].shape[0])]
    w['mla'] = [_prep_mla(prm['mla_w_in'][o], prm['mla_q_norm'][o], prm['mla_kv_norm'][o], prm['mla_w_uq'][o],
                          prm['mla_w_uk'][o], prm['mla_w_uv'][o], prm['mla_w_out'][o])
                for o in range(prm['mla_w_in'].shape[0])]
    w['hg'] = [dict(w_in=prm['hg_w_in'][o].astype(BF16), f_b=prm['hg_f_b'][o], norm=prm['hg_norm'][o],
                    out=prm['hg_w_out'][o].astype(BF16)) for o in range(prm['hg_w_in'].shape[0])]
    w['ffn'] = [dict(up=prm['ffn_w_up'][i].astype(BF16),
                     conv4=jnp.concatenate([prm['ffn_conv_w'][i], prm['ffn_conv_b'][i][None]], axis=0).astype(F32),
                     down=prm['ffn_w_down'][i].astype(BF16)) for i in range(depth)]
    w['ple'] = [dict(proj=prm['ple_w_proj'][i].astype(BF16), gate=prm['ple_w_gate'][i].astype(BF16))
                for i in range(depth)]
    return w


def _trunk(x, p, start, past, prm, w):
    b, t, d = x.shape
    depth = w['depth']
    pos = start + jnp.arange(t, dtype=jnp.int32)
    if past is None:
        tabs = {hf: _rope_tables(pos, hf) for hf in (NSA_DH // 2, MLA_ROPE // 2)}
    else:
        tabs = {hf: tuple(jnp.tile(a, (b, 1)) for a in _rope_tables(pos, hf)) for hf in (NSA_DH // 2, MLA_ROPE // 2)}
    new = {}
    h = x.reshape(b * t, d)
    for i in range(depth):
        kind, occ = i % 4, i // 4
        g = prm['norm_mix'][i]
        if kind == 0:
            if past is None:
                h, rows = _nsa_fresh(h, g, w['nsa'][occ], b, t, tabs)
                win = rows[2][:, -min(WINDOW, t):]
            else:
                h, rows, win = _nsa_past(h, g, w['nsa'][occ], b, t, tabs, start, past, occ)
            for name, r in zip(('nsa_cmp', 'nsa_slc'), rows[:2]):
                new.setdefault(name, []).append(r.reshape(b, t, 2, NSA_G, NSA_DH))
            new.setdefault('nsa_win', []).append(win.reshape(b, win.shape[1], 2, NSA_G, NSA_DH))
        elif kind == 1:
            if past is None:
                h, st = _ml_fresh(h, g, w['ml'][occ], b, t)
            else:
                h, st = _ml_past(h, g, w['ml'][occ], b, t, past, occ)
            for name, s in zip(('ml_C', 'ml_n', 'ml_m'), st):
                new.setdefault(name, []).append(s)
        elif kind == 2:
            if past is None:
                h, lat = _mla_fresh(h, g, w['mla'][occ], b, t, tabs)
            else:
                h, lat = _mla_past(h, g, w['mla'][occ], b, t, tabs, past, occ)
            new.setdefault('mla', []).append(lat)
        else:
            hw = w['hg'][occ]
            if past is None:
                h, s_f = _hg_fresh(h, g, hw['w_in'], w['lower'][i], hw['f_b'], hw['norm'], hw['out'], b, t)
            else:
                h, s_f = _hg_past(h, g, hw['w_in'], w['lower'][i], hw['f_b'], hw['norm'], hw['out'], b, t,
                                  past['hg_S'][occ])
            new.setdefault('hg_S', []).append(s_f)
        fw = w['ffn'][i]
        f = fw['down'].shape[0]
        if past is None:
            h, sa, sg = _ffn(h, prm['norm_ffn'][i], fw['up'], fw['conv4'], fw['down'], t)
            new.setdefault('ffn_conv', []).append(jnp.concatenate([sa, sg], axis=-1))
        else:
            buf = past['ffn_conv'][i]
            zero = jnp.zeros((b, t - 1, 2 * f), F32)
            p1 = jnp.concatenate([buf[:, 1:2], zero], axis=1).reshape(b * t, 2 * f)
            p2 = jnp.concatenate([buf, zero[:, 1:]], axis=1).reshape(b * t, 2 * f)
            h, ua, ug = _ffn(h, prm['norm_ffn'][i], fw['up'], fw['conv4'], fw['down'], t, prev=(p1, p2))
            u = jnp.concatenate([ua, ug], axis=-1).reshape(b, t, 2 * f)
            new.setdefault('ffn_conv', []).append(u[:, -(CONV_W - 1):])
        pw = w['ple'][i]
        h = _ple(h, prm['norm_ple'][i], pw['gate'], p[i].reshape(b * t, -1), pw['proj'],
                 final_g=prm['norm_final'] if i == depth - 1 else None)
    return h.reshape(b, t, d), {k: jnp.stack(v) for k, v in new.items()}


def kernel(x_prompt, x_sample, cache_nsa_cmp_kv, cache_nsa_slc_kv, state_nsa_win_kv, cache_mla_latent,
           state_mlstm_C, state_mlstm_n, state_mlstm_m, state_hgrn_S, state_ffn_conv, page_table,
           p_prompt, p_sample, norm_mix, norm_ffn, norm_ple, norm_final, nsa_w_in, nsa_cmp_pe, nsa_cmp_w1,
           nsa_cmp_w2, nsa_w_out, ml_w_in, ml_gate_b, ml_w_out, mla_w_in, mla_q_norm, mla_kv_norm, mla_w_uq,
           mla_w_uk, mla_w_uv, mla_w_out, hg_w_in, hg_f_b, hg_lb_logits, hg_norm, hg_w_out, ffn_w_up,
           ffn_conv_w, ffn_conv_b, ffn_w_down, ple_w_proj, ple_w_gate):
    prm = {
        'norm_mix': norm_mix, 'norm_ffn': norm_ffn, 'norm_ple': norm_ple, 'norm_final': norm_final,
        'nsa_w_in': nsa_w_in, 'nsa_cmp_pe': nsa_cmp_pe, 'nsa_cmp_w1': nsa_cmp_w1, 'nsa_cmp_w2': nsa_cmp_w2,
        'nsa_w_out': nsa_w_out, 'ml_w_in': ml_w_in, 'ml_gate_b': ml_gate_b, 'ml_w_out': ml_w_out,
        'mla_w_in': mla_w_in, 'mla_q_norm': mla_q_norm, 'mla_kv_norm': mla_kv_norm, 'mla_w_uq': mla_w_uq,
        'mla_w_uk': mla_w_uk, 'mla_w_uv': mla_w_uv, 'mla_w_out': mla_w_out, 'hg_w_in': hg_w_in,
        'hg_f_b': hg_f_b, 'hg_lb_logits': hg_lb_logits, 'hg_norm': hg_norm, 'hg_w_out': hg_w_out,
        'ffn_w_up': ffn_w_up, 'ffn_conv_w': ffn_conv_w, 'ffn_conv_b': ffn_conv_b, 'ffn_w_down': ffn_w_down,
        'ple_w_proj': ple_w_proj, 'ple_w_gate': ple_w_gate,
    }
    past = {
        'nsa_cmp': cache_nsa_cmp_kv, 'nsa_slc': cache_nsa_slc_kv, 'nsa_win': state_nsa_win_kv,
        'mla': cache_mla_latent, 'ml_C': state_mlstm_C, 'ml_n': state_mlstm_n, 'ml_m': state_mlstm_m,
        'hg_S': state_hgrn_S, 'ffn_conv': state_ffn_conv, 'page_table': page_table,
    }
    w = _prepare(prm)
    past_len = page_table.shape[1] * PAGE
    y_p, sp = _trunk(x_prompt, p_prompt, 0, None, prm, w)
    y_s, ss = _trunk(x_sample, p_sample, past_len, past, prm, w)
    return (y_p, y_s,
            sp['nsa_cmp'], ss['nsa_cmp'], sp['nsa_slc'], ss['nsa_slc'], sp['nsa_win'], ss['nsa_win'],
            sp['mla'], ss['mla'], sp['ml_C'], ss['ml_C'], sp['ml_n'], ss['ml_n'], sp['ml_m'], ss['ml_m'],
            sp['hg_S'], ss['hg_S'], sp['ffn_conv'], ss['ffn_conv'])
```

```python
import functools
import math

import numpy as np
import jax
import jax.numpy as jnp
from jax import lax
from jax.experimental import pallas as pl
from jax.experimental.pallas import tpu as pltpu

F32 = jnp.float32
BF16 = jnp.bfloat16
NEG = -1e30
NORM_EPS = 1e-6
ROPE_THETA = 10000.0
FORCE_BONUS = 1e4

LANES = 128
VMEM_LIMIT = 56 * 1024 * 1024

PAGE = 128
NSA_HEADS, NSA_G, NSA_DH = 16, 2, 64
CMP_LEN, CMP_STRIDE, CMP_HID = 32, 16, 256
SLC_BLOCK, SLC_TOPN, WINDOW = 64, 16, 512
ML_HEADS, ML_DK, ML_DV = 4, 128, 256
MLA_HEADS, MLA_Q_LORA, MLA_KV_LORA, MLA_NOPE, MLA_ROPE, MLA_DV = 16, 512, 256, 64, 32, 64
HG_HEADS, HG_DK, HG_DV = 8, 128, 128
CONV_W = 3


def _cparams(*sem):
    return pltpu.CompilerParams(dimension_semantics=sem, vmem_limit_bytes=VMEM_LIMIT)


def _dot(a, b):
    return jnp.dot(a.astype(BF16), b.astype(BF16), preferred_element_type=F32)


def _dot_nt(a, b):
    return lax.dot_general(a.astype(BF16), b.astype(BF16), (((1,), (1,)), ((), ())),
                           preferred_element_type=F32)


def _dot_f32(a, b):
    return jnp.dot(a, b, precision=lax.Precision.HIGHEST, preferred_element_type=F32)


def _dot_nt_f32(a, b):
    return lax.dot_general(a, b, (((1,), (1,)), ((), ())), precision=lax.Precision.HIGHEST,
                           preferred_element_type=F32)


def _rms(x, g):
    return x * lax.rsqrt(jnp.mean(x * x, axis=-1, keepdims=True) + NORM_EPS) * g


def _sigmoid(x):
    return 1.0 / (1.0 + jnp.exp(-x))


def _silu(x):
    return x * _sigmoid(x)


def _rope_tile(z, cos, sin, half):
    n = z.shape[1]
    reps = n // LANES
    c = jnp.concatenate([cos] * reps, axis=1) if reps > 1 else cos
    s = jnp.concatenate([sin] * reps, axis=1) if reps > 1 else sin
    lane = lax.broadcasted_iota(jnp.int32, z.shape, 1)
    lower = (lane & (2 * half - 1)) < half
    partner = jnp.where(lower, pltpu.roll(z, n - half, 1), pltpu.roll(z, half, 1))
    return z * c + partner * s


def _rope_tables(pos, half):
    inv = jnp.power(ROPE_THETA, -jnp.arange(half, dtype=F32) / half)
    ang = pos.astype(F32)[:, None] * inv[None, :]
    cos, sin = jnp.cos(ang), jnp.sin(ang)
    reps = LANES // (2 * half)
    return (jnp.tile(jnp.concatenate([cos, cos], axis=1), (1, reps)),
            jnp.tile(jnp.concatenate([-sin, sin], axis=1), (1, reps)))


def _mm_kernel(*refs, norm, res, epi, n_aux, x_slabs, out_slabs):
    x_ref = refs[0]
    pos = 1
    g_ref = None
    if norm:
        g_ref = refs[pos]
        pos += 1
    w_ref = refs[pos]
    pos += 1
    aux = refs[pos:pos + n_aux]
    pos += n_aux
    r_ref = None
    if res:
        r_ref = refs[pos]
        pos += 1
    o_ref = refs[pos]
    j = pl.program_id(1)
    if norm:
        xn_ref = refs[pos + 1]

        @pl.when(j == 0)
        def _():
            xn_ref[...] = _rms(x_ref[...], g_ref[...]).astype(BF16)

        x = xn_ref[...]
    elif x_slabs:
        x = jnp.concatenate([x_ref[s] for s in range(x_slabs)], axis=1)
    else:
        x = x_ref[...]
    z = _dot(x, w_ref[...])
    if epi is not None:
        z = epi(z, j, *[a[...] for a in aux])
    if res:
        z = z + r_ref[...]
    if out_slabs:
        for s in range(out_slabs):
            o_ref[s] = z[:, s * LANES:(s + 1) * LANES].astype(o_ref.dtype)
    else:
        o_ref[...] = z.astype(o_ref.dtype)


def _mm(x, w, *, g=None, res=None, epi=None, aux=(), out_dtype=F32, layout='flat',
        seq=None, tm=1024, tn=1024, x_cols=None, x_slab_seq=None):
    k, n = w.shape
    if x_slab_seq is not None:
        b, ks, t, _ = x.shape
        m = b * t
    else:
        m = x.shape[0]
        t = seq
    tm = min(tm, m if t is None else t)
    tn = min(tn, n)
    assert m % tm == 0 and n % tn == 0
    ni, nj = m // tm, n // tn
    tps = None if t is None else t // tm
    in_specs, args = [], []
    if x_slab_seq is not None:
        in_specs.append(pl.BlockSpec((None, ks, tm, LANES), lambda i, j: (i // tps, 0, i % tps, 0)))
    elif x_cols is not None:
        in_specs.append(pl.BlockSpec((tm, x_cols[0]), lambda i, j: (i, x_cols[1])))
    else:
        in_specs.append(pl.BlockSpec((tm, k), lambda i, j: (i, 0)))
    args.append(x)
    if g is not None:
        in_specs.append(pl.BlockSpec((1, k), lambda i, j: (0, 0)))
        args.append(g.reshape(1, k).astype(F32))
    in_specs.append(pl.BlockSpec((k, tn), lambda i, j: (0, j)))
    args.append(w)
    for a, bs, im in aux:
        in_specs.append(pl.BlockSpec(bs, im))
        args.append(a)
    if res is not None:
        in_specs.append(pl.BlockSpec((tm, tn), lambda i, j: (i, j)))
        args.append(res)
    if layout == 'flat':
        out_shape = jax.ShapeDtypeStruct((m, n), out_dtype)
        out_spec = pl.BlockSpec((tm, tn), lambda i, j: (i, j))
        out_slabs = 0
    elif layout == 'tiles':
        out_shape = jax.ShapeDtypeStruct((nj, m, tn), out_dtype)
        out_spec = pl.BlockSpec((None, tm, tn), lambda i, j: (j, i, 0))
        out_slabs = 0
    else:
        out_slabs = tn // LANES
        out_shape = jax.ShapeDtypeStruct((m // t, n // LANES, t, LANES), out_dtype)
        out_spec = pl.BlockSpec((None, out_slabs, tm, LANES), lambda i, j: (i // tps, j, i % tps, 0))
    scratch = [pltpu.VMEM((tm, k), BF16)] if g is not None else []
    kern = functools.partial(_mm_kernel, norm=g is not None, res=res is not None, epi=epi,
                             n_aux=len(aux), x_slabs=(ks if x_slab_seq is not None else 0),
                             out_slabs=out_slabs)
    return pl.pallas_call(
        kern, grid=(ni, nj), in_specs=in_specs, out_specs=out_spec, out_shape=out_shape,
        scratch_shapes=scratch, compiler_params=_cparams('parallel', 'arbitrary'),
        name='mm')(*args)


def _rope_aux(cos, sin, tm, period_blocks):
    if period_blocks is None:
        im = lambda i, j: (i, 0)
    else:
        im = lambda i, j: (i % period_blocks, 0)
    return [(cos, (tm, LANES), im), (sin, (tm, LANES), im)]


def _headmm_kernel(x_ref, w_ref, o_ref, *, heads, a, c):
    for h in range(heads):
        o_ref[:, h * c:(h + 1) * c] = _dot(x_ref[:, h * a:(h + 1) * a], w_ref[h]).astype(o_ref.dtype)


def _headmm(x, w, out_dtype=BF16, tm=512):
    m = x.shape[0]
    heads, a, c = w.shape
    tm = min(tm, m)
    return pl.pallas_call(
        functools.partial(_headmm_kernel, heads=heads, a=a, c=c), grid=(m // tm,),
        in_specs=[pl.BlockSpec((tm, heads * a), lambda i: (i, 0)),
                  pl.BlockSpec((heads, a, c), lambda i: (0, 0, 0))],
        out_specs=pl.BlockSpec((tm, heads * c), lambda i: (i, 0)),
        out_shape=jax.ShapeDtypeStruct((m, heads * c), out_dtype),
        compiler_params=_cparams('parallel'), name='headmm')(x, w)


def _ffn_kernel(*refs, tps, seq_len, has_prev):
    if has_prev:
        (h_ref, g_ref, wa_ref, wg_ref, ca_ref, cg_ref, wd_ref, p1a_ref, p1g_ref, p2a_ref, p2g_ref,
         o_ref, ua_ref, ug_ref, hn_ref) = refs
    else:
        (h_ref, g_ref, wa_ref, wg_ref, ca_ref, cg_ref, wd_ref,
         o_ref, sa_ref, sg_ref, hn_ref, carry_ref) = refs
    i, j = pl.program_id(0), pl.program_id(1)
    tm = h_ref.shape[0]
    ch = min(64, tm)

    @pl.when(j == 0)
    def _():
        x = h_ref[...]
        hn_ref[...] = _rms(x, g_ref[...]).astype(BF16)
        o_ref[...] = x

    hn = hn_ref[...]
    ua = _dot(hn, wa_ref[...])
    ug = _dot(hn, wg_ref[...])
    if has_prev:
        assert ch % seq_len == 0
        ua_ref[...] = ua
        ug_ref[...] = ug
        prevs = (None, None)
    else:
        @pl.when(i % tps == 0)
        def _():
            carry_ref[j] = jnp.zeros(carry_ref.shape[1:], F32)

        prevs = (carry_ref[j, 0], carry_ref[j, 1])
        carry_ref[j, 0] = ua[tm - 8:]
        carry_ref[j, 1] = ug[tm - 8:]
        sa_ref[...] = pltpu.roll(ua[tm - 8:], CONV_W - 1, 0)[0:CONV_W - 1]
        sg_ref[...] = pltpu.roll(ug[tm - 8:], CONV_W - 1, 0)[0:CONV_W - 1]

    def conv_chunk(u, cw, prev8, p1_ref, p2_ref, r0):
        cur = u[r0:r0 + ch]
        if has_prev:
            t = lax.broadcasted_iota(jnp.int32, cur.shape, 0) % seq_len
            s1 = jnp.where(t >= 1, pltpu.roll(cur, 1, 0), p1_ref[r0:r0 + ch, :])
            s2 = jnp.where(t >= 2, pltpu.roll(cur, 2, 0), p2_ref[r0:r0 + ch, :])
        else:
            x = jnp.concatenate([prev8 if r0 == 0 else u[r0 - 8:r0], cur], axis=0)
            s1 = pltpu.roll(x, 1, 0)[8:]
            s2 = pltpu.roll(x, 2, 0)[8:]
        return cw[0:1] * s2 + cw[1:2] * s1 + cw[2:3] * cur + cw[3:4]

    cwa, cwg = ca_ref[...], cg_ref[...]
    acts = []
    for c in range(tm // ch):
        ca = conv_chunk(ua, cwa, prevs[0], p1a_ref if has_prev else None, p2a_ref if has_prev else None, c * ch)
        cg = conv_chunk(ug, cwg, prevs[1], p1g_ref if has_prev else None, p2g_ref if has_prev else None, c * ch)
        acts.append((_silu(ca) * cg).astype(BF16))
    act = jnp.concatenate(acts, axis=0) if len(acts) > 1 else acts[0]
    o_ref[...] += _dot(act, wd_ref[...])


def _ffn_seq_kernel(h_ref, g_ref, wa_ref, wg_ref, ca_ref, cg_ref, wd_ref, o_ref, sa_ref, sg_ref,
                    hn_ref, carry_ref, ua0_ref, ug0_ref, ua1_ref, ug1_ref, *, tps):
    i, j = pl.program_id(0), pl.program_id(1)
    tm = h_ref.shape[0]
    ch = min(64, tm)

    @pl.when(j == 0)
    def _():
        x = h_ref[...]
        hn_ref[...] = _rms(x, g_ref[...]).astype(BF16)
        o_ref[...] = x

    @pl.when((i == 0) & (j == 0))
    def _():
        ua1_ref[...] = jnp.zeros(ua1_ref.shape, F32)
        ug1_ref[...] = jnp.zeros(ug1_ref.shape, F32)

    def step(rd_a, rd_g, wr_a, wr_g):
        hn = hn_ref[...]
        wr_a[...] = _dot(hn, wa_ref[...])
        wr_g[...] = _dot(hn, wg_ref[...])

        jm = jnp.maximum(j - 1, 0)
        live = j >= 1

        @pl.when(i % tps == 0)
        def _():
            carry_ref[jm] = jnp.zeros(carry_ref.shape[1:], F32)

        ua, ug = rd_a[...], rd_g[...]
        prevs = (carry_ref[jm, 0], carry_ref[jm, 1])
        carry_ref[jm, 0] = jnp.where(live, ua[tm - 8:], prevs[0])
        carry_ref[jm, 1] = jnp.where(live, ug[tm - 8:], prevs[1])
        sa_ref[...] = pltpu.roll(ua[tm - 8:], CONV_W - 1, 0)[0:CONV_W - 1]
        sg_ref[...] = pltpu.roll(ug[tm - 8:], CONV_W - 1, 0)[0:CONV_W - 1]

        def conv_chunk(u, cw, prev8, r0):
            cur = u[r0:r0 + ch]
            x = jnp.concatenate([prev8 if r0 == 0 else u[r0 - 8:r0], cur], axis=0)
            return cw[0:1] * pltpu.roll(x, 2, 0)[8:] + cw[1:2] * pltpu.roll(x, 1, 0)[8:] + cw[2:3] * cur + cw[3:4]

        cwa, cwg = ca_ref[...], cg_ref[...]
        acts = []
        for c in range(tm // ch):
            a = _silu(conv_chunk(ua, cwa, prevs[0], c * ch)) * conv_chunk(ug, cwg, prevs[1], c * ch)
            acts.append(a.astype(BF16))
        act = jnp.concatenate(acts, axis=0) if len(acts) > 1 else acts[0]
        o_ref[...] += _dot(jnp.where(live, act, jnp.zeros_like(act)), wd_ref[...])

    @pl.when(j % 2 == 0)
    def _():
        step(ua1_ref, ug1_ref, ua0_ref, ug0_ref)

    @pl.when(j % 2 == 1)
    def _():
        step(ua0_ref, ug0_ref, ua1_ref, ug1_ref)


def _ffn_seq(h, g, w_up, conv4, w_down, seq_len, tn=256):
    m, d = h.shape
    f = w_down.shape[0]
    nj = f // tn
    tm = min(1024, seq_len)
    tps = seq_len // tm
    ni = m // tm
    cur = lambda j: jnp.minimum(j, nj - 1)
    prv = lambda j: jnp.maximum(j - 1, 0)
    specs = [pl.BlockSpec((tm, d), lambda i, j: (i, 0)),
             pl.BlockSpec((1, d), lambda i, j: (0, 0)),
             pl.BlockSpec((d, tn), lambda i, j: (0, cur(j))),
             pl.BlockSpec((d, tn), lambda i, j: (0, cur(j) + nj)),
             pl.BlockSpec((4, tn), lambda i, j: (0, prv(j))),
             pl.BlockSpec((4, tn), lambda i, j: (0, prv(j) + nj)),
             pl.BlockSpec((tn, d), lambda i, j: (prv(j), 0))]
    out_shape = [jax.ShapeDtypeStruct((m, d), F32), jax.ShapeDtypeStruct((ni, CONV_W - 1, f), F32),
                 jax.ShapeDtypeStruct((ni, CONV_W - 1, f), F32)]
    out_specs = [pl.BlockSpec((tm, d), lambda i, j: (i, 0)),
                 pl.BlockSpec((None, CONV_W - 1, tn), lambda i, j: (i, 0, prv(j))),
                 pl.BlockSpec((None, CONV_W - 1, tn), lambda i, j: (i, 0, prv(j)))]
    scratch = [pltpu.VMEM((tm, d), BF16), pltpu.VMEM((nj, 2, 8, tn), F32)] + [pltpu.VMEM((tm, tn), F32)] * 4
    outs = pl.pallas_call(
        functools.partial(_ffn_seq_kernel, tps=tps), grid=(ni, nj + 1), in_specs=specs, out_specs=out_specs,
        out_shape=out_shape, scratch_shapes=scratch, compiler_params=_cparams('arbitrary', 'arbitrary'),
        name='ffn_seq')(h, g.reshape(1, d).astype(F32), w_up, w_up, conv4, conv4, w_down)
    return outs[0], outs[1][tps - 1::tps], outs[2][tps - 1::tps]


def _ffn(h, g, w_up, conv4, w_down, seq_len, prev=None, tn=256):
    m, d = h.shape
    f = w_down.shape[0]
    nj = f // tn
    has_prev = prev is not None
    tm = m if has_prev else min(1024, seq_len)
    tps = max(seq_len // tm, 1)
    ni = m // tm
    specs = [pl.BlockSpec((tm, d), lambda i, j: (i, 0)),
             pl.BlockSpec((1, d), lambda i, j: (0, 0)),
             pl.BlockSpec((d, tn), lambda i, j: (0, j)),
             pl.BlockSpec((d, tn), lambda i, j: (0, j + nj)),
             pl.BlockSpec((4, tn), lambda i, j: (0, j)),
             pl.BlockSpec((4, tn), lambda i, j: (0, j + nj)),
             pl.BlockSpec((tn, d), lambda i, j: (j, 0))]
    args = [h, g.reshape(1, d).astype(F32), w_up, w_up, conv4, conv4, w_down]
    scratch = [pltpu.VMEM((tm, d), BF16)]
    if has_prev:
        p1, p2 = prev
        specs += [pl.BlockSpec((tm, tn), lambda i, j: (i, j)), pl.BlockSpec((tm, tn), lambda i, j: (i, j + nj)),
                  pl.BlockSpec((tm, tn), lambda i, j: (i, j)), pl.BlockSpec((tm, tn), lambda i, j: (i, j + nj))]
        args += [p1, p1, p2, p2]
        out_shape = [jax.ShapeDtypeStruct((m, d), F32), jax.ShapeDtypeStruct((m, f), F32),
                     jax.ShapeDtypeStruct((m, f), F32)]
        out_specs = [pl.BlockSpec((tm, d), lambda i, j: (i, 0)), pl.BlockSpec((tm, tn), lambda i, j: (i, j)),
                     pl.BlockSpec((tm, tn), lambda i, j: (i, j))]
    else:
        out_shape = [jax.ShapeDtypeStruct((m, d), F32), jax.ShapeDtypeStruct((ni, CONV_W - 1, f), F32),
                     jax.ShapeDtypeStruct((ni, CONV_W - 1, f), F32)]
        out_specs = [pl.BlockSpec((tm, d), lambda i, j: (i, 0)),
                     pl.BlockSpec((None, CONV_W - 1, tn), lambda i, j: (i, 0, j)),
                     pl.BlockSpec((None, CONV_W - 1, tn), lambda i, j: (i, 0, j))]
        scratch.append(pltpu.VMEM((nj, 2, 8, tn), F32))
    outs = pl.pallas_call(
        functools.partial(_ffn_kernel, tps=tps, seq_len=seq_len, has_prev=has_prev),
        grid=(ni, nj), in_specs=specs, out_specs=out_specs, out_shape=out_shape,
        scratch_shapes=scratch, compiler_params=_cparams('arbitrary', 'arbitrary'), name='ffn')(*args)
    if has_prev:
        return outs
    return outs[0], outs[1][tps - 1::tps], outs[2][tps - 1::tps]


def _ple_kernel(*refs, final):
    if final:
        h_ref, g_ref, wg_ref, p_ref, wp_ref, gf_ref, o_ref = refs
    else:
        h_ref, g_ref, wg_ref, p_ref, wp_ref, o_ref = refs
    h = h_ref[...]
    gate = _sigmoid(_dot(_rms(h, g_ref[...]), wg_ref[...]))
    y = h + gate * _dot(p_ref[...], wp_ref[...])
    if final:
        y = _rms(y, gf_ref[...])
    o_ref[...] = y


def _ple(h, g, w_gate, p, w_proj, final_g=None, tm=1024):
    m, d = h.shape
    pd = p.shape[1]
    tm = min(tm, m)
    specs = [pl.BlockSpec((tm, d), lambda i: (i, 0)), pl.BlockSpec((1, d), lambda i: (0, 0)),
             pl.BlockSpec((d, d), lambda i: (0, 0)), pl.BlockSpec((tm, pd), lambda i: (i, 0)),
             pl.BlockSpec((pd, d), lambda i: (0, 0))]
    args = [h, g.reshape(1, d).astype(F32), w_gate, p, w_proj]
    if final_g is not None:
        specs.append(pl.BlockSpec((1, d), lambda i: (0, 0)))
        args.append(final_g.reshape(1, d).astype(F32))
    return pl.pallas_call(
        functools.partial(_ple_kernel, final=final_g is not None), grid=(m // tm,),
        in_specs=specs, out_specs=pl.BlockSpec((tm, d), lambda i: (i, 0)),
        out_shape=jax.ShapeDtypeStruct((m, d), F32), compiler_params=_cparams('parallel'),
        name='ple')(*args)


def _softmax_tile(s, mask):
    s = jnp.where(mask, s, NEG)
    e = jnp.where(mask, jnp.exp(s - jnp.max(s, axis=-1, keepdims=True)), 0.0)
    return e / jnp.maximum(jnp.sum(e, axis=-1, keepdims=True), 1e-30)


def _online_init(m_ref, l_ref, acc_ref):
    m_ref[...] = jnp.full(m_ref.shape, NEG, F32)
    l_ref[...] = jnp.zeros(l_ref.shape, F32)
    acc_ref[...] = jnp.zeros(acc_ref.shape, F32)


def _online_update(s, mask, v, m_ref, l_ref, acc_ref, v_t=False):
    if mask is not None:
        s = jnp.where(mask, s, NEG)
    m_old = m_ref[...]
    m_new = jnp.maximum(m_old, jnp.max(s, axis=-1, keepdims=True))
    p = jnp.exp(s - m_new)
    if mask is not None:
        p = jnp.where(mask, p, 0.0)
    alpha = jnp.exp(m_old - m_new)
    l_ref[...] = alpha * l_ref[...] + jnp.sum(p, axis=-1, keepdims=True)
    acc_ref[...] = alpha * acc_ref[...] + (_dot_nt(p, v) if v_t else _dot(p, v))
    m_ref[...] = m_new


def _online_result(l_ref, acc_ref):
    return acc_ref[...] / jnp.maximum(l_ref[...], 1e-30)


def _online_update_t(st, bias, v_t, m_ref, l_ref, acc_ref):
    if bias is not None:
        st = st + bias
    m_old = m_ref[...]
    m_new = jnp.maximum(m_old, jnp.max(st, axis=0, keepdims=True))
    p = jnp.exp(st - m_new)
    alpha = jnp.exp(m_old - m_new)
    l_ref[...] = alpha * l_ref[...] + jnp.sum(p, axis=0, keepdims=True)
    acc_ref[...] = alpha * acc_ref[...] + _dot(v_t, p)
    m_ref[...] = m_new


def _softmax_tile_t(st, mask):
    st = jnp.where(mask, st, NEG)
    e = jnp.where(mask, jnp.exp(st - jnp.max(st, axis=0, keepdims=True)), 0.0)
    return e / jnp.maximum(jnp.sum(e, axis=0, keepdims=True), 1e-30)


def _cumsum(x, axis):
    n = x.shape[axis]
    idx = lax.broadcasted_iota(jnp.int32, x.shape, axis)
    k = 1
    while k < n:
        x = x + jnp.where(idx >= k, pltpu.roll(x, k, axis), 0.0)
        k *= 2
    return x


def _log_sigmoid(x):
    return jnp.minimum(x, 0.0) - jnp.log(1.0 + jnp.exp(-jnp.abs(x)))


def _compress_rows(k_ref, v_ref, n_seg, pe_ref, w1_ref, w2p_ref):
    outs = []
    seg_w = CMP_STRIDE * NSA_DH
    lane = lax.broadcasted_iota(jnp.int32, (n_seg, LANES), 1)
    for kind, rows_ref in enumerate((k_ref, v_ref)):
        xs = [rows_ref[pl.ds(s, n_seg, stride=CMP_STRIDE), :] for s in range(CMP_STRIDE)]
        o = None
        for g in range(NSA_G):
            pieces = []
            for a in range(CMP_STRIDE // 2):
                ev, od = xs[2 * a], xs[2 * a + 1]
                if g == 0:
                    pieces.append(jnp.where(lane < NSA_DH, ev, pltpu.roll(od, NSA_DH, 1)))
                else:
                    pieces.append(jnp.where(lane < NSA_DH, pltpu.roll(ev, NSA_DH, 1), od))
            seg = jnp.concatenate(pieces, axis=1)
            pre = None
            for r in range(CMP_LEN // CMP_STRIDE):
                acc = _dot(seg + pe_ref[kind, r:r + 1, :], w1_ref[kind, r * seg_w:(r + 1) * seg_w, :])
                pre = acc if r == 0 else pre + pltpu.roll(acc, n_seg - r, 0)
            t = _dot(_silu(pre), w2p_ref[kind, g])
            o = t if o is None else o + t
        outs.append(o)
    return outs


def _nsa_cmp_kernel(k_ref, v_ref, pe_ref, w1_ref, w2p_ref, kc_ref, vc_ref, *, n_seg):
    kc, vc = _compress_rows(k_ref, v_ref, n_seg, pe_ref, w1_ref, w2p_ref)
    kc_ref[...] = kc
    vc_ref[...] = vc


def _nsa_compress(rows, pe, w1, w2p):
    b, t, _ = rows.shape
    n_seg = t // CMP_STRIDE
    full = lambda shp: pl.BlockSpec(shp, lambda i: (0,) * len(shp))
    return pl.pallas_call(
        functools.partial(_nsa_cmp_kernel, n_seg=n_seg), grid=(b,),
        in_specs=[pl.BlockSpec((None, t, LANES), lambda i: (i, 0, 0)),
                  pl.BlockSpec((None, t, LANES), lambda i: (i, 0, 1)), full(pe.shape), full(w1.shape),
                  full(w2p.shape)],
        out_specs=[pl.BlockSpec((None, n_seg, LANES), lambda i: (i, 0, 0))] * 2,
        out_shape=[jax.ShapeDtypeStruct((b, n_seg, LANES), F32)] * 2,
        compiler_params=_cparams('parallel'), name='nsa_compress')(rows, rows, pe, w1, w2p)


def _overlap_t(nc, ns, nc_pad, ns_pad):
    cs = np.arange(nc)[None, :] * CMP_STRIDE
    ss = np.arange(ns)[:, None] * SLC_BLOCK
    ov = np.maximum(np.minimum(cs + CMP_LEN, ss + SLC_BLOCK) - np.maximum(cs, ss), 0) / CMP_LEN
    out = np.zeros((ns_pad, nc_pad), np.float32)
    out[:ns, :nc] = ov
    return jnp.asarray(out)


def _select_blocks(sc, valid, blk, n_cand, n_sel):
    rank = jnp.zeros(sc.shape, F32)
    for j in range(n_cand):
        rj = sc[j:j + 1, :]
        beats = (rj > sc) | ((rj == sc) & (blk > j))
        rank = rank + jnp.where(beats, 1.0, 0.0)
    return jnp.where(valid & (rank < n_sel), 1.0, 0.0)


def _nsa_attn_kernel(q_ref, gt_ref, kc_ref, vc_ref, ks_ref, vs_ref, kw_ref, vw_ref, ovt_ref, et_ref,
                     o_ref, ocmp_ref, msk_ref, wmsk_ref, qt_ref, vst_ref, vwt_ref, m_ref, l_ref, acc_ref,
                     *, tq, t_len, nc, n_sel):
    qi = pl.program_id(1)
    q0 = qi * tq
    scale = NSA_DH ** -0.5
    kt = 256
    ns = t_len // SLC_BLOCK
    n_pairs = NSA_HEADS // 2

    @pl.when(qi == 0)
    def _():
        for jt in range(t_len // kt):
            vst_ref[jt] = vs_ref[jt * kt:(jt + 1) * kt, :].T.astype(BF16)
            vwt_ref[jt] = vw_ref[jt * kt:(jt + 1) * kt, :].T.astype(BF16)

    hpg = NSA_HEADS // NSA_G
    pairs_g = n_pairs // NSA_G
    wq = hpg * tq
    zeros_half = jnp.zeros((NSA_DH, tq), F32)
    for g in range(NSA_G):
        pieces = []
        for j in range(pairs_g):
            slab_t = q_ref[g * pairs_g + j].astype(F32).T * scale
            for part in (slab_t[:NSA_DH], slab_t[NSA_DH:]):
                pieces.append(jnp.concatenate([part, zeros_half] if g == 0 else [zeros_half, part], axis=0))
        qt_ref[g] = jnp.concatenate(pieces, axis=1).astype(BF16)

    kc = kc_ref[...]
    vc_t = vc_ref[...].T
    ncp = kc.shape[0]
    cidx = lax.broadcasted_iota(jnp.int32, (ncp, wq), 0)
    qpos_c = q0 + (lax.broadcasted_iota(jnp.int32, (ncp, wq), 1) & (tq - 1))
    cmask = (cidx < nc) & (cidx * CMP_STRIDE + (CMP_LEN - 1) <= qpos_c)
    imp = []
    for g in range(NSA_G):
        p = _softmax_tile_t(_dot(kc, qt_ref[g]), cmask)
        ocmp_ref[g] = _dot(vc_t, p)
        tot = p[:, 0:tq]
        for hh in range(1, hpg):
            tot = tot + p[:, hh * tq:(hh + 1) * tq]
        imp.append(tot)

    nsp = ovt_ref.shape[0]
    blk = lax.broadcasted_iota(jnp.int32, (ns, tq), 0)
    cur = (q0 + lax.broadcasted_iota(jnp.int32, (ns, tq), 1)) // SLC_BLOCK
    forced = (blk == 0) | (blk == cur) | (blk == cur - 1)
    valid = blk <= cur
    krow1 = lax.broadcasted_iota(jnp.int32, (kt, tq), 0)
    qpos1 = q0 + lax.broadcasted_iota(jnp.int32, (kt, tq), 1)
    for g in range(NSA_G):
        sc = _dot_f32(ovt_ref[...], imp[g])[:ns]
        sc = jnp.where(valid, sc + jnp.where(forced, FORCE_BONUS, 0.0), NEG)
        sel_t = _select_blocks(sc, valid, blk, ns, n_sel)
        if nsp > ns:
            sel_t = jnp.concatenate([sel_t, jnp.zeros((nsp - ns, tq), F32)], axis=0)
        for jt in range(t_len // kt):
            hit = _dot(et_ref[jt * kt:(jt + 1) * kt, :], sel_t)
            msk_ref[g, jt] = jnp.where((jt * kt + krow1 <= qpos1) & (hit > 0.5), 0.0, NEG)
    win_tiles = list(range(-WINDOW, tq, kt))[::-1]
    for wi, d in enumerate(win_tiles):
        kpos = q0 + d + krow1
        wmsk_ref[wi] = jnp.where((kpos <= qpos1) & (kpos > qpos1 - WINDOW), 0.0, NEG)

    gt_t = gt_ref[...].T
    for g in range(NSA_G):
        q_g = qt_ref[g]
        _online_init(m_ref, l_ref, acc_ref)

        def slc_body(jt, carry):
            start = pl.multiple_of(jt * kt, kt)
            bias = jnp.concatenate([msk_ref[g, jt]] * hpg, axis=1)
            _online_update_t(_dot(ks_ref[pl.ds(start, kt), :], q_g), bias, vst_ref[jt], m_ref, l_ref, acc_ref)
            return carry

        lax.fori_loop(0, (q0 + tq) // kt, slc_body, 0)
        o_slc = _online_result(l_ref, acc_ref)

        _online_init(m_ref, l_ref, acc_ref)
        for wi, d in enumerate(win_tiles):
            @pl.when(q0 + d >= 0)
            def _():
                start = pl.multiple_of(q0 + d, kt)
                bias = jnp.concatenate([wmsk_ref[wi]] * hpg, axis=1)
                _online_update_t(_dot(kw_ref[pl.ds(start, kt), :], q_g), bias, vwt_ref[(q0 + d) // kt],
                                 m_ref, l_ref, acc_ref)
        o_win = _online_result(l_ref, acc_ref)

        def gate(c):
            row = c * NSA_HEADS + g * hpg
            return jnp.concatenate([gt_t[row + hh:row + hh + 1, :] for hh in range(hpg)], axis=1)

        o = gate(0) * ocmp_ref[g] + gate(1) * o_slc + gate(2) * o_win
        o = o[g * NSA_DH:(g + 1) * NSA_DH]
        for j in range(pairs_g):
            o_ref[g * pairs_g + j] = jnp.concatenate(
                [o[:, 2 * j * tq:(2 * j + 1) * tq], o[:, (2 * j + 1) * tq:(2 * j + 2) * tq]],
                axis=0).T.astype(o_ref.dtype)


def _nsa_attn(q_slab, gates, kc, vc, slc_rows, win_rows, tq=256):
    b, n_pairs, t, _ = q_slab.shape
    tq = min(tq, t)
    n_seg = kc.shape[1]
    nc = n_seg - 1
    ns = t // SLC_BLOCK
    nsp = LANES
    ovt = _overlap_t(nc, ns, n_seg, nsp)
    e = jnp.asarray((np.arange(t)[:, None] // SLC_BLOCK == np.arange(nsp)[None, :]).astype(np.float32), BF16)
    kern = functools.partial(_nsa_attn_kernel, tq=tq, t_len=t, nc=nc, n_sel=min(SLC_TOPN, ns))
    wq = (NSA_HEADS // NSA_G) * tq
    seq = lambda c: pl.BlockSpec((None, t, LANES), lambda i, j: (i, 0, c))
    return pl.pallas_call(
        kern, grid=(b, t // tq),
        in_specs=[pl.BlockSpec((None, n_pairs, tq, LANES), lambda i, j: (i, 0, j, 0)),
                  pl.BlockSpec((None, tq, LANES), lambda i, j: (i, j, 0)),
                  pl.BlockSpec((None, n_seg, LANES), lambda i, j: (i, 0, 0)),
                  pl.BlockSpec((None, n_seg, LANES), lambda i, j: (i, 0, 0)),
                  seq(0), seq(1), seq(0), seq(1),
                  pl.BlockSpec(ovt.shape, lambda i, j: (0, 0)),
                  pl.BlockSpec(e.shape, lambda i, j: (0, 0))],
        out_specs=pl.BlockSpec((None, n_pairs, tq, LANES), lambda i, j: (i, 0, j, 0)),
        out_shape=jax.ShapeDtypeStruct((b, n_pairs, t, LANES), BF16),
        scratch_shapes=[pltpu.VMEM((NSA_G, LANES, wq), F32),
                        pltpu.VMEM((NSA_G, t // 256, 256, tq), F32),
                        pltpu.VMEM((len(range(-WINDOW, tq, 256)), 256, tq), F32),
                        pltpu.VMEM((NSA_G, LANES, wq), BF16),
                        pltpu.VMEM((t // 256, LANES, 256), BF16), pltpu.VMEM((t // 256, LANES, 256), BF16),
                        pltpu.VMEM((1, wq), F32), pltpu.VMEM((1, wq), F32), pltpu.VMEM((LANES, wq), F32)],
        compiler_params=_cparams('arbitrary', 'arbitrary'), name='nsa_attn')(
            q_slab, gates, kc, vc, slc_rows, slc_rows, win_rows, win_rows, ovt, e)


def _mlstm_kernel(*refs, L, t_real, has_state, pad):
    it = iter(refs)
    z_ref, gc_ref, gr_ref, bc_ref, br_ref = next(it), next(it), next(it), next(it), next(it)
    if has_state:
        c0_ref, n0_ref, m0_ref = next(it), next(it), next(it)
    h_ref, c_out, n_out, m_out = next(it), next(it), next(it), next(it)
    c_scr, n_scr, m_scr = next(it), next(it), next(it)
    if pad:
        zp_ref, gcp_ref, grp_ref = next(it), next(it), next(it)
    ci = pl.program_id(1)
    nchunks = pl.num_programs(1)

    @pl.when(ci == 0)
    def _():
        if has_state:
            c_scr[...] = c0_ref[...]
            n_scr[...] = n0_ref[...]
            m_scr[...] = m0_ref[...]
        else:
            c_scr[...] = jnp.zeros(c_scr.shape, F32)
            n_scr[...] = jnp.zeros(n_scr.shape, F32)
            m_scr[...] = jnp.zeros(m_scr.shape, F32)

    if pad:
        @pl.when((pl.program_id(0) == 0) & (ci == 0))
        def _():
            zp_ref[...] = jnp.zeros(zp_ref.shape, F32)
            gcp_ref[...] = jnp.zeros(gcp_ref.shape, F32)
            grp_ref[...] = jnp.zeros(grp_ref.shape, F32)

        zp_ref[:, 0:t_real, :] = z_ref[...]
        gcp_ref[0:t_real, :] = gc_ref[...]
        grp_ref[:, 0:t_real] = gr_ref[...]
        z_ref, gc_ref, gr_ref = zp_ref, gcp_ref, grp_ref

    H = ML_HEADS
    gcol = gc_ref[...] + bc_ref[...]
    grow = gr_ref[...] + br_ref[...]
    lf_c = _log_sigmoid(gcol)
    lf_r = _log_sigmoid(grow)
    ig_c, ig_r = gcol, grow
    if t_real < L:
        rv = lax.broadcasted_iota(jnp.int32, gcol.shape, 0) < t_real
        lv = lax.broadcasted_iota(jnp.int32, grow.shape, 1) < t_real
        lf_c, ig_c = jnp.where(rv, lf_c, 0.0), jnp.where(rv, ig_c, NEG)
        lf_r, ig_r = jnp.where(lv, lf_r, 0.0), jnp.where(lv, ig_r, NEG)
    bcum_c = _cumsum(lf_c, 0)
    bcum_r = _cumsum(lf_r, 1)
    tri = lax.broadcasted_iota(jnp.int32, (L, L), 0) >= lax.broadcasted_iota(jnp.int32, (L, L), 1)
    for h in range(H):
        q = z_ref[h]
        k = z_ref[H + h] * (ML_DK ** -0.5)
        v = jnp.concatenate([z_ref[2 * H + 2 * h], z_ref[2 * H + 2 * h + 1]], axis=1)
        og = jnp.concatenate([z_ref[4 * H + 2 * h], z_ref[4 * H + 2 * h + 1]], axis=1)
        bc_t = bcum_c[:, H + h:H + h + 1]
        bc_s = bcum_r[H + h:H + h + 1, :]
        ig_s = ig_r[h:h + 1, :]
        ig_t = ig_c[:, h:h + 1]
        m_old = m_scr[h:h + 1, 0:1]
        dmat = jnp.where(tri, bc_t - bc_s + ig_s, NEG)
        inter = bc_t + m_old
        mt = jnp.maximum(jnp.max(dmat, axis=1, keepdims=True), inter)
        w_intra = jnp.exp(dmat - mt)
        w_state = jnp.exp(inter - mt)
        sc = _dot_nt(q, k) * w_intra
        c_st = c_scr[h]
        n_st = n_scr[h:h + 1, :]
        num = _dot(sc, v) + w_state * _dot_nt(q, c_st)
        den = jnp.sum(sc, axis=1, keepdims=True) + w_state * jnp.sum(q * n_st, axis=1, keepdims=True)
        hh = num / jnp.maximum(jnp.abs(den), jnp.exp(-mt)) * og
        h_ref[2 * h] = hh[0:h_ref.shape[1], 0:LANES].astype(h_ref.dtype)
        h_ref[2 * h + 1] = hh[0:h_ref.shape[1], LANES:].astype(h_ref.dtype)
        bl = bc_s[:, L - 1:L]
        dl_r = bl - bc_s + ig_s
        dl_t = bl - bc_t + ig_t
        m_new = jnp.maximum(bl + m_old, jnp.max(dl_r, axis=1, keepdims=True))
        ws_t = jnp.exp(dl_t - m_new)
        wc = jnp.exp(bl + m_old - m_new)
        c_scr[h] = wc * c_st + _dot((v * ws_t).T, k)
        n_scr[h:h + 1, :] = wc * n_st + jnp.sum(ws_t * k, axis=0, keepdims=True)
        m_scr[h:h + 1, :] = jnp.broadcast_to(m_new, (1, LANES))

    @pl.when(ci == nchunks - 1)
    def _():
        c_out[...] = c_scr[...]
        n_out[...] = n_scr[...]
        m_out[...] = m_scr[...]


def _mlstm(z_slab, gates_col, gates_row, gate_b, state=None, L=256):
    b, ns, t, _ = z_slab.shape
    pad = t < 8
    L = 128 if pad else min(L, t)
    lr = t if pad else L
    nch = 1 if pad else t // L
    H = ML_HEADS
    bcol = jnp.zeros((1, LANES), F32).at[0, :2 * H].set(gate_b.reshape(-1).astype(F32))
    brow = gate_b.reshape(2 * H, 1).astype(F32)
    specs = [pl.BlockSpec((None, ns, lr, LANES), lambda i, c: (i, 0, c, 0)),
             pl.BlockSpec((None, lr, LANES), lambda i, c: (i, c, 0)),
             pl.BlockSpec((None, 2 * H, lr), lambda i, c: (i, 0, c)),
             pl.BlockSpec((1, LANES), lambda i, c: (0, 0)),
             pl.BlockSpec((2 * H, 1), lambda i, c: (0, 0))]
    args = [z_slab, gates_col, gates_row, bcol, brow]
    if state is not None:
        c0, n0, m0 = state
        n0p = jnp.zeros((b, 8, LANES), F32).at[:, :H].set(n0.astype(F32))
        m0p = jnp.zeros((b, 8, LANES), F32).at[:, :H].set(jnp.broadcast_to(m0.astype(F32)[..., None], (b, H, LANES)))
        specs += [pl.BlockSpec((None, H, ML_DV, ML_DK), lambda i, c: (i, 0, 0, 0)),
                  pl.BlockSpec((None, 8, LANES), lambda i, c: (i, 0, 0)),
                  pl.BlockSpec((None, 8, LANES), lambda i, c: (i, 0, 0))]
        args += [c0.astype(F32), n0p, m0p]
    scratch = [pltpu.VMEM((H, ML_DV, ML_DK), F32), pltpu.VMEM((8, LANES), F32), pltpu.VMEM((8, LANES), F32)]
    if pad:
        scratch += [pltpu.VMEM((ns, L, LANES), F32), pltpu.VMEM((L, LANES), F32), pltpu.VMEM((2 * H, L), F32)]
    kern = functools.partial(_mlstm_kernel, L=L, t_real=t if pad else L, has_state=state is not None, pad=pad)
    h_slab, c_f, n_f, m_f = pl.pallas_call(
        kern, grid=(b, nch), in_specs=specs,
        out_specs=[pl.BlockSpec((None, 2 * H, lr, LANES), lambda i, c: (i, 0, c, 0)),
                   pl.BlockSpec((None, H, ML_DV, ML_DK), lambda i, c: (i, 0, 0, 0)),
                   pl.BlockSpec((None, 8, LANES), lambda i, c: (i, 0, 0)),
                   pl.BlockSpec((None, 8, LANES), lambda i, c: (i, 0, 0))],
        out_shape=[jax.ShapeDtypeStruct((b, 2 * H, t, LANES), BF16),
                   jax.ShapeDtypeStruct((b, H, ML_DV, ML_DK), F32),
                   jax.ShapeDtypeStruct((b, 8, LANES), F32), jax.ShapeDtypeStruct((b, 8, LANES), F32)],
        scratch_shapes=scratch, compiler_params=_cparams('arbitrary', 'arbitrary'), name='mlstm')(*args)
    return h_slab, c_f, n_f[:, :H], m_f[:, :H, 0]


def _hgrn_levels(L):
    n_lev = int(math.log2(L))
    t = np.arange(L)
    pall = np.zeros((n_lev * L, L), np.float32)
    lmask = np.zeros((n_lev, L, L), np.float32)
    for lev in range(n_lev):
        w = L >> lev
        mid = (t // w) * w + w // 2
        pall[lev * L + t, mid - 1] = 1.0
        same = (t[:, None] // w) == (t[None, :] // w)
        lmask[lev] = same & ((t[:, None] % w) >= w // 2) & ((t[None, :] % w) < w // 2)
    return jnp.asarray(pall, BF16), jnp.asarray(lmask)


def _hgrn_kernel(*refs, L, t_real, has_state, pad):
    it = iter(refs)
    z_ref, lf_ref, gn_ref, pall_ref, lmask_ref = next(it), next(it), next(it), next(it), next(it)
    if has_state:
        s0_ref = next(it)
    o_ref, s_out = next(it), next(it)
    st_scr = next(it)
    if pad:
        zp_ref = next(it)
    ci = pl.program_id(1)
    nchunks = pl.num_programs(1)
    H = HG_HEADS
    n_lev = lmask_ref.shape[0]

    @pl.when(ci == 0)
    def _():
        for h in range(H):
            st_scr[h] = s0_ref[h].T if has_state else jnp.zeros((HG_DV, HG_DK), F32)

    if pad:
        @pl.when((pl.program_id(0) == 0) & (ci == 0))
        def _():
            zp_ref[...] = jnp.zeros(zp_ref.shape, F32)

        zp_ref[:, 0:t_real, :] = z_ref[...]
        z_ref = zp_ref

    rows = lax.broadcasted_iota(jnp.int32, (L, LANES), 0)
    eye = lax.broadcasted_iota(jnp.int32, (L, L), 0) == lax.broadcasted_iota(jnp.int32, (L, L), 1)
    gn = gn_ref[...]

    def head(h, carry):
        q = z_ref[h]
        zf = z_ref[H + h]
        v = z_ref[2 * H + h]
        gate = z_ref[3 * H + h]
        lower = lf_ref[pl.ds(h, 1), :]
        fb = lf_ref[pl.ds(H + h, 1), :]
        f = lower + (1.0 - lower) * _sigmoid(zf + fb)
        lf = jnp.log(f)
        k = 1.0 - f
        if t_real < L:
            lf = jnp.where(rows < t_real, lf, 0.0)
            k = jnp.where(rows < t_real, k, 0.0)
        bcum = _cumsum(lf, 0)
        lev0 = n_lev - max(1, (t_real - 1).bit_length())
        hi = bcum.astype(BF16)
        rem = bcum - hi.astype(F32)
        mid = rem.astype(BF16)
        low = (rem - mid.astype(F32)).astype(BF16)
        picked = jnp.dot(pall_ref[lev0 * L:, :], jnp.concatenate([hi, mid, low], axis=1),
                         preferred_element_type=F32)
        refs_all = picked[:, :HG_DK] + picked[:, HG_DK:2 * HG_DK] + picked[:, 2 * HG_DK:]
        att = jnp.where(eye, jnp.sum(q * k, axis=1, keepdims=True), 0.0)
        for lev in range(lev0, n_lev):
            r = refs_all[(lev - lev0) * L:(lev - lev0 + 1) * L]
            qt = q * jnp.exp(jnp.minimum(bcum - r, 0.0))
            kt = k * jnp.exp(jnp.minimum(r - bcum, 0.0))
            att = att + _dot_nt(qt, kt) * lmask_ref[lev]
        s_t = st_scr[h]
        o = _dot(att, v) + _dot_nt(q * jnp.exp(bcum), s_t)
        bl = bcum[L - 1:L, :]
        st_scr[h] = jnp.exp(bl) * s_t + _dot(v.T, k * jnp.exp(bl - bcum))
        on = _rms(o, gn) * _silu(gate)
        o_ref[h] = on[0:o_ref.shape[1]].astype(o_ref.dtype)
        return carry

    lax.fori_loop(0, H, head, 0, unroll=4)

    @pl.when(ci == nchunks - 1)
    def _():
        for h in range(H):
            s_out[h] = st_scr[h].T


def _hgrn(z_slab, lower, f_b, g_norm, state=None, L=128):
    b, ns, t, _ = z_slab.shape
    pad = t < 8
    L = L if pad else min(L, t)
    lr = t if pad else L
    nch = 1 if pad else t // L
    H = HG_HEADS
    pall, lmask = _hgrn_levels(L)
    lowfb = jnp.concatenate([lower.reshape(H, HG_DK), f_b.reshape(H, HG_DK)], axis=0).astype(F32)
    specs = [pl.BlockSpec((None, ns, lr, LANES), lambda i, c: (i, 0, c, 0)),
             pl.BlockSpec((2 * H, LANES), lambda i, c: (0, 0)),
             pl.BlockSpec((1, LANES), lambda i, c: (0, 0)),
             pl.BlockSpec(pall.shape, lambda i, c: (0, 0)),
             pl.BlockSpec(lmask.shape, lambda i, c: (0, 0, 0))]
    args = [z_slab, lowfb, g_norm.reshape(1, HG_DV).astype(F32), pall, lmask]
    if state is not None:
        specs.append(pl.BlockSpec((None, H, HG_DK, HG_DV), lambda i, c: (i, 0, 0, 0)))
        args.append(state.astype(F32))
    scratch = [pltpu.VMEM((H, HG_DV, HG_DK), F32)]
    if pad:
        scratch.append(pltpu.VMEM((ns, L, LANES), F32))
    kern = functools.partial(_hgrn_kernel, L=L, t_real=t if pad else L, has_state=state is not None, pad=pad)
    return pl.pallas_call(
        kern, grid=(b, nch), in_specs=specs,
        out_specs=[pl.BlockSpec((None, H, lr, LANES), lambda i, c: (i, 0, c, 0)),
                   pl.BlockSpec((None, H, HG_DK, HG_DV), lambda i, c: (i, 0, 0, 0))],
        out_shape=[jax.ShapeDtypeStruct((b, H, t, LANES), BF16),
                   jax.ShapeDtypeStruct((b, H, HG_DK, HG_DV), F32)],
        scratch_shapes=scratch, compiler_params=_cparams('arbitrary', 'arbitrary'), name='hgrn')(*args)


def _mla_attn_kernel(ql_ref, qr_ref, lc_ref, lr_ref, o_ref, lct_ref, qlt_ref, qrt_ref, m_ref, l_ref, acc_ref,
                     *, tq, t_len):
    qi = pl.program_id(1)
    q0 = qi * tq
    kt = lct_ref.shape[2]
    scale = (MLA_NOPE + MLA_ROPE) ** -0.5
    mq = MLA_HEADS * tq
    per_slab = LANES // MLA_ROPE

    @pl.when(qi == 0)
    def _():
        for jt in range(t_len // kt):
            lct_ref[jt] = lc_ref[jt * kt:(jt + 1) * kt, :].T.astype(BF16)

    zeros_r = jnp.zeros((LANES - MLA_ROPE, tq), F32)
    for h in range(MLA_HEADS):
        qlt_ref[:, h * tq:(h + 1) * tq] = ql_ref[:, h * MLA_KV_LORA:(h + 1) * MLA_KV_LORA].astype(F32).T.astype(BF16)
        if h % per_slab == 0:
            slab_t = qr_ref[:, (h // per_slab) * LANES:(h // per_slab + 1) * LANES].astype(F32).T
        rope_t = slab_t[(h % per_slab) * MLA_ROPE:(h % per_slab + 1) * MLA_ROPE]
        qrt_ref[:, h * tq:(h + 1) * tq] = jnp.concatenate([rope_t, zeros_r], axis=0).astype(BF16)
    q_lat = qlt_ref[...]
    q_rope = qrt_ref[...]
    _online_init(m_ref, l_ref, acc_ref)

    def tile(j, masked):
        start = pl.multiple_of(j * kt, kt)
        st = (_dot(lc_ref[pl.ds(start, kt), :], q_lat) + _dot(lr_ref[pl.ds(start, kt), :], q_rope)) * scale
        if masked:
            krow = lax.broadcasted_iota(jnp.int32, (kt, mq), 0)
            qpos = q0 + (lax.broadcasted_iota(jnp.int32, (kt, mq), 1) & (tq - 1))
            bias = jnp.where(start + krow <= qpos, 0.0, NEG)
        else:
            bias = None
        _online_update_t(st, bias, lct_ref[j], m_ref, l_ref, acc_ref)

    def body(j, carry):
        tile(j, False)
        return carry

    n_full = q0 // kt
    lax.fori_loop(0, n_full, body, 0)
    tile(n_full, True)
    o = _online_result(l_ref, acc_ref)
    for h in range(MLA_HEADS):
        o_ref[:, h * MLA_KV_LORA:(h + 1) * MLA_KV_LORA] = o[:, h * tq:(h + 1) * tq].T.astype(o_ref.dtype)


def _mla_attn(q_lat, q_rope, lat, b, t, tq=128):
    tq = min(tq, t)
    kt = min(256, t)
    wl = MLA_HEADS * MLA_KV_LORA
    wr = MLA_HEADS * MLA_ROPE
    lat3 = lat.reshape(b, t, lat.shape[-1])
    c_blk = MLA_Q_LORA // MLA_KV_LORA
    r_blk = (MLA_Q_LORA + MLA_KV_LORA) // LANES
    return pl.pallas_call(
        functools.partial(_mla_attn_kernel, tq=tq, t_len=t), grid=(b, t // tq),
        in_specs=[pl.BlockSpec((None, tq, wl), lambda i, j: (i, j, 0)),
                  pl.BlockSpec((None, tq, wr), lambda i, j: (i, j, 0)),
                  pl.BlockSpec((None, t, MLA_KV_LORA), lambda i, j: (i, 0, c_blk)),
                  pl.BlockSpec((None, t, LANES), lambda i, j: (i, 0, r_blk))],
        out_specs=pl.BlockSpec((None, tq, wl), lambda i, j: (i, j, 0)),
        out_shape=jax.ShapeDtypeStruct((b, t, wl), BF16),
        scratch_shapes=[pltpu.VMEM((t // kt, MLA_KV_LORA, kt), BF16),
                        pltpu.VMEM((MLA_KV_LORA, MLA_HEADS * tq), BF16), pltpu.VMEM((LANES, MLA_HEADS * tq), BF16),
                        pltpu.VMEM((1, MLA_HEADS * tq), F32), pltpu.VMEM((1, MLA_HEADS * tq), F32),
                        pltpu.VMEM((MLA_KV_LORA, MLA_HEADS * tq), F32)],
        compiler_params=_cparams('arbitrary', 'arbitrary'), name='mla_attn')(
            q_lat.reshape(b, t, wl), q_rope.reshape(b, t, wr), lat3, lat3).reshape(b * t, wl)


def _pad_cols(w, n):
    return jnp.pad(w, ((0, 0), (0, n - w.shape[1])))


def _prep_nsa(w_in, pe, w1, w2, w_out):
    nq = NSA_HEADS * NSA_DH
    kvw = 2 * NSA_G * NSA_DH
    w2 = w2.astype(BF16)
    z = jnp.zeros_like(w2)
    w2p = jnp.stack([jnp.concatenate([w2, z], axis=-1), jnp.concatenate([z, w2], axis=-1)], axis=1)
    return dict(q=w_in[:, :nq].astype(BF16), kv=w_in[:, nq:nq + 3 * kvw].astype(BF16),
                g=_pad_cols(w_in[:, nq + 3 * kvw:], LANES).astype(BF16),
                pe=pe.astype(F32).reshape(2, CMP_LEN // CMP_STRIDE, CMP_STRIDE * NSA_DH),
                w1=w1.astype(BF16), w2p=w2p, out=w_out.astype(BF16))


def _prep_ml(w_in, gate_b, w_out):
    a = 2 * ML_HEADS * ML_DK + ML_HEADS * ML_DV
    main = jnp.concatenate([w_in[:, :a], w_in[:, a + 2 * ML_HEADS:]], axis=1)
    return dict(main=main.astype(BF16), gate=_pad_cols(w_in[:, a:a + 2 * ML_HEADS], LANES).astype(BF16),
                gate_b=gate_b, out=w_out.astype(BF16))


def _prep_mla(w_in, q_norm, kv_norm, w_uq, w_uk, w_uv, w_out):
    uq = w_uq.reshape(MLA_Q_LORA, MLA_HEADS, MLA_NOPE + MLA_ROPE)
    return dict(w_in=_pad_cols(w_in, 7 * LANES).astype(BF16),
                q_norm=q_norm.reshape(1, -1).astype(F32), kv_norm=kv_norm.reshape(1, -1).astype(F32),
                uq_nope=uq[:, :, :MLA_NOPE].reshape(MLA_Q_LORA, -1).astype(BF16),
                uq_rope=uq[:, :, MLA_NOPE:].reshape(MLA_Q_LORA, -1).astype(BF16),
                ukt=jnp.transpose(w_uk, (1, 2, 0)).astype(BF16),
                uv=jnp.transpose(w_uv, (1, 0, 2)).astype(BF16), out=w_out.astype(BF16))


def _sig_tile2(z, j):
    return jnp.where(j == 2, _sigmoid(z), z)


def _rope64_epi(z, j, c, s):
    return _rope_tile(z, c, s, NSA_DH // 2)


def _rope_kv_epi(z, j, c, s):
    return jnp.concatenate([_rope_tile(z[:, :LANES], c, s, NSA_DH // 2), z[:, LANES:]], axis=1)


def _rope32_epi(z, j, c, s):
    return _rope_tile(z, c, s, MLA_ROPE // 2)


def _sigmoid_epi(z, j):
    return _sigmoid(z)


def _mla_in_epi(z, j, qn, kvn, c, s):
    a, bnd = MLA_Q_LORA, MLA_Q_LORA + MLA_KV_LORA
    return jnp.concatenate([_rms(z[:, :a], qn), _rms(z[:, a:bnd], kvn),
                            _rope_tile(z[:, bnd:], c, s, MLA_ROPE // 2)], axis=1)


def _const_aux(a):
    return (a, a.shape, lambda i, j: (0,) * a.ndim)


def _nsa_fresh(h, g, w, b, t, tabs):
    cos, sin = tabs[NSA_DH // 2]
    tm = min(1024, t)
    rope = _rope_aux(cos, sin, tm, t // tm)
    q = _mm(h, w['q'], g=g, epi=_rope64_epi, aux=rope, out_dtype=BF16, layout='slab', seq=t)
    kv = _mm(h, w['kv'], g=g, epi=_rope_kv_epi, aux=rope, layout='tiles', seq=t, tn=256)
    gates = _mm(h, w['g'], g=g, epi=_sigmoid_epi, seq=t)
    rows = [kv[i].reshape(b, t, 256) for i in range(3)]
    kc, vc = _nsa_compress(rows[0], w['pe'], w['w1'], w['w2p'])
    o = _nsa_attn(q, gates.reshape(b, t, LANES), kc, vc, rows[1], rows[2])
    h = _mm(o, w['out'], res=h, x_slab_seq=t)
    return h, rows


def _ml_fresh(h, g, w, b, t):
    z = _mm(h, w['main'], g=g, epi=_sig_tile2, layout='slab', seq=t)
    gc = _mm(h, w['gate'], g=g, seq=t).reshape(b, t, LANES)
    gr = jnp.swapaxes(gc[:, :, :2 * ML_HEADS], 1, 2)
    hs, c_f, n_f, m_f = _mlstm(z, gc, gr, w['gate_b'])
    return _mm(hs, w['out'], res=h, x_slab_seq=t), (c_f, n_f, m_f)


def _mla_fresh(h, g, w, b, t, tabs):
    cos, sin = tabs[MLA_ROPE // 2]
    tm = min(1024, t)
    rope = _rope_aux(cos, sin, tm, t // tm)
    lat = _mm(h, w['w_in'], g=g, epi=_mla_in_epi,
              aux=[_const_aux(w['q_norm']), _const_aux(w['kv_norm'])] + rope, seq=t)
    qn = _mm(lat, w['uq_nope'], x_cols=(MLA_Q_LORA, 0), out_dtype=BF16, seq=t)
    qr = _mm(lat, w['uq_rope'], x_cols=(MLA_Q_LORA, 0), epi=_rope32_epi, aux=rope, out_dtype=BF16, seq=t)
    ql = _headmm(qn, w['ukt'])
    ol = _mla_attn(ql, qr, lat, b, t)
    o = _headmm(ol, w['uv'])
    h = _mm(o, w['out'], res=h)
    new_lat = lat[:, MLA_Q_LORA:MLA_Q_LORA + MLA_KV_LORA + MLA_ROPE].reshape(b, t, -1)
    return h, new_lat


def _hg_fresh(h, g, w_in, lower, f_b, g_norm, w_out, b, t):
    z = _mm(h, w_in, g=g, layout='slab', seq=t)
    os_, s_f = _hgrn(z, lower, f_b, g_norm)
    return _mm(os_, w_out, res=h, x_slab_seq=t), s_f


PAGES_PER_STEP = 16


def _page_specs(width, n):
    return [pl.BlockSpec((None, width, PAGE), functools.partial(
        lambda i, s, pt, k: (pt[i, s * n + k], 0, 0), k=k)) for k in range(n)]


def _pages_feature_major(pool):
    nd = pool.ndim
    return jnp.transpose(pool, (0,) + tuple(range(2, nd)) + (1,)).reshape(pool.shape[0], -1, pool.shape[1])


def _stack8(x):
    return jnp.concatenate([x] * (NSA_HEADS // NSA_G), axis=0)


def _nsa_cmp_past_kernel(pt_ref, *refs, n_pg, t, start, nc):
    pages = refs[:n_pg]
    q_ref, pe_ref, w1_ref, w2p_ref, ocmp_ref, imp_ref, k_scr, v_scr = refs[n_pg:]
    s_i = pl.program_id(1)
    for k in range(n_pg):
        off = pl.multiple_of((s_i * n_pg + k) * PAGE, PAGE)
        rows = pages[k][...].T
        k_scr[pl.ds(off, PAGE), :] = rows[:, 0:LANES]
        v_scr[pl.ds(off, PAGE), :] = rows[:, LANES:2 * LANES]

    @pl.when(s_i == pl.num_programs(1) - 1)
    def _():
        n_seg = k_scr.shape[0] // CMP_STRIDE
        kc, vc = _compress_rows(k_scr, v_scr, n_seg, pe_ref, w1_ref, w2p_ref)
        q = q_ref[...]
        r_n = q.shape[0]
        cidx = lax.broadcasted_iota(jnp.int32, (r_n, n_seg), 1)
        qpos = start + lax.rem(lax.broadcasted_iota(jnp.int32, (r_n, n_seg), 0), t)
        mask = (cidx < nc) & (cidx * CMP_STRIDE + (CMP_LEN - 1) <= qpos)
        p = _softmax_tile(_dot_nt(q, kc) * (NSA_DH ** -0.5), mask)
        ocmp_ref[...] = _dot(p, vc)
        gt = NSA_G * t
        imp = p[0:gt]
        for hh in range(1, r_n // gt):
            imp = imp + p[hh * gt:(hh + 1) * gt]
        imp_ref[...] = imp


def _nsa_cmp_past(page_table, cache, q_ext, w, t, start):
    b, npg = page_table.shape
    n_pg = min(PAGES_PER_STEP, npg)
    r_n = q_ext.shape[1]
    n_seg = npg * PAGE // CMP_STRIDE
    nc = (start + t) // CMP_STRIDE - CMP_LEN // CMP_STRIDE + 1
    full = lambda a: pl.BlockSpec(a.shape, lambda i, s, pt: (0,) * a.ndim)
    gs = pltpu.PrefetchScalarGridSpec(
        num_scalar_prefetch=1, grid=(b, npg // n_pg),
        in_specs=_page_specs(256, n_pg) + [pl.BlockSpec((None, r_n, LANES), lambda i, s, pt: (i, 0, 0)),
                                           full(w['pe']), full(w['w1']), full(w['w2p'])],
        out_specs=[pl.BlockSpec((None, r_n, LANES), lambda i, s, pt: (i, 0, 0)),
                   pl.BlockSpec((None, NSA_G * t, n_seg), lambda i, s, pt: (i, 0, 0))],
        scratch_shapes=[pltpu.VMEM((npg * PAGE, LANES), F32), pltpu.VMEM((npg * PAGE, LANES), F32)])
    return pl.pallas_call(
        functools.partial(_nsa_cmp_past_kernel, n_pg=n_pg, t=t, start=start, nc=nc), grid_spec=gs,
        out_shape=[jax.ShapeDtypeStruct((b, r_n, LANES), F32),
                   jax.ShapeDtypeStruct((b, NSA_G * t, n_seg), F32)],
        compiler_params=_cparams('arbitrary', 'arbitrary'), name='nsa_cmp_past')(
            page_table, *([cache] * n_pg), q_ext, w['pe'], w['w1'], w['w2p']), nc


def _nsa_select_kernel(imp_ref, ovt_ref, pos_ref, sel_ref, sc_scr, rank_scr, *, ns, n_sel):
    shape = sc_scr.shape
    blk = lax.broadcasted_iota(jnp.int32, shape, 0)
    cur = pos_ref[...] // SLC_BLOCK
    forced = (blk == 0) | (blk == cur) | (blk == cur - 1)
    valid = blk <= cur
    sc = _dot_nt_f32(ovt_ref[...], imp_ref[...])
    sc_scr[...] = jnp.where(valid, sc + jnp.where(forced, FORCE_BONUS, 0.0), NEG)
    rank_scr[...] = jnp.zeros(shape, F32)

    def body(j, carry):
        sc_all = sc_scr[...]
        rj = sc_scr[pl.ds(j, 1), :]
        beats = (rj > sc_all) | ((rj == sc_all) & (blk > j))
        rank_scr[...] += jnp.where(beats, 1.0, 0.0)
        return carry

    lax.fori_loop(0, ns, body, 0)
    sel_ref[...] = jnp.where(valid & (rank_scr[...] < n_sel), 1.0, 0.0)


def _nsa_select_past(imp, nc, t, start):
    rows, n_seg = imp.shape
    ns = -(-(start + t) // SLC_BLOCK)
    nsp = -(-ns // 8) * 8
    ovt = _overlap_t(nc, ns, n_seg, nsp)
    pos = (start + jnp.arange(rows, dtype=jnp.int32) % t).reshape(1, rows)
    return pl.pallas_call(
        functools.partial(_nsa_select_kernel, ns=ns, n_sel=min(SLC_TOPN, ns)),
        out_shape=jax.ShapeDtypeStruct((nsp, rows), F32),
        scratch_shapes=[pltpu.VMEM((nsp, rows), F32), pltpu.VMEM((nsp, rows), F32)],
        compiler_params=pltpu.CompilerParams(vmem_limit_bytes=VMEM_LIMIT), name='nsa_select')(imp, ovt, pos)


def _nsa_slcwin_past_kernel(pt_ref, *refs, n_pg, t, wl):
    pages = refs[:n_pg]
    (msk_ref, q_ref, ns_ref, nw_ref, wb_ref, nf_ref, gt_ref, ocmp_ref, o_ref,
     m_ref, l_ref, acc_ref, oslc_scr, pad_scr) = refs[n_pg:]
    s_i = pl.program_id(1)
    scale = NSA_DH ** -0.5
    q = q_ref[...]
    r_n = q.shape[0]

    @pl.when(s_i == 0)
    def _():
        _online_init(m_ref, l_ref, acc_ref)

    s = jnp.concatenate([_dot(q, pg[0:LANES, :]) * scale for pg in pages], axis=1)
    v_t = jnp.concatenate([pg[LANES:2 * LANES, :] for pg in pages], axis=1)
    mask = _stack8(msk_ref[...].astype(F32)) > 0.5
    _online_update(s, mask, v_t, m_ref, l_ref, acc_ref, v_t=True)

    @pl.when(s_i == pl.num_programs(1) - 1)
    def _():
        col = lax.broadcasted_iota(jnp.int32, (r_n, PAGE), 1)
        tq = lax.rem(lax.broadcasted_iota(jnp.int32, (r_n, PAGE), 0), t)
        new_ok = (col < t) & (col <= tq)

        def padded(ref):
            pad_scr[...] = jnp.zeros(pad_scr.shape, F32)
            pad_scr[0:t, :] = ref[...]
            return pad_scr[...]

        rows = padded(ns_ref)
        _online_update(_dot_nt(q, rows[:, 0:LANES]) * scale, new_ok & (_stack8(nf_ref[...]) > 0.5),
                       rows[:, LANES:], m_ref, l_ref, acc_ref)
        oslc_scr[...] = _online_result(l_ref, acc_ref)

        _online_init(m_ref, l_ref, acc_ref)
        colw = lax.broadcasted_iota(jnp.int32, (r_n, wl), 1)
        tqw = lax.rem(lax.broadcasted_iota(jnp.int32, (r_n, wl), 0), t)
        _online_update(_dot(q, wb_ref[0:LANES, :]) * scale, colw > tqw + (wl - WINDOW),
                       wb_ref[LANES:2 * LANES, :], m_ref, l_ref, acc_ref, v_t=True)
        rows = padded(nw_ref)
        _online_update(_dot_nt(q, rows[:, 0:LANES]) * scale, new_ok, rows[:, LANES:], m_ref, l_ref, acc_ref)
        o_win = _online_result(l_ref, acc_ref)
        gt = gt_ref[...]
        o_ref[...] = gt[:, 0:1] * ocmp_ref[...] + gt[:, 1:2] * oslc_scr[...] + gt[:, 2:3] * o_win


def _nsa_slcwin_past(page_table, cache, key_mask, q_ext, new_slc, new_win, win_buf, new_flag, gates_r, ocmp, t):
    b, npg = page_table.shape
    n_pg = min(PAGES_PER_STEP, npg)
    r_n = q_ext.shape[1]
    gtn = NSA_G * t
    wl = win_buf.shape[2]
    per_b = lambda shp: pl.BlockSpec((None,) + shp, lambda i, s, pt: (i,) + (0,) * len(shp))
    gs = pltpu.PrefetchScalarGridSpec(
        num_scalar_prefetch=1, grid=(b, npg // n_pg),
        in_specs=_page_specs(256, n_pg) + [
            pl.BlockSpec((None, gtn, n_pg * PAGE), lambda i, s, pt: (i, 0, s)),
            per_b((r_n, LANES)), per_b((t, 256)), per_b((t, 256)), per_b((256, wl)),
            per_b((gtn, LANES)), per_b((r_n, LANES)), per_b((r_n, LANES))],
        out_specs=per_b((r_n, LANES)),
        scratch_shapes=[pltpu.VMEM((r_n, 1), F32), pltpu.VMEM((r_n, 1), F32), pltpu.VMEM((r_n, LANES), F32),
                        pltpu.VMEM((r_n, LANES), F32), pltpu.VMEM((PAGE, 256), F32)])
    return pl.pallas_call(
        functools.partial(_nsa_slcwin_past_kernel, n_pg=n_pg, t=t, wl=wl), grid_spec=gs,
        out_shape=jax.ShapeDtypeStruct((b, r_n, LANES), F32),
        compiler_params=_cparams('arbitrary', 'arbitrary'), name='nsa_slcwin_past')(
            page_table, *([cache] * n_pg), key_mask, q_ext, new_slc, new_win, win_buf, new_flag, gates_r, ocmp)


def _mla_past_kernel(pt_ref, *refs, n_pg, t):
    pages = refs[:n_pg]
    ql_ref, qr_ref, new_ref, o_ref, m_ref, l_ref, acc_ref, pad_scr = refs[n_pg:]
    s_i = pl.program_id(1)
    scale = (MLA_NOPE + MLA_ROPE) ** -0.5
    ql = ql_ref[...]
    qr = qr_ref[...]
    r_n = ql.shape[0]

    @pl.when(s_i == 0)
    def _():
        _online_init(m_ref, l_ref, acc_ref)

    def scores(rows):
        kc = rows[:, 0:MLA_KV_LORA].astype(BF16)
        kr = rows[:, MLA_KV_LORA:MLA_KV_LORA + MLA_ROPE]
        return (_dot_nt(ql, kc) + _dot_nt(qr, kr)) * scale, kc

    kc_t = [pg[0:MLA_KV_LORA, :].astype(BF16) for pg in pages]
    s = [(_dot(ql, kc) + _dot(qr, pg[MLA_KV_LORA:MLA_KV_LORA + MLA_ROPE, :])) * scale
         for kc, pg in zip(kc_t, pages)]
    _online_update(jnp.concatenate(s, axis=1), None, jnp.concatenate(kc_t, axis=1),
                   m_ref, l_ref, acc_ref, v_t=True)

    @pl.when(s_i == pl.num_programs(1) - 1)
    def _():
        pad_scr[...] = jnp.zeros(pad_scr.shape, F32)
        pad_scr[0:t, :] = new_ref[...]
        s, kc = scores(pad_scr[...])
        col = lax.broadcasted_iota(jnp.int32, (r_n, PAGE), 1)
        tq = lax.rem(lax.broadcasted_iota(jnp.int32, (r_n, PAGE), 0), t)
        _online_update(s, (col < t) & (col <= tq), kc, m_ref, l_ref, acc_ref)
        o_ref[...] = _online_result(l_ref, acc_ref).astype(o_ref.dtype)


def _mla_past_attn(page_table, cache, ql, qr, new_lat, t):
    b, npg = page_table.shape
    n_pg = min(PAGES_PER_STEP, npg)
    r_n = ql.shape[1]
    width = cache.shape[1]
    per_b = lambda shp: pl.BlockSpec((None,) + shp, lambda i, s, pt: (i,) + (0,) * len(shp))
    gs = pltpu.PrefetchScalarGridSpec(
        num_scalar_prefetch=1, grid=(b, npg // n_pg),
        in_specs=_page_specs(width, n_pg) + [per_b((r_n, MLA_KV_LORA)), per_b((r_n, MLA_ROPE)), per_b((t, width))],
        out_specs=per_b((r_n, MLA_KV_LORA)),
        scratch_shapes=[pltpu.VMEM((r_n, 1), F32), pltpu.VMEM((r_n, 1), F32), pltpu.VMEM((r_n, MLA_KV_LORA), F32),
                        pltpu.VMEM((PAGE, width), F32)])
    return pl.pallas_call(
        functools.partial(_mla_past_kernel, n_pg=n_pg, t=t), grid_spec=gs,
        out_shape=jax.ShapeDtypeStruct((b, r_n, MLA_KV_LORA), BF16),
        compiler_params=_cparams('arbitrary', 'arbitrary'), name='mla_past')(
            page_table, *([cache] * n_pg), ql, qr, new_lat)


def _to_slab(z, b, t):
    return jnp.transpose(z.reshape(b, t, -1, LANES), (0, 2, 1, 3))


def _from_slab(s, b, t):
    return jnp.transpose(s, (0, 2, 1, 3)).reshape(b * t, -1)


def _nsa_past(h, g, w, b, t, tabs, start, past, occ):
    assert t < CMP_STRIDE and start % SLC_BLOCK == 0
    cos, sin = tabs[NSA_DH // 2]
    m = b * t
    rope = _rope_aux(cos, sin, min(1024, m), None)
    q = _mm(h, w['q'], g=g, epi=_rope64_epi, aux=rope, out_dtype=BF16)
    kv = _mm(h, w['kv'], g=g, epi=_rope_kv_epi, aux=rope, layout='tiles', tn=256)
    gates = _mm(h, w['g'], g=g, epi=_sigmoid_epi)
    rows = [kv[i].reshape(b, t, 256) for i in range(3)]
    hpg = NSA_HEADS // NSA_G
    r_n = hpg * NSA_G * t
    q5 = jnp.transpose(q.reshape(b, t, NSA_G, hpg, NSA_DH), (0, 3, 2, 1, 4))
    eye = jnp.eye(NSA_G, dtype=q5.dtype)
    q_ext = (q5[:, :, :, :, None, :] * eye[None, None, :, None, :, None]).reshape(b, r_n, NSA_G * NSA_DH)
    pt = past['page_table']
    width = 2 * NSA_G * NSA_DH
    cmp_cache = _pages_feature_major(past['nsa_cmp'][occ])
    slc_cache = _pages_feature_major(past['nsa_slc'][occ])
    win_buf = past['nsa_win'][occ].reshape(b, -1, width)
    win_buf_t = _pages_feature_major(past['nsa_win'][occ])
    (ocmp, imp), nc = _nsa_cmp_past(pt, cmp_cache, q_ext, w, t, start)
    gtn = NSA_G * t
    sel = _nsa_select_past(imp.reshape(b * gtn, -1), nc, t, start)
    sel = jnp.transpose(sel).reshape(b, gtn, -1)
    n_past_blk = start // SLC_BLOCK
    key_mask = jnp.repeat(sel[:, :, :n_past_blk], SLC_BLOCK, axis=-1).astype(BF16)
    new_flag = jnp.broadcast_to(sel[:, :, n_past_blk:n_past_blk + 1], (b, gtn, LANES))
    g5 = jnp.transpose(gates[:, :3 * NSA_HEADS].reshape(b, t, 3, NSA_G, hpg), (0, 4, 3, 1, 2))
    gates_r = jnp.pad(g5.reshape(b, r_n, 3), ((0, 0), (0, 0), (0, LANES - 3)))
    o = _nsa_slcwin_past(pt, slc_cache, key_mask, q_ext, rows[1], rows[2], win_buf_t, new_flag, gates_r, ocmp, t)
    o5 = o.reshape(b, hpg, NSA_G, t, NSA_G, NSA_DH)
    o = jnp.stack([o5[:, :, gi, :, gi, :] for gi in range(NSA_G)], axis=2)
    o = jnp.transpose(o, (0, 3, 2, 1, 4)).reshape(m, NSA_HEADS * NSA_DH).astype(BF16)
    h = _mm(o, w['out'], res=h)
    win = jnp.concatenate([win_buf, rows[2]], axis=1)[:, -win_buf.shape[1]:]
    return h, rows, win


def _ml_past(h, g, w, b, t, past, occ):
    z = _to_slab(_mm(h, w['main'], g=g, epi=_sig_tile2), b, t)
    gc = _mm(h, w['gate'], g=g).reshape(b, t, LANES)
    gr = jnp.swapaxes(gc[:, :, :2 * ML_HEADS], 1, 2)
    hs, c_f, n_f, m_f = _mlstm(z, gc, gr, w['gate_b'],
                               state=(past['ml_C'][occ], past['ml_n'][occ], past['ml_m'][occ]))
    return _mm(_from_slab(hs, b, t), w['out'], res=h), (c_f, n_f, m_f)


def _mla_past(h, g, w, b, t, tabs, past, occ):
    cos, sin = tabs[MLA_ROPE // 2]
    m = b * t
    rope = _rope_aux(cos, sin, min(1024, m), None)
    lat = _mm(h, w['w_in'], g=g, epi=_mla_in_epi,
              aux=[_const_aux(w['q_norm']), _const_aux(w['kv_norm'])] + rope)
    qn = _mm(lat, w['uq_nope'], x_cols=(MLA_Q_LORA, 0), out_dtype=BF16)
    qr = _mm(lat, w['uq_rope'], x_cols=(MLA_Q_LORA, 0), epi=_rope32_epi, aux=rope, out_dtype=BF16)
    ql = _headmm(qn, w['ukt'])
    new_lat = lat[:, MLA_Q_LORA:MLA_Q_LORA + MLA_KV_LORA + MLA_ROPE].reshape(b, t, -1)
    hd = lambda a: jnp.transpose(a.reshape(b, t, MLA_HEADS, -1), (0, 2, 1, 3)).reshape(b, MLA_HEADS * t, -1)
    cache = _pages_feature_major(past['mla'][occ])
    ol = _mla_past_attn(past['page_table'], cache, hd(ql), hd(qr), new_lat, t)
    ol = jnp.transpose(ol.reshape(b, MLA_HEADS, t, -1), (0, 2, 1, 3)).reshape(m, -1)
    o = _headmm(ol, w['uv'])
    return _mm(o, w['out'], res=h), new_lat


def _hg_past(h, g, w_in, lower, f_b, g_norm, w_out, b, t, state):
    z = _to_slab(_mm(h, w_in, g=g), b, t)
    os_, s_f = _hgrn(z, lower, f_b, g_norm, state=state)
    return _mm(_from_slab(os_, b, t), w_out, res=h), s_f


def _prepare(prm):
    depth = prm['norm_mix'].shape[0]
    sm = jax.nn.softmax(prm['hg_lb_logits'].astype(F32), axis=0)
    lower = jnp.cumsum(sm, axis=0) - sm[0]
    w = dict(depth=depth, lower=lower)
    w['nsa'] = [_prep_nsa(prm['nsa_w_in'][o], prm['nsa_cmp_pe'][o], prm['nsa_cmp_w1'][o], prm['nsa_cmp_w2'][o],
                          prm['nsa_w_out'][o]) for o in range(prm['nsa_w_in'].shape[0])]
    w['ml'] = [_prep_ml(prm['ml_w_in'][o], prm['ml_gate_b'][o], prm['ml_w_out'][o])
               for o in range(prm['ml_w_in'].shape[0])]
    w['mla'] = [_prep_mla(prm['mla_w_in'][o], prm['mla_q_norm'][o], prm['mla_kv_norm'][o], prm['mla_w_uq'][o],
                          prm['mla_w_uk'][o], prm['mla_w_uv'][o], prm['mla_w_out'][o])
                for o in range(prm['mla_w_in'].shape[0])]
    w['hg'] = [dict(w_in=prm['hg_w_in'][o].astype(BF16), f_b=prm['hg_f_b'][o], norm=prm['hg_norm'][o],
                    out=prm['hg_w_out'][o].astype(BF16)) for o in range(prm['hg_w_in'].shape[0])]
    w['ffn'] = [dict(up=prm['ffn_w_up'][i].astype(BF16),
                     conv4=jnp.concatenate([prm['ffn_conv_w'][i], prm['ffn_conv_b'][i][None]], axis=0).astype(F32),
                     down=prm['ffn_w_down'][i].astype(BF16)) for i in range(depth)]
    w['ple'] = [dict(proj=prm['ple_w_proj'][i].astype(BF16), gate=prm['ple_w_gate'][i].astype(BF16))
                for i in range(depth)]
    return w


def _trunk(x, p, start, past, prm, w):
    b, t, d = x.shape
    depth = w['depth']
    pos = start + jnp.arange(t, dtype=jnp.int32)
    if past is None:
        tabs = {hf: _rope_tables(pos, hf) for hf in (NSA_DH // 2, MLA_ROPE // 2)}
    else:
        tabs = {hf: tuple(jnp.tile(a, (b, 1)) for a in _rope_tables(pos, hf)) for hf in (NSA_DH // 2, MLA_ROPE // 2)}
    new = {}
    h = x.reshape(b * t, d)
    for i in range(depth):
        kind, occ = i % 4, i // 4
        g = prm['norm_mix'][i]
        if kind == 0:
            if past is None:
                h, rows = _nsa_fresh(h, g, w['nsa'][occ], b, t, tabs)
                win = rows[2][:, -min(WINDOW, t):]
            else:
                h, rows, win = _nsa_past(h, g, w['nsa'][occ], b, t, tabs, start, past, occ)
            for name, r in zip(('nsa_cmp', 'nsa_slc'), rows[:2]):
                new.setdefault(name, []).append(r.reshape(b, t, 2, NSA_G, NSA_DH))
            new.setdefault('nsa_win', []).append(win.reshape(b, win.shape[1], 2, NSA_G, NSA_DH))
        elif kind == 1:
            if past is None:
                h, st = _ml_fresh(h, g, w['ml'][occ], b, t)
            else:
                h, st = _ml_past(h, g, w['ml'][occ], b, t, past, occ)
            for name, s in zip(('ml_C', 'ml_n', 'ml_m'), st):
                new.setdefault(name, []).append(s)
        elif kind == 2:
            if past is None:
                h, lat = _mla_fresh(h, g, w['mla'][occ], b, t, tabs)
            else:
                h, lat = _mla_past(h, g, w['mla'][occ], b, t, tabs, past, occ)
            new.setdefault('mla', []).append(lat)
        else:
            hw = w['hg'][occ]
            if past is None:
                h, s_f = _hg_fresh(h, g, hw['w_in'], w['lower'][i], hw['f_b'], hw['norm'], hw['out'], b, t)
            else:
                h, s_f = _hg_past(h, g, hw['w_in'], w['lower'][i], hw['f_b'], hw['norm'], hw['out'], b, t,
                                  past['hg_S'][occ])
            new.setdefault('hg_S', []).append(s_f)
        fw = w['ffn'][i]
        f = fw['down'].shape[0]
        if past is None:
            h, sa, sg = _ffn_seq(h, prm['norm_ffn'][i], fw['up'], fw['conv4'], fw['down'], t)
            new.setdefault('ffn_conv', []).append(jnp.concatenate([sa, sg], axis=-1))
        else:
            buf = past['ffn_conv'][i]
            zero = jnp.zeros((b, t - 1, 2 * f), F32)
            p1 = jnp.concatenate([buf[:, 1:2], zero], axis=1).reshape(b * t, 2 * f)
            p2 = jnp.concatenate([buf, zero[:, 1:]], axis=1).reshape(b * t, 2 * f)
            h, ua, ug = _ffn(h, prm['norm_ffn'][i], fw['up'], fw['conv4'], fw['down'], t, prev=(p1, p2))
            u = jnp.concatenate([ua, ug], axis=-1).reshape(b, t, 2 * f)
            new.setdefault('ffn_conv', []).append(u[:, -(CONV_W - 1):])
        pw = w['ple'][i]
        h = _ple(h, prm['norm_ple'][i], pw['gate'], p[i].reshape(b * t, -1), pw['proj'],
                 final_g=prm['norm_final'] if i == depth - 1 else None)
    return h.reshape(b, t, d), {k: jnp.stack(v) for k, v in new.items()}


def kernel(x_prompt, x_sample, cache_nsa_cmp_kv, cache_nsa_slc_kv, state_nsa_win_kv, cache_mla_latent,
           state_mlstm_C, state_mlstm_n, state_mlstm_m, state_hgrn_S, state_ffn_conv, page_table,
           p_prompt, p_sample, norm_mix, norm_ffn, norm_ple, norm_final, nsa_w_in, nsa_cmp_pe, nsa_cmp_w1,
           nsa_cmp_w2, nsa_w_out, ml_w_in, ml_gate_b, ml_w_out, mla_w_in, mla_q_norm, mla_kv_norm, mla_w_uq,
           mla_w_uk, mla_w_uv, mla_w_out, hg_w_in, hg_f_b, hg_lb_logits, hg_norm, hg_w_out, ffn_w_up,
           ffn_conv_w, ffn_conv_b, ffn_w_down, ple_w_proj, ple_w_gate):
    prm = {
        'norm_mix': norm_mix, 'norm_ffn': norm_ffn, 'norm_ple': norm_ple, 'norm_final': norm_final,
        'nsa_w_in': nsa_w_in, 'nsa_cmp_pe': nsa_cmp_pe, 'nsa_cmp_w1': nsa_cmp_w1, 'nsa_cmp_w2': nsa_cmp_w2,
        'nsa_w_out': nsa_w_out, 'ml_w_in': ml_w_in, 'ml_gate_b': ml_gate_b, 'ml_w_out': ml_w_out,
        'mla_w_in': mla_w_in, 'mla_q_norm': mla_q_norm, 'mla_kv_norm': mla_kv_norm, 'mla_w_uq': mla_w_uq,
        'mla_w_uk': mla_w_uk, 'mla_w_uv': mla_w_uv, 'mla_w_out': mla_w_out, 'hg_w_in': hg_w_in,
        'hg_f_b': hg_f_b, 'hg_lb_logits': hg_lb_logits, 'hg_norm': hg_norm, 'hg_w_out': hg_w_out,
        'ffn_w_up': ffn_w_up, 'ffn_conv_w': ffn_conv_w, 'ffn_conv_b': ffn_conv_b, 'ffn_w_down': ffn_w_down,
        'ple_w_proj': ple_w_proj, 'ple_w_gate': ple_w_gate,
    }
    past = {
        'nsa_cmp': cache_nsa_cmp_kv, 'nsa_slc': cache_nsa_slc_kv, 'nsa_win': state_nsa_win_kv,
        'mla': cache_mla_latent, 'ml_C': state_mlstm_C, 'ml_n': state_mlstm_n, 'ml_m': state_mlstm_m,
        'hg_S': state_hgrn_S, 'ffn_conv': state_ffn_conv, 'page_table': page_table,
    }
    w = _prepare(prm)
    past_len = page_table.shape[1] * PAGE
    y_p, sp = _trunk(x_prompt, p_prompt, 0, None, prm, w)
    y_s, ss = _trunk(x_sample, p_sample, past_len, past, prm, w)
    return (y_p, y_s,
            sp['nsa_cmp'], ss['nsa_cmp'], sp['nsa_slc'], ss['nsa_slc'], sp['nsa_win'], ss['nsa_win'],
            sp['mla'], ss['mla'], sp['ml_C'], ss['ml_C'], sp['ml_n'], ss['ml_n'], sp['ml_m'], ss['ml_m'],
            sp['hg_S'], ss['hg_S'], sp['ffn_conv'], ss['ffn_conv'])
```

```python
import functools
import math

import numpy as np
import jax
import jax.numpy as jnp
from jax import lax
from jax.experimental import pallas as pl
from jax.experimental.pallas import tpu as pltpu

F32 = jnp.float32
BF16 = jnp.bfloat16
NEG = -1e30
NORM_EPS = 1e-6
ROPE_THETA = 10000.0
FORCE_BONUS = 1e4

LANES = 128
VMEM_LIMIT = 56 * 1024 * 1024

PAGE = 128
NSA_HEADS, NSA_G, NSA_DH = 16, 2, 64
CMP_LEN, CMP_STRIDE, CMP_HID = 32, 16, 256
SLC_BLOCK, SLC_TOPN, WINDOW = 64, 16, 512
ML_HEADS, ML_DK, ML_DV = 4, 128, 256
MLA_HEADS, MLA_Q_LORA, MLA_KV_LORA, MLA_NOPE, MLA_ROPE, MLA_DV = 16, 512, 256, 64, 32, 64
HG_HEADS, HG_DK, HG_DV = 8, 128, 128
CONV_W = 3


def _cparams(*sem):
    return pltpu.CompilerParams(dimension_semantics=sem, vmem_limit_bytes=VMEM_LIMIT)


def _dot(a, b):
    return jnp.dot(a.astype(BF16), b.astype(BF16), preferred_element_type=F32)


def _dot_nt(a, b):
    return lax.dot_general(a.astype(BF16), b.astype(BF16), (((1,), (1,)), ((), ())),
                           preferred_element_type=F32)


def _dot_f32(a, b):
    return jnp.dot(a, b, precision=lax.Precision.HIGHEST, preferred_element_type=F32)


def _dot_nt_f32(a, b):
    return lax.dot_general(a, b, (((1,), (1,)), ((), ())), precision=lax.Precision.HIGHEST,
                           preferred_element_type=F32)


def _rms(x, g):
    return x * lax.rsqrt(jnp.mean(x * x, axis=-1, keepdims=True) + NORM_EPS) * g


def _sigmoid(x):
    return 1.0 / (1.0 + jnp.exp(-x))


def _silu(x):
    return x * _sigmoid(x)


def _rope_tile(z, cos, sin, half):
    n = z.shape[1]
    reps = n // LANES
    c = jnp.concatenate([cos] * reps, axis=1) if reps > 1 else cos
    s = jnp.concatenate([sin] * reps, axis=1) if reps > 1 else sin
    lane = lax.broadcasted_iota(jnp.int32, z.shape, 1)
    lower = (lane & (2 * half - 1)) < half
    partner = jnp.where(lower, pltpu.roll(z, n - half, 1), pltpu.roll(z, half, 1))
    return z * c + partner * s


def _rope_tables(pos, half):
    inv = jnp.power(ROPE_THETA, -jnp.arange(half, dtype=F32) / half)
    ang = pos.astype(F32)[:, None] * inv[None, :]
    cos, sin = jnp.cos(ang), jnp.sin(ang)
    reps = LANES // (2 * half)
    return (jnp.tile(jnp.concatenate([cos, cos], axis=1), (1, reps)),
            jnp.tile(jnp.concatenate([-sin, sin], axis=1), (1, reps)))


def _mm_kernel(*refs, norm, res, epi, n_aux, x_slabs, out_slabs):
    x_ref = refs[0]
    pos = 1
    g_ref = None
    if norm:
        g_ref = refs[pos]
        pos += 1
    w_ref = refs[pos]
    pos += 1
    aux = refs[pos:pos + n_aux]
    pos += n_aux
    r_ref = None
    if res:
        r_ref = refs[pos]
        pos += 1
    o_ref = refs[pos]
    j = pl.program_id(1)
    if norm:
        xn_ref = refs[pos + 1]

        @pl.when(j == 0)
        def _():
            xn_ref[...] = _rms(x_ref[...], g_ref[...]).astype(BF16)

        x = xn_ref[...]
    elif x_slabs:
        x = jnp.concatenate([x_ref[s] for s in range(x_slabs)], axis=1)
    else:
        x = x_ref[...]
    z = _dot(x, w_ref[...])
    if epi is not None:
        z = epi(z, j, *[a[...] for a in aux])
    if res:
        z = z + r_ref[...]
    if out_slabs:
        for s in range(out_slabs):
            o_ref[s] = z[:, s * LANES:(s + 1) * LANES].astype(o_ref.dtype)
    else:
        o_ref[...] = z.astype(o_ref.dtype)


def _mm(x, w, *, g=None, res=None, epi=None, aux=(), out_dtype=F32, layout='flat',
        seq=None, tm=1024, tn=1024, x_cols=None, x_slab_seq=None):
    k, n = w.shape
    if x_slab_seq is not None:
        b, ks, t, _ = x.shape
        m = b * t
    else:
        m = x.shape[0]
        t = seq
    tm = min(tm, m if t is None else t)
    tn = min(tn, n)
    assert m % tm == 0 and n % tn == 0
    ni, nj = m // tm, n // tn
    tps = None if t is None else t // tm
    in_specs, args = [], []
    if x_slab_seq is not None:
        in_specs.append(pl.BlockSpec((None, ks, tm, LANES), lambda i, j: (i // tps, 0, i % tps, 0)))
    elif x_cols is not None:
        in_specs.append(pl.BlockSpec((tm, x_cols[0]), lambda i, j: (i, x_cols[1])))
    else:
        in_specs.append(pl.BlockSpec((tm, k), lambda i, j: (i, 0)))
    args.append(x)
    if g is not None:
        in_specs.append(pl.BlockSpec((1, k), lambda i, j: (0, 0)))
        args.append(g.reshape(1, k).astype(F32))
    in_specs.append(pl.BlockSpec((k, tn), lambda i, j: (0, j)))
    args.append(w)
    for a, bs, im in aux:
        in_specs.append(pl.BlockSpec(bs, im))
        args.append(a)
    if res is not None:
        in_specs.append(pl.BlockSpec((tm, tn), lambda i, j: (i, j)))
        args.append(res)
    if layout == 'flat':
        out_shape = jax.ShapeDtypeStruct((m, n), out_dtype)
        out_spec = pl.BlockSpec((tm, tn), lambda i, j: (i, j))
        out_slabs = 0
    elif layout == 'tiles':
        out_shape = jax.ShapeDtypeStruct((nj, m, tn), out_dtype)
        out_spec = pl.BlockSpec((None, tm, tn), lambda i, j: (j, i, 0))
        out_slabs = 0
    else:
        out_slabs = tn // LANES
        out_shape = jax.ShapeDtypeStruct((m // t, n // LANES, t, LANES), out_dtype)
        out_spec = pl.BlockSpec((None, out_slabs, tm, LANES), lambda i, j: (i // tps, j, i % tps, 0))
    scratch = [pltpu.VMEM((tm, k), BF16)] if g is not None else []
    kern = functools.partial(_mm_kernel, norm=g is not None, res=res is not None, epi=epi,
                             n_aux=len(aux), x_slabs=(ks if x_slab_seq is not None else 0),
                             out_slabs=out_slabs)
    return pl.pallas_call(
        kern, grid=(ni, nj), in_specs=in_specs, out_specs=out_spec, out_shape=out_shape,
        scratch_shapes=scratch, compiler_params=_cparams('parallel', 'arbitrary'),
        name='mm')(*args)


def _rope_aux(cos, sin, tm, period_blocks):
    if period_blocks is None:
        im = lambda i, j: (i, 0)
    else:
        im = lambda i, j: (i % period_blocks, 0)
    return [(cos, (tm, LANES), im), (sin, (tm, LANES), im)]


def _headmm_kernel(x_ref, w_ref, o_ref, *, heads, a, c):
    for h in range(heads):
        o_ref[:, h * c:(h + 1) * c] = _dot(x_ref[:, h * a:(h + 1) * a], w_ref[h]).astype(o_ref.dtype)


def _headmm(x, w, out_dtype=BF16, tm=512):
    m = x.shape[0]
    heads, a, c = w.shape
    tm = min(tm, m)
    return pl.pallas_call(
        functools.partial(_headmm_kernel, heads=heads, a=a, c=c), grid=(m // tm,),
        in_specs=[pl.BlockSpec((tm, heads * a), lambda i: (i, 0)),
                  pl.BlockSpec((heads, a, c), lambda i: (0, 0, 0))],
        out_specs=pl.BlockSpec((tm, heads * c), lambda i: (i, 0)),
        out_shape=jax.ShapeDtypeStruct((m, heads * c), out_dtype),
        compiler_params=_cparams('parallel'), name='headmm')(x, w)


def _ffn_kernel(*refs, tps, seq_len, has_prev):
    if has_prev:
        (h_ref, g_ref, wa_ref, wg_ref, ca_ref, cg_ref, wd_ref, p1a_ref, p1g_ref, p2a_ref, p2g_ref,
         o_ref, ua_ref, ug_ref, hn_ref) = refs
    else:
        (h_ref, g_ref, wa_ref, wg_ref, ca_ref, cg_ref, wd_ref,
         o_ref, sa_ref, sg_ref, hn_ref, carry_ref) = refs
    i, j = pl.program_id(0), pl.program_id(1)
    tm = h_ref.shape[0]
    ch = min(64, tm)

    @pl.when(j == 0)
    def _():
        x = h_ref[...]
        hn_ref[...] = _rms(x, g_ref[...]).astype(BF16)
        o_ref[...] = x

    hn = hn_ref[...]
    ua = _dot(hn, wa_ref[...])
    ug = _dot(hn, wg_ref[...])
    if has_prev:
        assert ch % seq_len == 0
        ua_ref[...] = ua
        ug_ref[...] = ug
        prevs = (None, None)
    else:
        @pl.when(i % tps == 0)
        def _():
            carry_ref[j] = jnp.zeros(carry_ref.shape[1:], F32)

        prevs = (carry_ref[j, 0], carry_ref[j, 1])
        carry_ref[j, 0] = ua[tm - 8:]
        carry_ref[j, 1] = ug[tm - 8:]
        sa_ref[...] = pltpu.roll(ua[tm - 8:], CONV_W - 1, 0)[0:CONV_W - 1]
        sg_ref[...] = pltpu.roll(ug[tm - 8:], CONV_W - 1, 0)[0:CONV_W - 1]

    def conv_chunk(u, cw, prev8, p1_ref, p2_ref, r0):
        cur = u[r0:r0 + ch]
        if has_prev:
            t = lax.broadcasted_iota(jnp.int32, cur.shape, 0) % seq_len
            s1 = jnp.where(t >= 1, pltpu.roll(cur, 1, 0), p1_ref[r0:r0 + ch, :])
            s2 = jnp.where(t >= 2, pltpu.roll(cur, 2, 0), p2_ref[r0:r0 + ch, :])
        else:
            x = jnp.concatenate([prev8 if r0 == 0 else u[r0 - 8:r0], cur], axis=0)
            s1 = pltpu.roll(x, 1, 0)[8:]
            s2 = pltpu.roll(x, 2, 0)[8:]
        return cw[0:1] * s2 + cw[1:2] * s1 + cw[2:3] * cur + cw[3:4]

    cwa, cwg = ca_ref[...], cg_ref[...]
    acts = []
    for c in range(tm // ch):
        ca = conv_chunk(ua, cwa, prevs[0], p1a_ref if has_prev else None, p2a_ref if has_prev else None, c * ch)
        cg = conv_chunk(ug, cwg, prevs[1], p1g_ref if has_prev else None, p2g_ref if has_prev else None, c * ch)
        acts.append((_silu(ca) * cg).astype(BF16))
    act = jnp.concatenate(acts, axis=0) if len(acts) > 1 else acts[0]
    o_ref[...] += _dot(act, wd_ref[...])


def _ffn(h, g, w_up, conv4, w_down, seq_len, prev=None, tn=256):
    m, d = h.shape
    f = w_down.shape[0]
    nj = f // tn
    has_prev = prev is not None
    tm = m if has_prev else min(1024, seq_len)
    tps = max(seq_len // tm, 1)
    ni = m // tm
    specs = [pl.BlockSpec((tm, d), lambda i, j: (i, 0)),
             pl.BlockSpec((1, d), lambda i, j: (0, 0)),
             pl.BlockSpec((d, tn), lambda i, j: (0, j)),
             pl.BlockSpec((d, tn), lambda i, j: (0, j + nj)),
             pl.BlockSpec((4, tn), lambda i, j: (0, j)),
             pl.BlockSpec((4, tn), lambda i, j: (0, j + nj)),
             pl.BlockSpec((tn, d), lambda i, j: (j, 0))]
    args = [h, g.reshape(1, d).astype(F32), w_up, w_up, conv4, conv4, w_down]
    scratch = [pltpu.VMEM((tm, d), BF16)]
    if has_prev:
        p1, p2 = prev
        specs += [pl.BlockSpec((tm, tn), lambda i, j: (i, j)), pl.BlockSpec((tm, tn), lambda i, j: (i, j + nj)),
                  pl.BlockSpec((tm, tn), lambda i, j: (i, j)), pl.BlockSpec((tm, tn), lambda i, j: (i, j + nj))]
        args += [p1, p1, p2, p2]
        out_shape = [jax.ShapeDtypeStruct((m, d), F32), jax.ShapeDtypeStruct((m, f), F32),
                     jax.ShapeDtypeStruct((m, f), F32)]
        out_specs = [pl.BlockSpec((tm, d), lambda i, j: (i, 0)), pl.BlockSpec((tm, tn), lambda i, j: (i, j)),
                     pl.BlockSpec((tm, tn), lambda i, j: (i, j))]
    else:
        out_shape = [jax.ShapeDtypeStruct((m, d), F32), jax.ShapeDtypeStruct((ni, CONV_W - 1, f), F32),
                     jax.ShapeDtypeStruct((ni, CONV_W - 1, f), F32)]
        out_specs = [pl.BlockSpec((tm, d), lambda i, j: (i, 0)),
                     pl.BlockSpec((None, CONV_W - 1, tn), lambda i, j: (i, 0, j)),
                     pl.BlockSpec((None, CONV_W - 1, tn), lambda i, j: (i, 0, j))]
        scratch.append(pltpu.VMEM((nj, 2, 8, tn), F32))
    outs = pl.pallas_call(
        functools.partial(_ffn_kernel, tps=tps, seq_len=seq_len, has_prev=has_prev),
        grid=(ni, nj), in_specs=specs, out_specs=out_specs, out_shape=out_shape,
        scratch_shapes=scratch, compiler_params=_cparams('arbitrary', 'arbitrary'), name='ffn')(*args)
    if has_prev:
        return outs
    return outs[0], outs[1][tps - 1::tps], outs[2][tps - 1::tps]


def _ple_kernel(*refs, final):
    if final:
        h_ref, g_ref, wg_ref, p_ref, wp_ref, gf_ref, o_ref = refs
    else:
        h_ref, g_ref, wg_ref, p_ref, wp_ref, o_ref = refs
    h = h_ref[...]
    gate = _sigmoid(_dot(_rms(h, g_ref[...]), wg_ref[...]))
    y = h + gate * _dot(p_ref[...], wp_ref[...])
    if final:
        y = _rms(y, gf_ref[...])
    o_ref[...] = y


def _ple(h, g, w_gate, p, w_proj, final_g=None, tm=1024):
    m, d = h.shape
    pd = p.shape[1]
    tm = min(tm, m)
    specs = [pl.BlockSpec((tm, d), lambda i: (i, 0)), pl.BlockSpec((1, d), lambda i: (0, 0)),
             pl.BlockSpec((d, d), lambda i: (0, 0)), pl.BlockSpec((tm, pd), lambda i: (i, 0)),
             pl.BlockSpec((pd, d), lambda i: (0, 0))]
    args = [h, g.reshape(1, d).astype(F32), w_gate, p, w_proj]
    if final_g is not None:
        specs.append(pl.BlockSpec((1, d), lambda i: (0, 0)))
        args.append(final_g.reshape(1, d).astype(F32))
    return pl.pallas_call(
        functools.partial(_ple_kernel, final=final_g is not None), grid=(m // tm,),
        in_specs=specs, out_specs=pl.BlockSpec((tm, d), lambda i: (i, 0)),
        out_shape=jax.ShapeDtypeStruct((m, d), F32), compiler_params=_cparams('parallel'),
        name='ple')(*args)


def _softmax_tile(s, mask):
    s = jnp.where(mask, s, NEG)
    e = jnp.where(mask, jnp.exp(s - jnp.max(s, axis=-1, keepdims=True)), 0.0)
    return e / jnp.maximum(jnp.sum(e, axis=-1, keepdims=True), 1e-30)


def _online_init(m_ref, l_ref, acc_ref):
    m_ref[...] = jnp.full(m_ref.shape, NEG, F32)
    l_ref[...] = jnp.zeros(l_ref.shape, F32)
    acc_ref[...] = jnp.zeros(acc_ref.shape, F32)


def _online_update(s, mask, v, m_ref, l_ref, acc_ref, v_t=False):
    if mask is not None:
        s = jnp.where(mask, s, NEG)
    m_old = m_ref[...]
    m_new = jnp.maximum(m_old, jnp.max(s, axis=-1, keepdims=True))
    p = jnp.exp(s - m_new)
    if mask is not None:
        p = jnp.where(mask, p, 0.0)
    alpha = jnp.exp(m_old - m_new)
    l_ref[...] = alpha * l_ref[...] + jnp.sum(p, axis=-1, keepdims=True)
    acc_ref[...] = alpha * acc_ref[...] + (_dot_nt(p, v) if v_t else _dot(p, v))
    m_ref[...] = m_new


def _online_result(l_ref, acc_ref):
    return acc_ref[...] / jnp.maximum(l_ref[...], 1e-30)


def _online_update_t(st, bias, v_t, m_ref, l_ref, acc_ref, base2=False):
    if bias is not None:
        st = st + bias
    ex = jnp.exp2 if base2 else jnp.exp
    m_old = m_ref[...]
    m_new = jnp.maximum(m_old, jnp.max(st, axis=0, keepdims=True))
    p = ex(st - m_new)
    alpha = ex(m_old - m_new)
    l_ref[...] = alpha * l_ref[...] + jnp.sum(p, axis=0, keepdims=True)
    acc_ref[...] = alpha * acc_ref[...] + _dot(v_t, p)
    m_ref[...] = m_new


def _softmax_tile_t(st, mask):
    st = jnp.where(mask, st, NEG)
    e = jnp.where(mask, jnp.exp(st - jnp.max(st, axis=0, keepdims=True)), 0.0)
    return e / jnp.maximum(jnp.sum(e, axis=0, keepdims=True), 1e-30)


def _cumsum(x, axis):
    n = x.shape[axis]
    idx = lax.broadcasted_iota(jnp.int32, x.shape, axis)
    k = 1
    while k < n:
        x = x + jnp.where(idx >= k, pltpu.roll(x, k, axis), 0.0)
        k *= 2
    return x


def _log_sigmoid(x):
    return jnp.minimum(x, 0.0) - jnp.log(1.0 + jnp.exp(-jnp.abs(x)))


def _compress_rows(k_ref, v_ref, n_seg, pe_ref, w1_ref, w2p_ref):
    outs = []
    seg_w = CMP_STRIDE * NSA_DH
    lane = lax.broadcasted_iota(jnp.int32, (n_seg, LANES), 1)
    for kind, rows_ref in enumerate((k_ref, v_ref)):
        xs = [rows_ref[pl.ds(s, n_seg, stride=CMP_STRIDE), :] for s in range(CMP_STRIDE)]
        o = None
        for g in range(NSA_G):
            pieces = []
            for a in range(CMP_STRIDE // 2):
                ev, od = xs[2 * a], xs[2 * a + 1]
                if g == 0:
                    pieces.append(jnp.where(lane < NSA_DH, ev, pltpu.roll(od, NSA_DH, 1)))
                else:
                    pieces.append(jnp.where(lane < NSA_DH, pltpu.roll(ev, NSA_DH, 1), od))
            seg = jnp.concatenate(pieces, axis=1)
            pre = None
            for r in range(CMP_LEN // CMP_STRIDE):
                acc = _dot(seg + pe_ref[kind, r:r + 1, :], w1_ref[kind, r * seg_w:(r + 1) * seg_w, :])
                pre = acc if r == 0 else pre + pltpu.roll(acc, n_seg - r, 0)
            t = _dot(_silu(pre), w2p_ref[kind, g])
            o = t if o is None else o + t
        outs.append(o)
    return outs


def _nsa_cmp_kernel(k_ref, v_ref, pe_ref, w1_ref, w2p_ref, kc_ref, vc_ref, *, n_seg):
    kc, vc = _compress_rows(k_ref, v_ref, n_seg, pe_ref, w1_ref, w2p_ref)
    kc_ref[...] = kc
    vc_ref[...] = vc


def _nsa_compress(rows, pe, w1, w2p):
    b, t, _ = rows.shape
    n_seg = t // CMP_STRIDE
    full = lambda shp: pl.BlockSpec(shp, lambda i: (0,) * len(shp))
    return pl.pallas_call(
        functools.partial(_nsa_cmp_kernel, n_seg=n_seg), grid=(b,),
        in_specs=[pl.BlockSpec((None, t, LANES), lambda i: (i, 0, 0)),
                  pl.BlockSpec((None, t, LANES), lambda i: (i, 0, 1)), full(pe.shape), full(w1.shape),
                  full(w2p.shape)],
        out_specs=[pl.BlockSpec((None, n_seg, LANES), lambda i: (i, 0, 0))] * 2,
        out_shape=[jax.ShapeDtypeStruct((b, n_seg, LANES), F32)] * 2,
        compiler_params=_cparams('parallel'), name='nsa_compress')(rows, rows, pe, w1, w2p)


def _overlap_t(nc, ns, nc_pad, ns_pad):
    cs = np.arange(nc)[None, :] * CMP_STRIDE
    ss = np.arange(ns)[:, None] * SLC_BLOCK
    ov = np.maximum(np.minimum(cs + CMP_LEN, ss + SLC_BLOCK) - np.maximum(cs, ss), 0) / CMP_LEN
    out = np.zeros((ns_pad, nc_pad), np.float32)
    out[:ns, :nc] = ov
    return jnp.asarray(out)


def _select_blocks(sc, valid, blk, n_cand, n_sel):
    rank = jnp.zeros(sc.shape, F32)
    for j in range(n_cand):
        rj = sc[j:j + 1, :]
        beats = (rj > sc) | ((rj == sc) & (blk > j))
        rank = rank + jnp.where(beats, 1.0, 0.0)
    return jnp.where(valid & (rank < n_sel), 1.0, 0.0)


def _nsa_attn_kernel(q_ref, gt_ref, kc_ref, vc_ref, ks_ref, vs_ref, kw_ref, vw_ref, ovt_ref, et_ref,
                     o_ref, ocmp_ref, msk_ref, wmsk_ref, qt_ref, vst_ref, vwt_ref, m_ref, l_ref, acc_ref,
                     *, tq, t_len, nc, n_sel):
    qi = pl.program_id(1)
    q0 = qi * tq
    scale = NSA_DH ** -0.5
    kt = 256
    ns = t_len // SLC_BLOCK
    n_pairs = NSA_HEADS // 2

    @pl.when(qi == 0)
    def _():
        for jt in range(t_len // kt):
            vst_ref[jt] = vs_ref[jt * kt:(jt + 1) * kt, :].T.astype(BF16)
            vwt_ref[jt] = vw_ref[jt * kt:(jt + 1) * kt, :].T.astype(BF16)

    hpg = NSA_HEADS // NSA_G
    pairs_g = n_pairs // NSA_G
    wq = NSA_HEADS * tq
    zeros_half = jnp.zeros((NSA_DH, tq), F32)
    for hp in range(n_pairs):
        g = hp // pairs_g
        slab_t = q_ref[hp].astype(F32).T * scale
        for u, part in enumerate((slab_t[:NSA_DH], slab_t[NSA_DH:])):
            ext = jnp.concatenate([part, zeros_half] if g == 0 else [zeros_half, part], axis=0)
            qt_ref[:, (2 * hp + u) * tq:(2 * hp + u + 1) * tq] = ext.astype(BF16)
    q_all = qt_ref[...]

    kc = kc_ref[...]
    vc_t = vc_ref[...].T
    ncp = kc.shape[0]
    cidx = lax.broadcasted_iota(jnp.int32, (ncp, wq), 0)
    qpos_c = q0 + (lax.broadcasted_iota(jnp.int32, (ncp, wq), 1) & (tq - 1))
    cmask = (cidx < nc) & (cidx * CMP_STRIDE + (CMP_LEN - 1) <= qpos_c)
    p = _softmax_tile_t(_dot(kc, q_all), cmask)
    ocmp_ref[...] = _dot(vc_t, p)
    imp = []
    for g in range(NSA_G):
        tot = p[:, g * hpg * tq:(g * hpg + 1) * tq]
        for hh in range(1, hpg):
            tot = tot + p[:, (g * hpg + hh) * tq:(g * hpg + hh + 1) * tq]
        imp.append(tot)

    nsp = ovt_ref.shape[0]
    blk = lax.broadcasted_iota(jnp.int32, (ns, tq), 0)
    cur = (q0 + lax.broadcasted_iota(jnp.int32, (ns, tq), 1)) // SLC_BLOCK
    forced = (blk == 0) | (blk == cur) | (blk == cur - 1)
    valid = blk <= cur
    krow1 = lax.broadcasted_iota(jnp.int32, (kt, tq), 0)
    qpos1 = q0 + lax.broadcasted_iota(jnp.int32, (kt, tq), 1)
    for g in range(NSA_G):
        sc = _dot_f32(ovt_ref[...], imp[g])[:ns]
        sc = jnp.where(valid, sc + jnp.where(forced, FORCE_BONUS, 0.0), NEG)
        sel_t = _select_blocks(sc, valid, blk, ns, n_sel)
        if nsp > ns:
            sel_t = jnp.concatenate([sel_t, jnp.zeros((nsp - ns, tq), F32)], axis=0)
        hit = _dot(et_ref[...], sel_t)
        for jt in range(t_len // kt):
            msk_ref[g, jt] = jnp.where((jt * kt + krow1 <= qpos1) & (hit[jt * kt:(jt + 1) * kt] > 0.5), 0.0, NEG)
    win_tiles = list(range(-WINDOW, tq, kt))[::-1]
    for wi, d in enumerate(win_tiles):
        kpos = q0 + d + krow1
        wmsk_ref[wi] = jnp.where((kpos <= qpos1) & (kpos > qpos1 - WINDOW), 0.0, NEG)

    gt_t = gt_ref[...].T
    _online_init(m_ref, l_ref, acc_ref)

    def slc_body(jt, carry):
        start = pl.multiple_of(jt * kt, kt)
        bias = jnp.concatenate([msk_ref[g, jt] for g in range(NSA_G) for _ in range(hpg)], axis=1)
        _online_update_t(_dot(ks_ref[pl.ds(start, kt), :], q_all), bias, vst_ref[jt], m_ref, l_ref, acc_ref)
        return carry

    lax.fori_loop(0, (q0 + tq) // kt, slc_body, 0)
    o_slc = _online_result(l_ref, acc_ref)

    _online_init(m_ref, l_ref, acc_ref)
    for wi, d in enumerate(win_tiles):
        @pl.when(q0 + d >= 0)
        def _():
            start = pl.multiple_of(q0 + d, kt)
            bias = jnp.concatenate([wmsk_ref[wi]] * NSA_HEADS, axis=1)
            _online_update_t(_dot(kw_ref[pl.ds(start, kt), :], q_all), bias, vwt_ref[(q0 + d) // kt],
                             m_ref, l_ref, acc_ref)
    o_win = _online_result(l_ref, acc_ref)

    def gate(c):
        return jnp.concatenate([gt_t[c * NSA_HEADS + h:c * NSA_HEADS + h + 1, :] for h in range(NSA_HEADS)], axis=1)

    o = gate(0) * ocmp_ref[...] + gate(1) * o_slc + gate(2) * o_win
    for hp in range(n_pairs):
        g = hp // pairs_g
        o_ref[hp] = jnp.concatenate(
            [o[g * NSA_DH:(g + 1) * NSA_DH, 2 * hp * tq:(2 * hp + 1) * tq],
             o[g * NSA_DH:(g + 1) * NSA_DH, (2 * hp + 1) * tq:(2 * hp + 2) * tq]], axis=0).T.astype(o_ref.dtype)


def _nsa_attn(q_slab, gates, kc, vc, slc_rows, win_rows, tq=256):
    b, n_pairs, t, _ = q_slab.shape
    tq = min(tq, t)
    n_seg = kc.shape[1]
    nc = n_seg - 1
    ns = t // SLC_BLOCK
    nsp = LANES
    ovt = _overlap_t(nc, ns, n_seg, nsp)
    e = jnp.asarray((np.arange(t)[:, None] // SLC_BLOCK == np.arange(nsp)[None, :]).astype(np.float32), BF16)
    kern = functools.partial(_nsa_attn_kernel, tq=tq, t_len=t, nc=nc, n_sel=min(SLC_TOPN, ns))
    wq = NSA_HEADS * tq
    seq = lambda c: pl.BlockSpec((None, t, LANES), lambda i, j: (i, 0, c))
    return pl.pallas_call(
        kern, grid=(b, t // tq),
        in_specs=[pl.BlockSpec((None, n_pairs, tq, LANES), lambda i, j: (i, 0, j, 0)),
                  pl.BlockSpec((None, tq, LANES), lambda i, j: (i, j, 0)),
                  pl.BlockSpec((None, n_seg, LANES), lambda i, j: (i, 0, 0)),
                  pl.BlockSpec((None, n_seg, LANES), lambda i, j: (i, 0, 0)),
                  seq(0), seq(1), seq(0), seq(1),
                  pl.BlockSpec(ovt.shape, lambda i, j: (0, 0)),
                  pl.BlockSpec(e.shape, lambda i, j: (0, 0))],
        out_specs=pl.BlockSpec((None, n_pairs, tq, LANES), lambda i, j: (i, 0, j, 0)),
        out_shape=jax.ShapeDtypeStruct((b, n_pairs, t, LANES), BF16),
        scratch_shapes=[pltpu.VMEM((LANES, wq), F32),
                        pltpu.VMEM((NSA_G, t // 256, 256, tq), F32),
                        pltpu.VMEM((len(range(-WINDOW, tq, 256)), 256, tq), F32),
                        pltpu.VMEM((LANES, wq), BF16),
                        pltpu.VMEM((t // 256, LANES, 256), BF16), pltpu.VMEM((t // 256, LANES, 256), BF16),
                        pltpu.VMEM((1, wq), F32), pltpu.VMEM((1, wq), F32), pltpu.VMEM((LANES, wq), F32)],
        compiler_params=_cparams('arbitrary', 'arbitrary'), name='nsa_attn')(
            q_slab, gates, kc, vc, slc_rows, slc_rows, win_rows, win_rows, ovt, e)


def _mlstm_kernel(*refs, L, t_real, has_state, pad):
    it = iter(refs)
    z_ref, gc_ref, gr_ref, bc_ref, br_ref = next(it), next(it), next(it), next(it), next(it)
    if has_state:
        c0_ref, n0_ref, m0_ref = next(it), next(it), next(it)
    h_ref, c_out, n_out, m_out = next(it), next(it), next(it), next(it)
    c_scr, n_scr, m_scr = next(it), next(it), next(it)
    if pad:
        zp_ref, gcp_ref, grp_ref = next(it), next(it), next(it)
    ci = pl.program_id(1)
    nchunks = pl.num_programs(1)

    @pl.when(ci == 0)
    def _():
        if has_state:
            c_scr[...] = c0_ref[...]
            n_scr[...] = n0_ref[...]
            m_scr[...] = m0_ref[...]
        else:
            c_scr[...] = jnp.zeros(c_scr.shape, F32)
            n_scr[...] = jnp.zeros(n_scr.shape, F32)
            m_scr[...] = jnp.zeros(m_scr.shape, F32)

    if pad:
        @pl.when((pl.program_id(0) == 0) & (ci == 0))
        def _():
            zp_ref[...] = jnp.zeros(zp_ref.shape, F32)
            gcp_ref[...] = jnp.zeros(gcp_ref.shape, F32)
            grp_ref[...] = jnp.zeros(grp_ref.shape, F32)

        zp_ref[:, 0:t_real, :] = z_ref[...]
        gcp_ref[0:t_real, :] = gc_ref[...]
        grp_ref[:, 0:t_real] = gr_ref[...]
        z_ref, gc_ref, gr_ref = zp_ref, gcp_ref, grp_ref

    H = ML_HEADS
    gcol = gc_ref[...] + bc_ref[...]
    grow = gr_ref[...] + br_ref[...]
    lf_c = _log_sigmoid(gcol)
    lf_r = _log_sigmoid(grow)
    ig_c, ig_r = gcol, grow
    if t_real < L:
        rv = lax.broadcasted_iota(jnp.int32, gcol.shape, 0) < t_real
        lv = lax.broadcasted_iota(jnp.int32, grow.shape, 1) < t_real
        lf_c, ig_c = jnp.where(rv, lf_c, 0.0), jnp.where(rv, ig_c, NEG)
        lf_r, ig_r = jnp.where(lv, lf_r, 0.0), jnp.where(lv, ig_r, NEG)
    bcum_c = _cumsum(lf_c, 0)
    bcum_r = _cumsum(lf_r, 1)
    tri = lax.broadcasted_iota(jnp.int32, (L, L), 0) >= lax.broadcasted_iota(jnp.int32, (L, L), 1)
    heads = range(H)
    n_all = n_scr[...]
    m_all = m_scr[...]
    q = [z_ref[h] for h in heads]
    k = [z_ref[H + h] * (ML_DK ** -0.5) for h in heads]
    v = [jnp.concatenate([z_ref[2 * H + 2 * h], z_ref[2 * H + 2 * h + 1]], axis=1) for h in heads]
    c_st = [c_scr[h] for h in heads]
    s_qk = [_dot_nt(q[h], k[h]) for h in heads]
    s_qc = [_dot_nt(q[h], c_st[h]) for h in heads]
    sc, mt, w_state, m_old, m_new, ws_t, wc = [], [], [], [], [], [], []
    for h in heads:
        bc_t = bcum_c[:, H + h:H + h + 1]
        bc_s = bcum_r[H + h:H + h + 1, :]
        ig_s = ig_r[h:h + 1, :]
        ig_t = ig_c[:, h:h + 1]
        m_old.append(m_all[h:h + 1, 0:1])
        dmat = jnp.where(tri, bc_t - bc_s + ig_s, NEG)
        inter = bc_t + m_old[h]
        mt.append(jnp.maximum(jnp.max(dmat, axis=1, keepdims=True), inter))
        w_state.append(jnp.exp(inter - mt[h]))
        sc.append(s_qk[h] * jnp.exp(dmat - mt[h]))
        bl = bc_s[:, L - 1:L]
        dl_r = bl - bc_s + ig_s
        dl_t = bl - bc_t + ig_t
        m_new.append(jnp.maximum(bl + m_old[h], jnp.max(dl_r, axis=1, keepdims=True)))
        ws_t.append(jnp.exp(dl_t - m_new[h]))
        wc.append(jnp.exp(bl + m_old[h] - m_new[h]))
    s_v = [_dot(sc[h], v[h]) for h in heads]
    upd = [_dot((v[h] * ws_t[h]).T, k[h]) for h in heads]
    for h in heads:
        n_st = n_all[h:h + 1, :]
        og = jnp.concatenate([z_ref[4 * H + 2 * h], z_ref[4 * H + 2 * h + 1]], axis=1)
        num = s_v[h] + w_state[h] * s_qc[h]
        den = jnp.sum(sc[h], axis=1, keepdims=True) + w_state[h] * jnp.sum(q[h] * n_st, axis=1, keepdims=True)
        hh = num / jnp.maximum(jnp.abs(den), jnp.exp(-mt[h])) * og
        h_ref[2 * h] = hh[0:h_ref.shape[1], 0:LANES].astype(h_ref.dtype)
        h_ref[2 * h + 1] = hh[0:h_ref.shape[1], LANES:].astype(h_ref.dtype)
        c_scr[h] = wc[h] * c_st[h] + upd[h]
        n_scr[h:h + 1, :] = wc[h] * n_st + jnp.sum(ws_t[h] * k[h], axis=0, keepdims=True)
        m_scr[h:h + 1, :] = jnp.broadcast_to(m_new[h], (1, LANES))

    @pl.when(ci == nchunks - 1)
    def _():
        c_out[...] = c_scr[...]
        n_out[...] = n_scr[...]
        m_out[...] = m_scr[...]


def _mlstm(z_slab, gates_col, gates_row, gate_b, state=None, L=256):
    b, ns, t, _ = z_slab.shape
    pad = t < 8
    L = 16 if pad else min(L, t)
    lr = t if pad else L
    nch = 1 if pad else t // L
    H = ML_HEADS
    bcol = jnp.zeros((1, LANES), F32).at[0, :2 * H].set(gate_b.reshape(-1).astype(F32))
    brow = gate_b.reshape(2 * H, 1).astype(F32)
    specs = [pl.BlockSpec((None, ns, lr, LANES), lambda i, c: (i, 0, c, 0)),
             pl.BlockSpec((None, lr, LANES), lambda i, c: (i, c, 0)),
             pl.BlockSpec((None, 2 * H, lr), lambda i, c: (i, 0, c)),
             pl.BlockSpec((1, LANES), lambda i, c: (0, 0)),
             pl.BlockSpec((2 * H, 1), lambda i, c: (0, 0))]
    args = [z_slab, gates_col, gates_row, bcol, brow]
    if state is not None:
        c0, n0, m0 = state
        n0p = jnp.zeros((b, 8, LANES), F32).at[:, :H].set(n0.astype(F32))
        m0p = jnp.zeros((b, 8, LANES), F32).at[:, :H].set(jnp.broadcast_to(m0.astype(F32)[..., None], (b, H, LANES)))
        specs += [pl.BlockSpec((None, H, ML_DV, ML_DK), lambda i, c: (i, 0, 0, 0)),
                  pl.BlockSpec((None, 8, LANES), lambda i, c: (i, 0, 0)),
                  pl.BlockSpec((None, 8, LANES), lambda i, c: (i, 0, 0))]
        args += [c0.astype(F32), n0p, m0p]
    scratch = [pltpu.VMEM((H, ML_DV, ML_DK), F32), pltpu.VMEM((8, LANES), F32), pltpu.VMEM((8, LANES), F32)]
    if pad:
        scratch += [pltpu.VMEM((ns, L, LANES), F32), pltpu.VMEM((L, LANES), F32), pltpu.VMEM((2 * H, L), F32)]
    kern = functools.partial(_mlstm_kernel, L=L, t_real=t if pad else L, has_state=state is not None, pad=pad)
    h_slab, c_f, n_f, m_f = pl.pallas_call(
        kern, grid=(b, nch), in_specs=specs,
        out_specs=[pl.BlockSpec((None, 2 * H, lr, LANES), lambda i, c: (i, 0, c, 0)),
                   pl.BlockSpec((None, H, ML_DV, ML_DK), lambda i, c: (i, 0, 0, 0)),
                   pl.BlockSpec((None, 8, LANES), lambda i, c: (i, 0, 0)),
                   pl.BlockSpec((None, 8, LANES), lambda i, c: (i, 0, 0))],
        out_shape=[jax.ShapeDtypeStruct((b, 2 * H, t, LANES), BF16),
                   jax.ShapeDtypeStruct((b, H, ML_DV, ML_DK), F32),
                   jax.ShapeDtypeStruct((b, 8, LANES), F32), jax.ShapeDtypeStruct((b, 8, LANES), F32)],
        scratch_shapes=scratch, compiler_params=_cparams('arbitrary', 'arbitrary'), name='mlstm')(*args)
    return h_slab, c_f, n_f[:, :H], m_f[:, :H, 0]


def _hgrn_levels(L):
    n_lev = int(math.log2(L))
    t = np.arange(L)
    pall = np.zeros((n_lev * L, L), np.float32)
    lmask = np.zeros((n_lev, L, L), np.float32)
    for lev in range(n_lev):
        w = L >> lev
        mid = (t // w) * w + w // 2
        pall[lev * L + t, mid - 1] = 1.0
        same = (t[:, None] // w) == (t[None, :] // w)
        lmask[lev] = same & ((t[:, None] % w) >= w // 2) & ((t[None, :] % w) < w // 2)
    return jnp.asarray(pall, BF16), jnp.asarray(lmask)


def _hgrn_kernel(*refs, L, t_real, has_state, pad):
    it = iter(refs)
    z_ref, lf_ref, gn_ref, pall_ref, lmask_ref = next(it), next(it), next(it), next(it), next(it)
    if has_state:
        s0_ref = next(it)
    o_ref, s_out = next(it), next(it)
    st_scr = next(it)
    if pad:
        zp_ref = next(it)
    ci = pl.program_id(1)
    nchunks = pl.num_programs(1)
    H = HG_HEADS
    n_lev = lmask_ref.shape[0]

    @pl.when(ci == 0)
    def _():
        for h in range(H):
            st_scr[h] = s0_ref[h].T if has_state else jnp.zeros((HG_DV, HG_DK), F32)

    if pad:
        @pl.when((pl.program_id(0) == 0) & (ci == 0))
        def _():
            zp_ref[...] = jnp.zeros(zp_ref.shape, F32)

        zp_ref[:, 0:t_real, :] = z_ref[...]
        z_ref = zp_ref

    rows = lax.broadcasted_iota(jnp.int32, (L, LANES), 0)
    eye = lax.broadcasted_iota(jnp.int32, (L, L), 0) == lax.broadcasted_iota(jnp.int32, (L, L), 1)
    gn = gn_ref[...]

    def head(h, s_t, out):
        q = z_ref[h]
        zf = z_ref[H + h]
        v = z_ref[2 * H + h]
        gate = z_ref[3 * H + h]
        lower = lf_ref[pl.ds(h, 1), :]
        fb = lf_ref[pl.ds(H + h, 1), :]
        f = lower + (1.0 - lower) * _sigmoid(zf + fb)
        lf = jnp.log(f)
        k = 1.0 - f
        if t_real < L:
            lf = jnp.where(rows < t_real, lf, 0.0)
            k = jnp.where(rows < t_real, k, 0.0)
        bcum = _cumsum(lf, 0)
        lev0 = n_lev - max(1, (t_real - 1).bit_length())
        hi = bcum.astype(BF16)
        rem = bcum - hi.astype(F32)
        mid = rem.astype(BF16)
        low = (rem - mid.astype(F32)).astype(BF16)
        pieces = jnp.concatenate([hi, mid, low], axis=1)
        yield
        picked = jnp.dot(pall_ref[lev0 * L:, :], pieces, preferred_element_type=F32)
        refs_all = picked[:, :HG_DK] + picked[:, HG_DK:2 * HG_DK] + picked[:, 2 * HG_DK:]
        att = jnp.where(eye, jnp.sum(q * k, axis=1, keepdims=True), 0.0)
        factors = []
        for lev in range(lev0, n_lev):
            r = refs_all[(lev - lev0) * L:(lev - lev0 + 1) * L]
            factors.append((q * jnp.exp(jnp.minimum(bcum - r, 0.0)), k * jnp.exp(jnp.minimum(r - bcum, 0.0))))
        q_dec = q * jnp.exp(bcum)
        bl = bcum[L - 1:L, :]
        k_dec = k * jnp.exp(bl - bcum)
        v_t = v.T
        yield
        for lev, (qt, kt) in zip(range(lev0, n_lev), factors):
            att = att + _dot_nt(qt, kt) * lmask_ref[lev]
        o_state = _dot_nt(q_dec, s_t)
        upd = _dot(v_t, k_dec)
        yield
        o = _dot(att, v) + o_state
        on = _rms(o, gn) * _silu(gate)
        o_ref[h] = on[0:o_ref.shape[1]].astype(o_ref.dtype)
        out.append(jnp.exp(bl) * s_t + upd)

    s_new = []
    gens = [head(h, st_scr[h], s_new) for h in range(H)]
    for _ in range(4):
        for gen in gens:
            next(gen, None)
    for h in range(H):
        st_scr[h] = s_new[h]

    @pl.when(ci == nchunks - 1)
    def _():
        for h in range(H):
            s_out[h] = st_scr[h].T


def _hgrn(z_slab, lower, f_b, g_norm, state=None, L=128):
    b, ns, t, _ = z_slab.shape
    pad = t < 8
    L = 16 if pad else min(L, t)
    lr = t if pad else L
    nch = 1 if pad else t // L
    H = HG_HEADS
    pall, lmask = _hgrn_levels(L)
    lowfb = jnp.concatenate([lower.reshape(H, HG_DK), f_b.reshape(H, HG_DK)], axis=0).astype(F32)
    specs = [pl.BlockSpec((None, ns, lr, LANES), lambda i, c: (i, 0, c, 0)),
             pl.BlockSpec((2 * H, LANES), lambda i, c: (0, 0)),
             pl.BlockSpec((1, LANES), lambda i, c: (0, 0)),
             pl.BlockSpec(pall.shape, lambda i, c: (0, 0)),
             pl.BlockSpec(lmask.shape, lambda i, c: (0, 0, 0))]
    args = [z_slab, lowfb, g_norm.reshape(1, HG_DV).astype(F32), pall, lmask]
    if state is not None:
        specs.append(pl.BlockSpec((None, H, HG_DK, HG_DV), lambda i, c: (i, 0, 0, 0)))
        args.append(state.astype(F32))
    scratch = [pltpu.VMEM((H, HG_DV, HG_DK), F32)]
    if pad:
        scratch.append(pltpu.VMEM((ns, L, LANES), F32))
    kern = functools.partial(_hgrn_kernel, L=L, t_real=t if pad else L, has_state=state is not None, pad=pad)
    return pl.pallas_call(
        kern, grid=(b, nch), in_specs=specs,
        out_specs=[pl.BlockSpec((None, H, lr, LANES), lambda i, c: (i, 0, c, 0)),
                   pl.BlockSpec((None, H, HG_DK, HG_DV), lambda i, c: (i, 0, 0, 0))],
        out_shape=[jax.ShapeDtypeStruct((b, H, t, LANES), BF16),
                   jax.ShapeDtypeStruct((b, H, HG_DK, HG_DV), F32)],
        scratch_shapes=scratch, compiler_params=_cparams('arbitrary', 'arbitrary'), name='hgrn')(*args)


def _mla_attn_kernel(ql_ref, qr_ref, lc_ref, lr_ref, o_ref, lct_ref, qlt_ref, qrt_ref, m_ref, l_ref, acc_ref,
                     *, tq, t_len):
    qi = pl.program_id(1)
    q0 = qi * tq
    kt = lct_ref.shape[2]
    scale = (MLA_NOPE + MLA_ROPE) ** -0.5
    mq = MLA_HEADS * tq
    per_slab = LANES // MLA_ROPE

    @pl.when(qi == 0)
    def _():
        for jt in range(t_len // kt):
            lct_ref[jt] = lc_ref[jt * kt:(jt + 1) * kt, :].T.astype(BF16)

    zeros_r = jnp.zeros((LANES - MLA_ROPE, tq), F32)
    for h in range(MLA_HEADS):
        qlt_ref[:, h * tq:(h + 1) * tq] = ql_ref[:, h * MLA_KV_LORA:(h + 1) * MLA_KV_LORA].astype(F32).T.astype(BF16)
        if h % per_slab == 0:
            slab_t = qr_ref[:, (h // per_slab) * LANES:(h // per_slab + 1) * LANES].astype(F32).T
        rope_t = slab_t[(h % per_slab) * MLA_ROPE:(h % per_slab + 1) * MLA_ROPE]
        qrt_ref[:, h * tq:(h + 1) * tq] = jnp.concatenate([rope_t, zeros_r], axis=0).astype(BF16)
    q_lat = qlt_ref[...]
    q_rope = qrt_ref[...]
    _online_init(m_ref, l_ref, acc_ref)

    def tile(j, masked):
        start = pl.multiple_of(j * kt, kt)
        st = (_dot(lc_ref[pl.ds(start, kt), :], q_lat) + _dot(lr_ref[pl.ds(start, kt), :], q_rope)) * (
            scale * math.log2(math.e))
        if masked:
            krow = lax.broadcasted_iota(jnp.int32, (kt, mq), 0)
            qpos = q0 + (lax.broadcasted_iota(jnp.int32, (kt, mq), 1) & (tq - 1))
            bias = jnp.where(start + krow <= qpos, 0.0, NEG)
        else:
            bias = None
        _online_update_t(st, bias, lct_ref[j], m_ref, l_ref, acc_ref, base2=True)

    def body(j, carry):
        tile(j, False)
        return carry

    n_full = q0 // kt
    lax.fori_loop(0, n_full, body, 0)
    tile(n_full, True)
    o = _online_result(l_ref, acc_ref)
    for h in range(MLA_HEADS):
        o_ref[:, h * MLA_KV_LORA:(h + 1) * MLA_KV_LORA] = o[:, h * tq:(h + 1) * tq].T.astype(o_ref.dtype)


def _mla_attn(q_lat, q_rope, lat, b, t, tq=256):
    tq = min(tq, t)
    kt = min(256, t)
    wl = MLA_HEADS * MLA_KV_LORA
    wr = MLA_HEADS * MLA_ROPE
    lat3 = lat.reshape(b, t, lat.shape[-1])
    c_blk = MLA_Q_LORA // MLA_KV_LORA
    r_blk = (MLA_Q_LORA + MLA_KV_LORA) // LANES
    return pl.pallas_call(
        functools.partial(_mla_attn_kernel, tq=tq, t_len=t), grid=(b, t // tq),
        in_specs=[pl.BlockSpec((None, tq, wl), lambda i, j: (i, j, 0)),
                  pl.BlockSpec((None, tq, wr), lambda i, j: (i, j, 0)),
                  pl.BlockSpec((None, t, MLA_KV_LORA), lambda i, j: (i, 0, c_blk)),
                  pl.BlockSpec((None, t, LANES), lambda i, j: (i, 0, r_blk))],
        out_specs=pl.BlockSpec((None, tq, wl), lambda i, j: (i, j, 0)),
        out_shape=jax.ShapeDtypeStruct((b, t, wl), BF16),
        scratch_shapes=[pltpu.VMEM((t // kt, MLA_KV_LORA, kt), BF16),
                        pltpu.VMEM((MLA_KV_LORA, MLA_HEADS * tq), BF16), pltpu.VMEM((LANES, MLA_HEADS * tq), BF16),
                        pltpu.VMEM((1, MLA_HEADS * tq), F32), pltpu.VMEM((1, MLA_HEADS * tq), F32),
                        pltpu.VMEM((MLA_KV_LORA, MLA_HEADS * tq), F32)],
        compiler_params=_cparams('arbitrary', 'arbitrary'), name='mla_attn')(
            q_lat.reshape(b, t, wl), q_rope.reshape(b, t, wr), lat3, lat3).reshape(b * t, wl)


def _pad_cols(w, n):
    return jnp.pad(w, ((0, 0), (0, n - w.shape[1])))


def _prep_nsa(w_in, pe, w1, w2, w_out):
    nq = NSA_HEADS * NSA_DH
    kvw = 2 * NSA_G * NSA_DH
    w2 = w2.astype(BF16)
    z = jnp.zeros_like(w2)
    w2p = jnp.stack([jnp.concatenate([w2, z], axis=-1), jnp.concatenate([z, w2], axis=-1)], axis=1)
    return dict(q=w_in[:, :nq].astype(BF16), kv=w_in[:, nq:nq + 3 * kvw].astype(BF16),
                g=_pad_cols(w_in[:, nq + 3 * kvw:], LANES).astype(BF16),
                pe=pe.astype(F32).reshape(2, CMP_LEN // CMP_STRIDE, CMP_STRIDE * NSA_DH),
                w1=w1.astype(BF16), w2p=w2p, out=w_out.astype(BF16))


def _prep_ml(w_in, gate_b, w_out):
    a = 2 * ML_HEADS * ML_DK + ML_HEADS * ML_DV
    main = jnp.concatenate([w_in[:, :a], w_in[:, a + 2 * ML_HEADS:]], axis=1)
    return dict(main=main.astype(BF16), gate=_pad_cols(w_in[:, a:a + 2 * ML_HEADS], LANES).astype(BF16),
                gate_b=gate_b, out=w_out.astype(BF16))


def _prep_mla(w_in, q_norm, kv_norm, w_uq, w_uk, w_uv, w_out):
    uq = w_uq.reshape(MLA_Q_LORA, MLA_HEADS, MLA_NOPE + MLA_ROPE)
    return dict(w_in=_pad_cols(w_in, 7 * LANES).astype(BF16),
                q_norm=q_norm.reshape(1, -1).astype(F32), kv_norm=kv_norm.reshape(1, -1).astype(F32),
                uq_nope=uq[:, :, :MLA_NOPE].reshape(MLA_Q_LORA, -1).astype(BF16),
                uq_rope=uq[:, :, MLA_NOPE:].reshape(MLA_Q_LORA, -1).astype(BF16),
                ukt=jnp.transpose(w_uk, (1, 2, 0)).astype(BF16),
                uv=jnp.transpose(w_uv, (1, 0, 2)).astype(BF16), out=w_out.astype(BF16))


def _sig_tile2(z, j):
    return jnp.where(j == 2, _sigmoid(z), z)


def _rope64_epi(z, j, c, s):
    return _rope_tile(z, c, s, NSA_DH // 2)


def _rope_kv_epi(z, j, c, s):
    return jnp.concatenate([_rope_tile(z[:, :LANES], c, s, NSA_DH // 2), z[:, LANES:]], axis=1)


def _rope32_epi(z, j, c, s):
    return _rope_tile(z, c, s, MLA_ROPE // 2)


def _sigmoid_epi(z, j):
    return _sigmoid(z)


def _mla_in_epi(z, j, qn, kvn, c, s):
    a, bnd = MLA_Q_LORA, MLA_Q_LORA + MLA_KV_LORA
    return jnp.concatenate([_rms(z[:, :a], qn), _rms(z[:, a:bnd], kvn),
                            _rope_tile(z[:, bnd:], c, s, MLA_ROPE // 2)], axis=1)


def _const_aux(a):
    return (a, a.shape, lambda i, j: (0,) * a.ndim)


def _nsa_fresh(h, g, w, b, t, tabs):
    cos, sin = tabs[NSA_DH // 2]
    tm = min(1024, t)
    rope = _rope_aux(cos, sin, tm, t // tm)
    q = _mm(h, w['q'], g=g, epi=_rope64_epi, aux=rope, out_dtype=BF16, layout='slab', seq=t)
    kv = _mm(h, w['kv'], g=g, epi=_rope_kv_epi, aux=rope, layout='tiles', seq=t, tn=256)
    gates = _mm(h, w['g'], g=g, epi=_sigmoid_epi, seq=t)
    rows = [kv[i].reshape(b, t, 256) for i in range(3)]
    kc, vc = _nsa_compress(rows[0], w['pe'], w['w1'], w['w2p'])
    o = _nsa_attn(q, gates.reshape(b, t, LANES), kc, vc, rows[1], rows[2])
    h = _mm(o, w['out'], res=h, x_slab_seq=t)
    return h, rows


def _ml_fresh(h, g, w, b, t):
    z = _mm(h, w['main'], g=g, epi=_sig_tile2, layout='slab', seq=t)
    gc = _mm(h, w['gate'], g=g, seq=t).reshape(b, t, LANES)
    gr = jnp.swapaxes(gc[:, :, :2 * ML_HEADS], 1, 2)
    hs, c_f, n_f, m_f = _mlstm(z, gc, gr, w['gate_b'])
    return _mm(hs, w['out'], res=h, x_slab_seq=t), (c_f, n_f, m_f)


def _mla_fresh(h, g, w, b, t, tabs):
    cos, sin = tabs[MLA_ROPE // 2]
    tm = min(1024, t)
    rope = _rope_aux(cos, sin, tm, t // tm)
    lat = _mm(h, w['w_in'], g=g, epi=_mla_in_epi,
              aux=[_const_aux(w['q_norm']), _const_aux(w['kv_norm'])] + rope, seq=t)
    qn = _mm(lat, w['uq_nope'], x_cols=(MLA_Q_LORA, 0), out_dtype=BF16, seq=t)
    qr = _mm(lat, w['uq_rope'], x_cols=(MLA_Q_LORA, 0), epi=_rope32_epi, aux=rope, out_dtype=BF16, seq=t)
    ql = _headmm(qn, w['ukt'])
    ol = _mla_attn(ql, qr, lat, b, t)
    o = _headmm(ol, w['uv'])
    h = _mm(o, w['out'], res=h)
    new_lat = lat[:, MLA_Q_LORA:MLA_Q_LORA + MLA_KV_LORA + MLA_ROPE].reshape(b, t, -1)
    return h, new_lat


def _hg_fresh(h, g, w_in, lower, f_b, g_norm, w_out, b, t):
    z = _mm(h, w_in, g=g, layout='slab', seq=t)
    os_, s_f = _hgrn(z, lower, f_b, g_norm)
    return _mm(os_, w_out, res=h, x_slab_seq=t), s_f


PAGES_PER_STEP = 16


def _page_specs(width, n):
    return [pl.BlockSpec((None, width, PAGE), functools.partial(
        lambda i, s, pt, k: (pt[i, s * n + k], 0, 0), k=k)) for k in range(n)]


def _pages_feature_major(pool):
    nd = pool.ndim
    return jnp.transpose(pool, (0,) + tuple(range(2, nd)) + (1,)).reshape(pool.shape[0], -1, pool.shape[1])


def _stack8(x):
    return jnp.concatenate([x] * (NSA_HEADS // NSA_G), axis=0)


def _nsa_cmp_past_kernel(pt_ref, *refs, n_pg, t, start, nc):
    pages = refs[:n_pg]
    q_ref, pe_ref, w1_ref, w2p_ref, ocmp_ref, imp_ref, k_scr, v_scr = refs[n_pg:]
    s_i = pl.program_id(1)
    for k in range(n_pg):
        off = pl.multiple_of((s_i * n_pg + k) * PAGE, PAGE)
        rows = pages[k][...].T
        k_scr[pl.ds(off, PAGE), :] = rows[:, 0:LANES]
        v_scr[pl.ds(off, PAGE), :] = rows[:, LANES:2 * LANES]

    @pl.when(s_i == pl.num_programs(1) - 1)
    def _():
        n_seg = k_scr.shape[0] // CMP_STRIDE
        kc, vc = _compress_rows(k_scr, v_scr, n_seg, pe_ref, w1_ref, w2p_ref)
        q = q_ref[...]
        r_n = q.shape[0]
        cidx = lax.broadcasted_iota(jnp.int32, (r_n, n_seg), 1)
        qpos = start + lax.rem(lax.broadcasted_iota(jnp.int32, (r_n, n_seg), 0), t)
        mask = (cidx < nc) & (cidx * CMP_STRIDE + (CMP_LEN - 1) <= qpos)
        p = _softmax_tile(_dot_nt(q, kc) * (NSA_DH ** -0.5), mask)
        ocmp_ref[...] = _dot(p, vc)
        gt = NSA_G * t
        imp = p[0:gt]
        for hh in range(1, r_n // gt):
            imp = imp + p[hh * gt:(hh + 1) * gt]
        imp_ref[...] = imp


def _nsa_cmp_past(page_table, cache, q_ext, w, t, start):
    b, npg = page_table.shape
    n_pg = min(PAGES_PER_STEP, npg)
    r_n = q_ext.shape[1]
    n_seg = npg * PAGE // CMP_STRIDE
    nc = (start + t) // CMP_STRIDE - CMP_LEN // CMP_STRIDE + 1
    full = lambda a: pl.BlockSpec(a.shape, lambda i, s, pt: (0,) * a.ndim)
    gs = pltpu.PrefetchScalarGridSpec(
        num_scalar_prefetch=1, grid=(b, npg // n_pg),
        in_specs=_page_specs(256, n_pg) + [pl.BlockSpec((None, r_n, LANES), lambda i, s, pt: (i, 0, 0)),
                                           full(w['pe']), full(w['w1']), full(w['w2p'])],
        out_specs=[pl.BlockSpec((None, r_n, LANES), lambda i, s, pt: (i, 0, 0)),
                   pl.BlockSpec((None, NSA_G * t, n_seg), lambda i, s, pt: (i, 0, 0))],
        scratch_shapes=[pltpu.VMEM((npg * PAGE, LANES), F32), pltpu.VMEM((npg * PAGE, LANES), F32)])
    return pl.pallas_call(
        functools.partial(_nsa_cmp_past_kernel, n_pg=n_pg, t=t, start=start, nc=nc), grid_spec=gs,
        out_shape=[jax.ShapeDtypeStruct((b, r_n, LANES), F32),
                   jax.ShapeDtypeStruct((b, NSA_G * t, n_seg), F32)],
        compiler_params=_cparams('arbitrary', 'arbitrary'), name='nsa_cmp_past')(
            page_table, *([cache] * n_pg), q_ext, w['pe'], w['w1'], w['w2p']), nc


def _nsa_select_kernel(imp_ref, ovt_ref, pos_ref, sel_ref, sc_scr, rank_scr, *, ns, n_sel):
    shape = sc_scr.shape
    blk = lax.broadcasted_iota(jnp.int32, shape, 0)
    cur = pos_ref[...] // SLC_BLOCK
    forced = (blk == 0) | (blk == cur) | (blk == cur - 1)
    valid = blk <= cur
    sc = _dot_nt_f32(ovt_ref[...], imp_ref[...])
    sc_scr[...] = jnp.where(valid, sc + jnp.where(forced, FORCE_BONUS, 0.0), NEG)
    rank_scr[...] = jnp.zeros(shape, F32)

    def body(j, carry):
        sc_all = sc_scr[...]
        rj = sc_scr[pl.ds(j, 1), :]
        beats = (rj > sc_all) | ((rj == sc_all) & (blk > j))
        rank_scr[...] += jnp.where(beats, 1.0, 0.0)
        return carry

    lax.fori_loop(0, ns, body, 0)
    sel_ref[...] = jnp.where(valid & (rank_scr[...] < n_sel), 1.0, 0.0)


def _nsa_select_past(imp, nc, t, start):
    rows, n_seg = imp.shape
    ns = -(-(start + t) // SLC_BLOCK)
    nsp = -(-ns // 8) * 8
    ovt = _overlap_t(nc, ns, n_seg, nsp)
    pos = (start + jnp.arange(rows, dtype=jnp.int32) % t).reshape(1, rows)
    return pl.pallas_call(
        functools.partial(_nsa_select_kernel, ns=ns, n_sel=min(SLC_TOPN, ns)),
        out_shape=jax.ShapeDtypeStruct((nsp, rows), F32),
        scratch_shapes=[pltpu.VMEM((nsp, rows), F32), pltpu.VMEM((nsp, rows), F32)],
        compiler_params=pltpu.CompilerParams(vmem_limit_bytes=VMEM_LIMIT), name='nsa_select')(imp, ovt, pos)


def _nsa_slcwin_past_kernel(pt_ref, *refs, n_pg, t, wl):
    pages = refs[:n_pg]
    (msk_ref, q_ref, ns_ref, nw_ref, wb_ref, nf_ref, gt_ref, ocmp_ref, o_ref,
     m_ref, l_ref, acc_ref, oslc_scr, pad_scr) = refs[n_pg:]
    s_i = pl.program_id(1)
    scale = NSA_DH ** -0.5
    q = q_ref[...]
    r_n = q.shape[0]

    @pl.when(s_i == 0)
    def _():
        _online_init(m_ref, l_ref, acc_ref)

    s = jnp.concatenate([_dot(q, pg[0:LANES, :]) * scale for pg in pages], axis=1)
    v_t = jnp.concatenate([pg[LANES:2 * LANES, :] for pg in pages], axis=1)
    mask = _stack8(msk_ref[...].astype(F32)) > 0.5
    _online_update(s, mask, v_t, m_ref, l_ref, acc_ref, v_t=True)

    @pl.when(s_i == pl.num_programs(1) - 1)
    def _():
        col = lax.broadcasted_iota(jnp.int32, (r_n, PAGE), 1)
        tq = lax.rem(lax.broadcasted_iota(jnp.int32, (r_n, PAGE), 0), t)
        new_ok = (col < t) & (col <= tq)

        def padded(ref):
            pad_scr[...] = jnp.zeros(pad_scr.shape, F32)
            pad_scr[0:t, :] = ref[...]
            return pad_scr[...]

        rows = padded(ns_ref)
        _online_update(_dot_nt(q, rows[:, 0:LANES]) * scale, new_ok & (_stack8(nf_ref[...]) > 0.5),
                       rows[:, LANES:], m_ref, l_ref, acc_ref)
        oslc_scr[...] = _online_result(l_ref, acc_ref)

        _online_init(m_ref, l_ref, acc_ref)
        colw = lax.broadcasted_iota(jnp.int32, (r_n, wl), 1)
        tqw = lax.rem(lax.broadcasted_iota(jnp.int32, (r_n, wl), 0), t)
        _online_update(_dot(q, wb_ref[0:LANES, :]) * scale, colw > tqw + (wl - WINDOW),
                       wb_ref[LANES:2 * LANES, :], m_ref, l_ref, acc_ref, v_t=True)
        rows = padded(nw_ref)
        _online_update(_dot_nt(q, rows[:, 0:LANES]) * scale, new_ok, rows[:, LANES:], m_ref, l_ref, acc_ref)
        o_win = _online_result(l_ref, acc_ref)
        gt = gt_ref[...]
        o_ref[...] = gt[:, 0:1] * ocmp_ref[...] + gt[:, 1:2] * oslc_scr[...] + gt[:, 2:3] * o_win


def _nsa_slcwin_past(page_table, cache, key_mask, q_ext, new_slc, new_win, win_buf, new_flag, gates_r, ocmp, t):
    b, npg = page_table.shape
    n_pg = min(PAGES_PER_STEP, npg)
    r_n = q_ext.shape[1]
    gtn = NSA_G * t
    wl = win_buf.shape[2]
    per_b = lambda shp: pl.BlockSpec((None,) + shp, lambda i, s, pt: (i,) + (0,) * len(shp))
    gs = pltpu.PrefetchScalarGridSpec(
        num_scalar_prefetch=1, grid=(b, npg // n_pg),
        in_specs=_page_specs(256, n_pg) + [
            pl.BlockSpec((None, gtn, n_pg * PAGE), lambda i, s, pt: (i, 0, s)),
            per_b((r_n, LANES)), per_b((t, 256)), per_b((t, 256)), per_b((256, wl)),
            per_b((gtn, LANES)), per_b((r_n, LANES)), per_b((r_n, LANES))],
        out_specs=per_b((r_n, LANES)),
        scratch_shapes=[pltpu.VMEM((r_n, 1), F32), pltpu.VMEM((r_n, 1), F32), pltpu.VMEM((r_n, LANES), F32),
                        pltpu.VMEM((r_n, LANES), F32), pltpu.VMEM((PAGE, 256), F32)])
    return pl.pallas_call(
        functools.partial(_nsa_slcwin_past_kernel, n_pg=n_pg, t=t, wl=wl), grid_spec=gs,
        out_shape=jax.ShapeDtypeStruct((b, r_n, LANES), F32),
        compiler_params=_cparams('arbitrary', 'arbitrary'), name='nsa_slcwin_past')(
            page_table, *([cache] * n_pg), key_mask, q_ext, new_slc, new_win, win_buf, new_flag, gates_r, ocmp)


def _mla_past_kernel(pt_ref, *refs, n_pg, t):
    pages = refs[:n_pg]
    ql_ref, qr_ref, new_ref, o_ref, m_ref, l_ref, acc_ref, pad_scr = refs[n_pg:]
    s_i = pl.program_id(1)
    scale = (MLA_NOPE + MLA_ROPE) ** -0.5
    ql = ql_ref[...]
    qr = qr_ref[...]
    r_n = ql.shape[0]

    @pl.when(s_i == 0)
    def _():
        _online_init(m_ref, l_ref, acc_ref)

    def scores(rows):
        kc = rows[:, 0:MLA_KV_LORA].astype(BF16)
        kr = rows[:, MLA_KV_LORA:MLA_KV_LORA + MLA_ROPE]
        return (_dot_nt(ql, kc) + _dot_nt(qr, kr)) * scale, kc

    kc_t = [pg[0:MLA_KV_LORA, :].astype(BF16) for pg in pages]
    s = [(_dot(ql, kc) + _dot(qr, pg[MLA_KV_LORA:MLA_KV_LORA + MLA_ROPE, :])) * scale
         for kc, pg in zip(kc_t, pages)]
    _online_update(jnp.concatenate(s, axis=1), None, jnp.concatenate(kc_t, axis=1),
                   m_ref, l_ref, acc_ref, v_t=True)

    @pl.when(s_i == pl.num_programs(1) - 1)
    def _():
        pad_scr[...] = jnp.zeros(pad_scr.shape, F32)
        pad_scr[0:t, :] = new_ref[...]
        s, kc = scores(pad_scr[...])
        col = lax.broadcasted_iota(jnp.int32, (r_n, PAGE), 1)
        tq = lax.rem(lax.broadcasted_iota(jnp.int32, (r_n, PAGE), 0), t)
        _online_update(s, (col < t) & (col <= tq), kc, m_ref, l_ref, acc_ref)
        o_ref[...] = _online_result(l_ref, acc_ref).astype(o_ref.dtype)


def _mla_past_attn(page_table, cache, ql, qr, new_lat, t):
    b, npg = page_table.shape
    n_pg = min(PAGES_PER_STEP, npg)
    r_n = ql.shape[1]
    width = cache.shape[1]
    per_b = lambda shp: pl.BlockSpec((None,) + shp, lambda i, s, pt: (i,) + (0,) * len(shp))
    gs = pltpu.PrefetchScalarGridSpec(
        num_scalar_prefetch=1, grid=(b, npg // n_pg),
        in_specs=_page_specs(width, n_pg) + [per_b((r_n, MLA_KV_LORA)), per_b((r_n, MLA_ROPE)), per_b((t, width))],
        out_specs=per_b((r_n, MLA_KV_LORA)),
        scratch_shapes=[pltpu.VMEM((r_n, 1), F32), pltpu.VMEM((r_n, 1), F32), pltpu.VMEM((r_n, MLA_KV_LORA), F32),
                        pltpu.VMEM((PAGE, width), F32)])
    return pl.pallas_call(
        functools.partial(_mla_past_kernel, n_pg=n_pg, t=t), grid_spec=gs,
        out_shape=jax.ShapeDtypeStruct((b, r_n, MLA_KV_LORA), BF16),
        compiler_params=_cparams('arbitrary', 'arbitrary'), name='mla_past')(
            page_table, *([cache] * n_pg), ql, qr, new_lat)


def _to_slab(z, b, t):
    return jnp.transpose(z.reshape(b, t, -1, LANES), (0, 2, 1, 3))


def _from_slab(s, b, t):
    return jnp.transpose(s, (0, 2, 1, 3)).reshape(b * t, -1)


def _nsa_past(h, g, w, b, t, tabs, start, past, occ):
    assert t < CMP_STRIDE and start % SLC_BLOCK == 0
    cos, sin = tabs[NSA_DH // 2]
    m = b * t
    rope = _rope_aux(cos, sin, min(1024, m), None)
    q = _mm(h, w['q'], g=g, epi=_rope64_epi, aux=rope, out_dtype=BF16)
    kv = _mm(h, w['kv'], g=g, epi=_rope_kv_epi, aux=rope, layout='tiles', tn=256)
    gates = _mm(h, w['g'], g=g, epi=_sigmoid_epi)
    rows = [kv[i].reshape(b, t, 256) for i in range(3)]
    hpg = NSA_HEADS // NSA_G
    r_n = hpg * NSA_G * t
    q5 = jnp.transpose(q.reshape(b, t, NSA_G, hpg, NSA_DH), (0, 3, 2, 1, 4))
    eye = jnp.eye(NSA_G, dtype=q5.dtype)
    q_ext = (q5[:, :, :, :, None, :] * eye[None, None, :, None, :, None]).reshape(b, r_n, NSA_G * NSA_DH)
    pt = past['page_table']
    width = 2 * NSA_G * NSA_DH
    cmp_cache = _pages_feature_major(past['nsa_cmp'][occ])
    slc_cache = _pages_feature_major(past['nsa_slc'][occ])
    win_buf = past['nsa_win'][occ].reshape(b, -1, width)
    win_buf_t = _pages_feature_major(past['nsa_win'][occ])
    (ocmp, imp), nc = _nsa_cmp_past(pt, cmp_cache, q_ext, w, t, start)
    gtn = NSA_G * t
    sel = _nsa_select_past(imp.reshape(b * gtn, -1), nc, t, start)
    sel = jnp.transpose(sel).reshape(b, gtn, -1)
    n_past_blk = start // SLC_BLOCK
    key_mask = jnp.repeat(sel[:, :, :n_past_blk], SLC_BLOCK, axis=-1).astype(BF16)
    new_flag = jnp.broadcast_to(sel[:, :, n_past_blk:n_past_blk + 1], (b, gtn, LANES))
    g5 = jnp.transpose(gates[:, :3 * NSA_HEADS].reshape(b, t, 3, NSA_G, hpg), (0, 4, 3, 1, 2))
    gates_r = jnp.pad(g5.reshape(b, r_n, 3), ((0, 0), (0, 0), (0, LANES - 3)))
    o = _nsa_slcwin_past(pt, slc_cache, key_mask, q_ext, rows[1], rows[2], win_buf_t, new_flag, gates_r, ocmp, t)
    o5 = o.reshape(b, hpg, NSA_G, t, NSA_G, NSA_DH)
    o = jnp.stack([o5[:, :, gi, :, gi, :] for gi in range(NSA_G)], axis=2)
    o = jnp.transpose(o, (0, 3, 2, 1, 4)).reshape(m, NSA_HEADS * NSA_DH).astype(BF16)
    h = _mm(o, w['out'], res=h)
    win = jnp.concatenate([win_buf, rows[2]], axis=1)[:, -win_buf.shape[1]:]
    return h, rows, win


def _ml_past(h, g, w, b, t, past, occ):
    z = _to_slab(_mm(h, w['main'], g=g, epi=_sig_tile2), b, t)
    gc = _mm(h, w['gate'], g=g).reshape(b, t, LANES)
    gr = jnp.swapaxes(gc[:, :, :2 * ML_HEADS], 1, 2)
    hs, c_f, n_f, m_f = _mlstm(z, gc, gr, w['gate_b'],
                               state=(past['ml_C'][occ], past['ml_n'][occ], past['ml_m'][occ]))
    return _mm(_from_slab(hs, b, t), w['out'], res=h), (c_f, n_f, m_f)


def _mla_past(h, g, w, b, t, tabs, past, occ):
    cos, sin = tabs[MLA_ROPE // 2]
    m = b * t
    rope = _rope_aux(cos, sin, min(1024, m), None)
    lat = _mm(h, w['w_in'], g=g, epi=_mla_in_epi,
              aux=[_const_aux(w['q_norm']), _const_aux(w['kv_norm'])] + rope)
    qn = _mm(lat, w['uq_nope'], x_cols=(MLA_Q_LORA, 0), out_dtype=BF16)
    qr = _mm(lat, w['uq_rope'], x_cols=(MLA_Q_LORA, 0), epi=_rope32_epi, aux=rope, out_dtype=BF16)
    ql = _headmm(qn, w['ukt'])
    new_lat = lat[:, MLA_Q_LORA:MLA_Q_LORA + MLA_KV_LORA + MLA_ROPE].reshape(b, t, -1)
    hd = lambda a: jnp.transpose(a.reshape(b, t, MLA_HEADS, -1), (0, 2, 1, 3)).reshape(b, MLA_HEADS * t, -1)
    cache = _pages_feature_major(past['mla'][occ])
    ol = _mla_past_attn(past['page_table'], cache, hd(ql), hd(qr), new_lat, t)
    ol = jnp.transpose(ol.reshape(b, MLA_HEADS, t, -1), (0, 2, 1, 3)).reshape(m, -1)
    o = _headmm(ol, w['uv'])
    return _mm(o, w['out'], res=h), new_lat


def _hg_past(h, g, w_in, lower, f_b, g_norm, w_out, b, t, state):
    z = _to_slab(_mm(h, w_in, g=g), b, t)
    os_, s_f = _hgrn(z, lower, f_b, g_norm, state=state)
    return _mm(_from_slab(os_, b, t), w_out, res=h), s_f


def _prepare(prm):
    depth = prm['norm_mix'].shape[0]
    sm = jax.nn.softmax(prm['hg_lb_logits'].astype(F32), axis=0)
    lower = jnp.cumsum(sm, axis=0) - sm[0]
    w = dict(depth=depth, lower=lower)
    w['nsa'] = [_prep_nsa(prm['nsa_w_in'][o], prm['nsa_cmp_pe'][o], prm['nsa_cmp_w1'][o], prm['nsa_cmp_w2'][o],
                          prm['nsa_w_out'][o]) for o in range(prm['nsa_w_in'].shape[0])]
    w['ml'] = [_prep_ml(prm['ml_w_in'][o], prm['ml_gate_b'][o], prm['ml_w_out'][o])
               for o in range(prm['ml_w_in'].shape[0])]
    w['mla'] = [_prep_mla(prm['mla_w_in'][o], prm['mla_q_norm'][o], prm['mla_kv_norm'][o], prm['mla_w_uq'][o],
                          prm['mla_w_uk'][o], prm['mla_w_uv'][o], prm['mla_w_out'][o])
                for o in range(prm['mla_w_in'].shape[0])]
    w['hg'] = [dict(w_in=prm['hg_w_in'][o].astype(BF16), f_b=prm['hg_f_b'][o], norm=prm['hg_norm'][o],
                    out=prm['hg_w_out'][o].astype(BF16)) for o in range(prm['hg_w_in'].shape[0])]
    w['ffn'] = [dict(up=prm['ffn_w_up'][i].astype(BF16),
                     conv4=jnp.concatenate([prm['ffn_conv_w'][i], prm['ffn_conv_b'][i][None]], axis=0).astype(F32),
                     down=prm['ffn_w_down'][i].astype(BF16)) for i in range(depth)]
    w['ple'] = [dict(proj=prm['ple_w_proj'][i].astype(BF16), gate=prm['ple_w_gate'][i].astype(BF16))
                for i in range(depth)]
    return w


def _trunk(x, p, start, past, prm, w):
    b, t, d = x.shape
    depth = w['depth']
    pos = start + jnp.arange(t, dtype=jnp.int32)
    if past is None:
        tabs = {hf: _rope_tables(pos, hf) for hf in (NSA_DH // 2, MLA_ROPE // 2)}
    else:
        tabs = {hf: tuple(jnp.tile(a, (b, 1)) for a in _rope_tables(pos, hf)) for hf in (NSA_DH // 2, MLA_ROPE // 2)}
    new = {}
    h = x.reshape(b * t, d)
    for i in range(depth):
        kind, occ = i % 4, i // 4
        g = prm['norm_mix'][i]
        if kind == 0:
            if past is None:
                h, rows = _nsa_fresh(h, g, w['nsa'][occ], b, t, tabs)
                win = rows[2][:, -min(WINDOW, t):]
            else:
                h, rows, win = _nsa_past(h, g, w['nsa'][occ], b, t, tabs, start, past, occ)
            for name, r in zip(('nsa_cmp', 'nsa_slc'), rows[:2]):
                new.setdefault(name, []).append(r.reshape(b, t, 2, NSA_G, NSA_DH))
            new.setdefault('nsa_win', []).append(win.reshape(b, win.shape[1], 2, NSA_G, NSA_DH))
        elif kind == 1:
            if past is None:
                h, st = _ml_fresh(h, g, w['ml'][occ], b, t)
            else:
                h, st = _ml_past(h, g, w['ml'][occ], b, t, past, occ)
            for name, s in zip(('ml_C', 'ml_n', 'ml_m'), st):
                new.setdefault(name, []).append(s)
        elif kind == 2:
            if past is None:
                h, lat = _mla_fresh(h, g, w['mla'][occ], b, t, tabs)
            else:
                h, lat = _mla_past(h, g, w['mla'][occ], b, t, tabs, past, occ)
            new.setdefault('mla', []).append(lat)
        else:
            hw = w['hg'][occ]
            if past is None:
                h, s_f = _hg_fresh(h, g, hw['w_in'], w['lower'][i], hw['f_b'], hw['norm'], hw['out'], b, t)
            else:
                h, s_f = _hg_past(h, g, hw['w_in'], w['lower'][i], hw['f_b'], hw['norm'], hw['out'], b, t,
                                  past['hg_S'][occ])
            new.setdefault('hg_S', []).append(s_f)
        fw = w['ffn'][i]
        f = fw['down'].shape[0]
        if past is None:
            h, sa, sg = _ffn(h, prm['norm_ffn'][i], fw['up'], fw['conv4'], fw['down'], t)
            new.setdefault('ffn_conv', []).append(jnp.concatenate([sa, sg], axis=-1))
        else:
            buf = past['ffn_conv'][i]
            zero = jnp.zeros((b, t - 1, 2 * f), F32)
            p1 = jnp.concatenate([buf[:, 1:2], zero], axis=1).reshape(b * t, 2 * f)
            p2 = jnp.concatenate([buf, zero[:, 1:]], axis=1).reshape(b * t, 2 * f)
            h, ua, ug = _ffn(h, prm['norm_ffn'][i], fw['up'], fw['conv4'], fw['down'], t, prev=(p1, p2))
            u = jnp.concatenate([ua, ug], axis=-1).reshape(b, t, 2 * f)
            new.setdefault('ffn_conv', []).append(u[:, -(CONV_W - 1):])
        pw = w['ple'][i]
        h = _ple(h, prm['norm_ple'][i], pw['gate'], p[i].reshape(b * t, -1), pw['proj'],
                 final_g=prm['norm_final'] if i == depth - 1 else None)
    return h.reshape(b, t, d), {k: jnp.stack(v) for k, v in new.items()}


def kernel(x_prompt, x_sample, cache_nsa_cmp_kv, cache_nsa_slc_kv, state_nsa_win_kv, cache_mla_latent,
           state_mlstm_C, state_mlstm_n, state_mlstm_m, state_hgrn_S, state_ffn_conv, page_table,
           p_prompt, p_sample, norm_mix, norm_ffn, norm_ple, norm_final, nsa_w_in, nsa_cmp_pe, nsa_cmp_w1,
           nsa_cmp_w2, nsa_w_out, ml_w_in, ml_gate_b, ml_w_out, mla_w_in, mla_q_norm, mla_kv_norm, mla_w_uq,
           mla_w_uk, mla_w_uv, mla_w_out, hg_w_in, hg_f_b, hg_lb_logits, hg_norm, hg_w_out, ffn_w_up,
           ffn_conv_w, ffn_conv_b, ffn_w_down, ple_w_proj, ple_w_gate):
    prm = {
        'norm_mix': norm_mix, 'norm_ffn': norm_ffn, 'norm_ple': norm_ple, 'norm_final': norm_final,
        'nsa_w_in': nsa_w_in, 'nsa_cmp_pe': nsa_cmp_pe, 'nsa_cmp_w1': nsa_cmp_w1, 'nsa_cmp_w2': nsa_cmp_w2,
        'nsa_w_out': nsa_w_out, 'ml_w_in': ml_w_in, 'ml_gate_b': ml_gate_b, 'ml_w_out': ml_w_out,
        'mla_w_in': mla_w_in, 'mla_q_norm': mla_q_norm, 'mla_kv_norm': mla_kv_norm, 'mla_w_uq': mla_w_uq,
        'mla_w_uk': mla_w_uk, 'mla_w_uv': mla_w_uv, 'mla_w_out': mla_w_out, 'hg_w_in': hg_w_in,
        'hg_f_b': hg_f_b, 'hg_lb_logits': hg_lb_logits, 'hg_norm': hg_norm, 'hg_w_out': hg_w_out,
        'ffn_w_up': ffn_w_up, 'ffn_conv_w': ffn_conv_w, 'ffn_conv_b': ffn_conv_b, 'ffn_w_down': ffn_w_down,
        'ple_w_proj': ple_w_proj, 'ple_w_gate': ple_w_gate,
    }
    past = {
        'nsa_cmp': cache_nsa_cmp_kv, 'nsa_slc': cache_nsa_slc_kv, 'nsa_win': state_nsa_win_kv,
        'mla': cache_mla_latent, 'ml_C': state_mlstm_C, 'ml_n': state_mlstm_n, 'ml_m': state_mlstm_m,
        'hg_S': state_hgrn_S, 'ffn_conv': state_ffn_conv, 'page_table': page_table,
    }
    w = _prepare(prm)
    past_len = page_table.shape[1] * PAGE
    y_p, sp = _trunk(x_prompt, p_prompt, 0, None, prm, w)
    y_s, ss = _trunk(x_sample, p_sample, past_len, past, prm, w)
    return (y_p, y_s,
            sp['nsa_cmp'], ss['nsa_cmp'], sp['nsa_slc'], ss['nsa_slc'], sp['nsa_win'], ss['nsa_win'],
            sp['mla'], ss['mla'], sp['ml_C'], ss['ml_C'], sp['ml_n'], ss['ml_n'], sp['ml_m'], ss['ml_m'],
            sp['hg_S'], ss['hg_S'], sp['ffn_conv'], ss['ffn_conv'])
```

```python
import functools
import math

import numpy as np
import jax
import jax.numpy as jnp
from jax import lax
from jax.experimental import pallas as pl
from jax.experimental.pallas import tpu as pltpu

F32 = jnp.float32
BF16 = jnp.bfloat16
NEG = -1e30
NORM_EPS = 1e-6
ROPE_THETA = 10000.0
FORCE_BONUS = 1e4

LANES = 128
VMEM_LIMIT = 56 * 1024 * 1024

PAGE = 128
NSA_HEADS, NSA_G, NSA_DH = 16, 2, 64
CMP_LEN, CMP_STRIDE, CMP_HID = 32, 16, 256
SLC_BLOCK, SLC_TOPN, WINDOW = 64, 16, 512
ML_HEADS, ML_DK, ML_DV = 4, 128, 256
MLA_HEADS, MLA_Q_LORA, MLA_KV_LORA, MLA_NOPE, MLA_ROPE, MLA_DV = 16, 512, 256, 64, 32, 64
HG_HEADS, HG_DK, HG_DV = 8, 128, 128
CONV_W = 3


def _cparams(*sem):
    return pltpu.CompilerParams(dimension_semantics=sem, vmem_limit_bytes=VMEM_LIMIT)


def _dot(a, b):
    return jnp.dot(a.astype(BF16), b.astype(BF16), preferred_element_type=F32)


def _dot_nt(a, b):
    return lax.dot_general(a.astype(BF16), b.astype(BF16), (((1,), (1,)), ((), ())),
                           preferred_element_type=F32)


def _dot_f32(a, b):
    return jnp.dot(a, b, precision=lax.Precision.HIGHEST, preferred_element_type=F32)


def _dot_nt_f32(a, b):
    return lax.dot_general(a, b, (((1,), (1,)), ((), ())), precision=lax.Precision.HIGHEST,
                           preferred_element_type=F32)


def _rms(x, g):
    return x * lax.rsqrt(jnp.mean(x * x, axis=-1, keepdims=True) + NORM_EPS) * g


def _sigmoid(x):
    return 1.0 / (1.0 + jnp.exp(-x))


def _silu(x):
    return x * _sigmoid(x)


def _rope_tile(z, cos, sin, half):
    n = z.shape[1]
    reps = n // LANES
    c = jnp.concatenate([cos] * reps, axis=1) if reps > 1 else cos
    s = jnp.concatenate([sin] * reps, axis=1) if reps > 1 else sin
    lane = lax.broadcasted_iota(jnp.int32, z.shape, 1)
    lower = (lane & (2 * half - 1)) < half
    partner = jnp.where(lower, pltpu.roll(z, n - half, 1), pltpu.roll(z, half, 1))
    return z * c + partner * s


def _rope_tables(pos, half):
    inv = jnp.power(ROPE_THETA, -jnp.arange(half, dtype=F32) / half)
    ang = pos.astype(F32)[:, None] * inv[None, :]
    cos, sin = jnp.cos(ang), jnp.sin(ang)
    reps = LANES // (2 * half)
    return (jnp.tile(jnp.concatenate([cos, cos], axis=1), (1, reps)),
            jnp.tile(jnp.concatenate([-sin, sin], axis=1), (1, reps)))


def _mm_kernel(*refs, norm, res, epi, n_aux, x_slabs, out_slabs):
    x_ref = refs[0]
    pos = 1
    g_ref = None
    if norm:
        g_ref = refs[pos]
        pos += 1
    w_ref = refs[pos]
    pos += 1
    aux = refs[pos:pos + n_aux]
    pos += n_aux
    r_ref = None
    if res:
        r_ref = refs[pos]
        pos += 1
    o_ref = refs[pos]
    j = pl.program_id(1)
    if norm:
        xn_ref = refs[pos + 1]

        @pl.when(j == 0)
        def _():
            xn_ref[...] = _rms(x_ref[...], g_ref[...]).astype(BF16)

        x = xn_ref[...]
    elif x_slabs:
        x = jnp.concatenate([x_ref[s] for s in range(x_slabs)], axis=1)
    else:
        x = x_ref[...]
    z = _dot(x, w_ref[...])
    if epi is not None:
        z = epi(z, j, *[a[...] for a in aux])
    if res:
        z = z + r_ref[...]
    if out_slabs:
        for s in range(out_slabs):
            o_ref[s] = z[:, s * LANES:(s + 1) * LANES].astype(o_ref.dtype)
    else:
        o_ref[...] = z.astype(o_ref.dtype)


def _mm(x, w, *, g=None, res=None, epi=None, aux=(), out_dtype=F32, layout='flat',
        seq=None, tm=1024, tn=1024, x_cols=None, x_slab_seq=None):
    k, n = w.shape
    if x_slab_seq is not None:
        b, ks, t, _ = x.shape
        m = b * t
    else:
        m = x.shape[0]
        t = seq
    tm = min(tm, m if t is None else t)
    tn = min(tn, n)
    assert m % tm == 0 and n % tn == 0
    ni, nj = m // tm, n // tn
    tps = None if t is None else t // tm
    in_specs, args = [], []
    if x_slab_seq is not None:
        in_specs.append(pl.BlockSpec((None, ks, tm, LANES), lambda i, j: (i // tps, 0, i % tps, 0)))
    elif x_cols is not None:
        in_specs.append(pl.BlockSpec((tm, x_cols[0]), lambda i, j: (i, x_cols[1])))
    else:
        in_specs.append(pl.BlockSpec((tm, k), lambda i, j: (i, 0)))
    args.append(x)
    if g is not None:
        in_specs.append(pl.BlockSpec((1, k), lambda i, j: (0, 0)))
        args.append(g.reshape(1, k).astype(F32))
    in_specs.append(pl.BlockSpec((k, tn), lambda i, j: (0, j)))
    args.append(w)
    for a, bs, im in aux:
        in_specs.append(pl.BlockSpec(bs, im))
        args.append(a)
    if res is not None:
        in_specs.append(pl.BlockSpec((tm, tn), lambda i, j: (i, j)))
        args.append(res)
    if layout == 'flat':
        out_shape = jax.ShapeDtypeStruct((m, n), out_dtype)
        out_spec = pl.BlockSpec((tm, tn), lambda i, j: (i, j))
        out_slabs = 0
    elif layout == 'tiles':
        out_shape = jax.ShapeDtypeStruct((nj, m, tn), out_dtype)
        out_spec = pl.BlockSpec((None, tm, tn), lambda i, j: (j, i, 0))
        out_slabs = 0
    else:
        out_slabs = tn // LANES
        out_shape = jax.ShapeDtypeStruct((m // t, n // LANES, t, LANES), out_dtype)
        out_spec = pl.BlockSpec((None, out_slabs, tm, LANES), lambda i, j: (i // tps, j, i % tps, 0))
    scratch = [pltpu.VMEM((tm, k), BF16)] if g is not None else []
    kern = functools.partial(_mm_kernel, norm=g is not None, res=res is not None, epi=epi,
                             n_aux=len(aux), x_slabs=(ks if x_slab_seq is not None else 0),
                             out_slabs=out_slabs)
    return pl.pallas_call(
        kern, grid=(ni, nj), in_specs=in_specs, out_specs=out_spec, out_shape=out_shape,
        scratch_shapes=scratch, compiler_params=_cparams('parallel', 'arbitrary'),
        name='mm')(*args)


def _rope_aux(cos, sin, tm, period_blocks):
    if period_blocks is None:
        im = lambda i, j: (i, 0)
    else:
        im = lambda i, j: (i % period_blocks, 0)
    return [(cos, (tm, LANES), im), (sin, (tm, LANES), im)]


def _headmm_kernel(x_ref, w_ref, o_ref, *, heads, a, c):
    for h in range(heads):
        o_ref[:, h * c:(h + 1) * c] = _dot(x_ref[:, h * a:(h + 1) * a], w_ref[h]).astype(o_ref.dtype)


def _headmm(x, w, out_dtype=BF16, tm=512):
    m = x.shape[0]
    heads, a, c = w.shape
    tm = min(tm, m)
    return pl.pallas_call(
        functools.partial(_headmm_kernel, heads=heads, a=a, c=c), grid=(m // tm,),
        in_specs=[pl.BlockSpec((tm, heads * a), lambda i: (i, 0)),
                  pl.BlockSpec((heads, a, c), lambda i: (0, 0, 0))],
        out_specs=pl.BlockSpec((tm, heads * c), lambda i: (i, 0)),
        out_shape=jax.ShapeDtypeStruct((m, heads * c), out_dtype),
        compiler_params=_cparams('parallel'), name='headmm')(x, w)


def _ffn_kernel(*refs, tps, seq_len, has_prev, final):
    n_ple = 5 if final else 4
    if has_prev:
        (h_ref, g_ref, wa_ref, wg_ref, ca_ref, cg_ref, wd_ref, p1a_ref, p1g_ref, p2a_ref, p2g_ref) = refs[:11]
        ple_refs = refs[11:11 + n_ple]
        o_ref, ua_ref, ug_ref, hn_ref = refs[11 + n_ple:]
    else:
        (h_ref, g_ref, wa_ref, wg_ref, ca_ref, cg_ref, wd_ref) = refs[:7]
        ple_refs = refs[7:7 + n_ple]
        o_ref, sa_ref, sg_ref, hn_ref, carry_ref = refs[7 + n_ple:]
    i, j = pl.program_id(0), pl.program_id(1)
    tm = h_ref.shape[0]
    ch = min(64, tm)

    @pl.when(j == 0)
    def _():
        x = h_ref[...]
        hn_ref[...] = _rms(x, g_ref[...]).astype(BF16)
        o_ref[...] = x

    hn = hn_ref[...]
    ua = _dot(hn, wa_ref[...])
    ug = _dot(hn, wg_ref[...])
    if has_prev:
        assert ch % seq_len == 0
        ua_ref[...] = ua
        ug_ref[...] = ug
        prevs = (None, None)
    else:
        @pl.when(i % tps == 0)
        def _():
            carry_ref[j] = jnp.zeros(carry_ref.shape[1:], F32)

        prevs = (carry_ref[j, 0], carry_ref[j, 1])
        carry_ref[j, 0] = ua[tm - 8:]
        carry_ref[j, 1] = ug[tm - 8:]
        sa_ref[...] = pltpu.roll(ua[tm - 8:], CONV_W - 1, 0)[0:CONV_W - 1]
        sg_ref[...] = pltpu.roll(ug[tm - 8:], CONV_W - 1, 0)[0:CONV_W - 1]

    def conv_chunk(u, cw, prev8, p1_ref, p2_ref, r0):
        cur = u[r0:r0 + ch]
        if has_prev:
            t = lax.broadcasted_iota(jnp.int32, cur.shape, 0) % seq_len
            s1 = jnp.where(t >= 1, pltpu.roll(cur, 1, 0), p1_ref[r0:r0 + ch, :])
            s2 = jnp.where(t >= 2, pltpu.roll(cur, 2, 0), p2_ref[r0:r0 + ch, :])
        else:
            x = jnp.concatenate([prev8 if r0 == 0 else u[r0 - 8:r0], cur], axis=0)
            s1 = pltpu.roll(x, 1, 0)[8:]
            s2 = pltpu.roll(x, 2, 0)[8:]
        return cw[0:1] * s2 + cw[1:2] * s1 + cw[2:3] * cur + cw[3:4]

    cwa, cwg = ca_ref[...], cg_ref[...]
    acts = []
    for c in range(tm // ch):
        ca = conv_chunk(ua, cwa, prevs[0], p1a_ref if has_prev else None, p2a_ref if has_prev else None, c * ch)
        cg = conv_chunk(ug, cwg, prevs[1], p1g_ref if has_prev else None, p2g_ref if has_prev else None, c * ch)
        acts.append((_silu(ca) * cg).astype(BF16))
    act = jnp.concatenate(acts, axis=0) if len(acts) > 1 else acts[0]
    o_ref[...] += _dot(act, wd_ref[...])

    @pl.when(j == pl.num_programs(1) - 1)
    def _():
        x = o_ref[...]
        gate = _sigmoid(_dot(_rms(x, ple_refs[0][...]), ple_refs[1][...]))
        y = x + gate * _dot(ple_refs[2][...], ple_refs[3][...])
        if final:
            y = _rms(y, ple_refs[4][...])
        o_ref[...] = y


def _ffn(h, g, w_up, conv4, w_down, seq_len, ple, prev=None, tn=256):
    m, d = h.shape
    f = w_down.shape[0]
    nj = f // tn
    has_prev = prev is not None
    tm = m if has_prev else min(1024, seq_len)
    tps = max(seq_len // tm, 1)
    ni = m // tm
    specs = [pl.BlockSpec((tm, d), lambda i, j: (i, 0)),
             pl.BlockSpec((1, d), lambda i, j: (0, 0)),
             pl.BlockSpec((d, tn), lambda i, j: (0, j)),
             pl.BlockSpec((d, tn), lambda i, j: (0, j + nj)),
             pl.BlockSpec((4, tn), lambda i, j: (0, j)),
             pl.BlockSpec((4, tn), lambda i, j: (0, j + nj)),
             pl.BlockSpec((tn, d), lambda i, j: (j, 0))]
    args = [h, g.reshape(1, d).astype(F32), w_up, w_up, conv4, conv4, w_down]
    scratch = [pltpu.VMEM((tm, d), BF16)]
    if has_prev:
        p1, p2 = prev
        specs += [pl.BlockSpec((tm, tn), lambda i, j: (i, j)), pl.BlockSpec((tm, tn), lambda i, j: (i, j + nj)),
                  pl.BlockSpec((tm, tn), lambda i, j: (i, j)), pl.BlockSpec((tm, tn), lambda i, j: (i, j + nj))]
        args += [p1, p1, p2, p2]
    g_ple, w_gate, p_emb, w_proj, final_g = ple
    pd = p_emb.shape[1]
    specs += [pl.BlockSpec((1, d), lambda i, j: (0, 0)), pl.BlockSpec((d, d), lambda i, j: (0, 0)),
              pl.BlockSpec((tm, pd), lambda i, j: (i, 0)), pl.BlockSpec((pd, d), lambda i, j: (0, 0))]
    args += [g_ple.reshape(1, d).astype(F32), w_gate, p_emb, w_proj]
    if final_g is not None:
        specs.append(pl.BlockSpec((1, d), lambda i, j: (0, 0)))
        args.append(final_g.reshape(1, d).astype(F32))
    if has_prev:
        out_shape = [jax.ShapeDtypeStruct((m, d), F32), jax.ShapeDtypeStruct((m, f), F32),
                     jax.ShapeDtypeStruct((m, f), F32)]
        out_specs = [pl.BlockSpec((tm, d), lambda i, j: (i, 0)), pl.BlockSpec((tm, tn), lambda i, j: (i, j)),
                     pl.BlockSpec((tm, tn), lambda i, j: (i, j))]
    else:
        out_shape = [jax.ShapeDtypeStruct((m, d), F32), jax.ShapeDtypeStruct((ni, CONV_W - 1, f), F32),
                     jax.ShapeDtypeStruct((ni, CONV_W - 1, f), F32)]
        out_specs = [pl.BlockSpec((tm, d), lambda i, j: (i, 0)),
                     pl.BlockSpec((None, CONV_W - 1, tn), lambda i, j: (i, 0, j)),
                     pl.BlockSpec((None, CONV_W - 1, tn), lambda i, j: (i, 0, j))]
        scratch.append(pltpu.VMEM((nj, 2, 8, tn), F32))
    outs = pl.pallas_call(
        functools.partial(_ffn_kernel, tps=tps, seq_len=seq_len, has_prev=has_prev, final=final_g is not None),
        grid=(ni, nj), in_specs=specs, out_specs=out_specs, out_shape=out_shape,
        scratch_shapes=scratch, compiler_params=_cparams('arbitrary', 'arbitrary'), name='ffn')(*args)
    if has_prev:
        return outs
    return outs[0], outs[1][tps - 1::tps], outs[2][tps - 1::tps]


def _softmax_tile(s, mask):
    s = jnp.where(mask, s, NEG)
    e = jnp.where(mask, jnp.exp(s - jnp.max(s, axis=-1, keepdims=True)), 0.0)
    return e / jnp.maximum(jnp.sum(e, axis=-1, keepdims=True), 1e-30)


def _online_init(m_ref, l_ref, acc_ref):
    m_ref[...] = jnp.full(m_ref.shape, NEG, F32)
    l_ref[...] = jnp.zeros(l_ref.shape, F32)
    acc_ref[...] = jnp.zeros(acc_ref.shape, F32)


def _online_update(s, mask, v, m_ref, l_ref, acc_ref, v_t=False):
    if mask is not None:
        s = jnp.where(mask, s, NEG)
    m_old = m_ref[...]
    m_new = jnp.maximum(m_old, jnp.max(s, axis=-1, keepdims=True))
    p = jnp.exp(s - m_new)
    if mask is not None:
        p = jnp.where(mask, p, 0.0)
    alpha = jnp.exp(m_old - m_new)
    l_ref[...] = alpha * l_ref[...] + jnp.sum(p, axis=-1, keepdims=True)
    acc_ref[...] = alpha * acc_ref[...] + (_dot_nt(p, v) if v_t else _dot(p, v))
    m_ref[...] = m_new


def _online_result(l_ref, acc_ref):
    return acc_ref[...] / jnp.maximum(l_ref[...], 1e-30)


def _online_update_t(st, bias, v_t, m_ref, l_ref, acc_ref, base2=False):
    if bias is not None:
        st = st + bias
    ex = jnp.exp2 if base2 else jnp.exp
    m_old = m_ref[...]
    m_new = jnp.maximum(m_old, jnp.max(st, axis=0, keepdims=True))
    p = ex(st - m_new)
    alpha = ex(m_old - m_new)
    l_ref[...] = alpha * l_ref[...] + jnp.sum(p, axis=0, keepdims=True)
    acc_ref[...] = alpha * acc_ref[...] + _dot(v_t, p)
    m_ref[...] = m_new


def _softmax_tile_t(st, mask):
    st = jnp.where(mask, st, NEG)
    e = jnp.where(mask, jnp.exp(st - jnp.max(st, axis=0, keepdims=True)), 0.0)
    return e / jnp.maximum(jnp.sum(e, axis=0, keepdims=True), 1e-30)


def _cumsum(x, axis):
    n = x.shape[axis]
    idx = lax.broadcasted_iota(jnp.int32, x.shape, axis)
    k = 1
    while k < n:
        x = x + jnp.where(idx >= k, pltpu.roll(x, k, axis), 0.0)
        k *= 2
    return x


def _log_sigmoid(x):
    return jnp.minimum(x, 0.0) - jnp.log(1.0 + jnp.exp(-jnp.abs(x)))


def _compress_rows(k_ref, v_ref, n_seg, pe_ref, w1_ref, w2p_ref):
    outs = []
    seg_w = CMP_STRIDE * NSA_DH
    lane = lax.broadcasted_iota(jnp.int32, (n_seg, LANES), 1)
    for kind, rows_ref in enumerate((k_ref, v_ref)):
        xs = [rows_ref[pl.ds(s, n_seg, stride=CMP_STRIDE), :] for s in range(CMP_STRIDE)]
        o = None
        for g in range(NSA_G):
            pieces = []
            for a in range(CMP_STRIDE // 2):
                ev, od = xs[2 * a], xs[2 * a + 1]
                if g == 0:
                    pieces.append(jnp.where(lane < NSA_DH, ev, pltpu.roll(od, NSA_DH, 1)))
                else:
                    pieces.append(jnp.where(lane < NSA_DH, pltpu.roll(ev, NSA_DH, 1), od))
            seg = jnp.concatenate(pieces, axis=1)
            pre = None
            for r in range(CMP_LEN // CMP_STRIDE):
                acc = _dot(seg + pe_ref[kind, r:r + 1, :], w1_ref[kind, r * seg_w:(r + 1) * seg_w, :])
                pre = acc if r == 0 else pre + pltpu.roll(acc, n_seg - r, 0)
            t = _dot(_silu(pre), w2p_ref[kind, g])
            o = t if o is None else o + t
        outs.append(o)
    return outs


def _nsa_cmp_kernel(k_ref, v_ref, pe_ref, w1_ref, w2p_ref, kc_ref, vc_ref, *, n_seg):
    kc, vc = _compress_rows(k_ref, v_ref, n_seg, pe_ref, w1_ref, w2p_ref)
    kc_ref[...] = kc
    vc_ref[...] = vc


def _nsa_compress(rows, pe, w1, w2p):
    b, t, _ = rows.shape
    n_seg = t // CMP_STRIDE
    full = lambda shp: pl.BlockSpec(shp, lambda i: (0,) * len(shp))
    return pl.pallas_call(
        functools.partial(_nsa_cmp_kernel, n_seg=n_seg), grid=(b,),
        in_specs=[pl.BlockSpec((None, t, LANES), lambda i: (i, 0, 0)),
                  pl.BlockSpec((None, t, LANES), lambda i: (i, 0, 1)), full(pe.shape), full(w1.shape),
                  full(w2p.shape)],
        out_specs=[pl.BlockSpec((None, n_seg, LANES), lambda i: (i, 0, 0))] * 2,
        out_shape=[jax.ShapeDtypeStruct((b, n_seg, LANES), F32)] * 2,
        compiler_params=_cparams('parallel'), name='nsa_compress')(rows, rows, pe, w1, w2p)


def _overlap_t(nc, ns, nc_pad, ns_pad):
    cs = np.arange(nc)[None, :] * CMP_STRIDE
    ss = np.arange(ns)[:, None] * SLC_BLOCK
    ov = np.maximum(np.minimum(cs + CMP_LEN, ss + SLC_BLOCK) - np.maximum(cs, ss), 0) / CMP_LEN
    out = np.zeros((ns_pad, nc_pad), np.float32)
    out[:ns, :nc] = ov
    return jnp.asarray(out)


def _select_blocks(sc, valid, blk, n_cand, n_sel):
    rank = jnp.zeros(sc.shape, F32)
    for j in range(n_cand):
        rj = sc[j:j + 1, :]
        beats = (rj > sc) | ((rj == sc) & (blk > j))
        rank = rank + jnp.where(beats, 1.0, 0.0)
    return jnp.where(valid & (rank < n_sel), 1.0, 0.0)


def _nsa_attn_kernel(q_ref, gt_ref, kc_ref, vc_ref, ks_ref, vs_ref, kw_ref, vw_ref, ovt_ref, et_ref,
                     o_ref, ocmp_ref, msk_ref, wmsk_ref, qt_ref, vst_ref, vwt_ref, m_ref, l_ref, acc_ref,
                     *, tq, t_len, nc, n_sel):
    qi = pl.program_id(1)
    q0 = qi * tq
    scale = NSA_DH ** -0.5
    kt = 256
    ns = t_len // SLC_BLOCK
    n_pairs = NSA_HEADS // 2

    @pl.when(qi == 0)
    def _():
        for jt in range(t_len // kt):
            vst_ref[jt] = vs_ref[jt * kt:(jt + 1) * kt, :].T.astype(BF16)
            vwt_ref[jt] = vw_ref[jt * kt:(jt + 1) * kt, :].T.astype(BF16)

    hpg = NSA_HEADS // NSA_G
    pairs_g = n_pairs // NSA_G
    wq = NSA_HEADS * tq
    zeros_half = jnp.zeros((NSA_DH, tq), F32)
    for hp in range(n_pairs):
        g = hp // pairs_g
        slab_t = q_ref[hp].astype(F32).T * scale
        for u, part in enumerate((slab_t[:NSA_DH], slab_t[NSA_DH:])):
            ext = jnp.concatenate([part, zeros_half] if g == 0 else [zeros_half, part], axis=0)
            qt_ref[:, (2 * hp + u) * tq:(2 * hp + u + 1) * tq] = ext.astype(BF16)
    q_all = qt_ref[...]

    kc = kc_ref[...]
    vc_t = vc_ref[...].T
    ncp = kc.shape[0]
    cidx = lax.broadcasted_iota(jnp.int32, (ncp, wq), 0)
    qpos_c = q0 + (lax.broadcasted_iota(jnp.int32, (ncp, wq), 1) & (tq - 1))
    cmask = (cidx < nc) & (cidx * CMP_STRIDE + (CMP_LEN - 1) <= qpos_c)
    p = _softmax_tile_t(_dot(kc, q_all), cmask)
    ocmp_ref[...] = _dot(vc_t, p)
    imp = []
    for g in range(NSA_G):
        tot = p[:, g * hpg * tq:(g * hpg + 1) * tq]
        for hh in range(1, hpg):
            tot = tot + p[:, (g * hpg + hh) * tq:(g * hpg + hh + 1) * tq]
        imp.append(tot)

    nsp = ovt_ref.shape[0]
    blk = lax.broadcasted_iota(jnp.int32, (ns, tq), 0)
    cur = (q0 + lax.broadcasted_iota(jnp.int32, (ns, tq), 1)) // SLC_BLOCK
    forced = (blk == 0) | (blk == cur) | (blk == cur - 1)
    valid = blk <= cur
    krow1 = lax.broadcasted_iota(jnp.int32, (kt, tq), 0)
    qpos1 = q0 + lax.broadcasted_iota(jnp.int32, (kt, tq), 1)
    for g in range(NSA_G):
        sc = _dot_f32(ovt_ref[...], imp[g])[:ns]
        sc = jnp.where(valid, sc + jnp.where(forced, FORCE_BONUS, 0.0), NEG)
        sel_t = _select_blocks(sc, valid, blk, ns, n_sel)
        if nsp > ns:
            sel_t = jnp.concatenate([sel_t, jnp.zeros((nsp - ns, tq), F32)], axis=0)
        hit = _dot(et_ref[...], sel_t)
        for jt in range(t_len // kt):
            msk_ref[g, jt] = jnp.where((jt * kt + krow1 <= qpos1) & (hit[jt * kt:(jt + 1) * kt] > 0.5), 0.0, NEG)
    win_tiles = list(range(-WINDOW, tq, kt))[::-1]
    for wi, d in enumerate(win_tiles):
        kpos = q0 + d + krow1
        wmsk_ref[wi] = jnp.where((kpos <= qpos1) & (kpos > qpos1 - WINDOW), 0.0, NEG)

    gt_t = gt_ref[...].T
    _online_init(m_ref, l_ref, acc_ref)

    def slc_body(jt, carry):
        start = pl.multiple_of(jt * kt, kt)
        bias = jnp.concatenate([msk_ref[g, jt] for g in range(NSA_G) for _ in range(hpg)], axis=1)
        _online_update_t(_dot(ks_ref[pl.ds(start, kt), :], q_all), bias, vst_ref[jt], m_ref, l_ref, acc_ref)
        return carry

    lax.fori_loop(0, (q0 + tq) // kt, slc_body, 0)
    o_slc = _online_result(l_ref, acc_ref)

    _online_init(m_ref, l_ref, acc_ref)
    for wi, d in enumerate(win_tiles):
        @pl.when(q0 + d >= 0)
        def _():
            start = pl.multiple_of(q0 + d, kt)
            bias = jnp.concatenate([wmsk_ref[wi]] * NSA_HEADS, axis=1)
            _online_update_t(_dot(kw_ref[pl.ds(start, kt), :], q_all), bias, vwt_ref[(q0 + d) // kt],
                             m_ref, l_ref, acc_ref)
    o_win = _online_result(l_ref, acc_ref)

    def gate(c):
        return jnp.concatenate([gt_t[c * NSA_HEADS + h:c * NSA_HEADS + h + 1, :] for h in range(NSA_HEADS)], axis=1)

    o = gate(0) * ocmp_ref[...] + gate(1) * o_slc + gate(2) * o_win
    for hp in range(n_pairs):
        g = hp // pairs_g
        o_ref[hp] = jnp.concatenate(
            [o[g * NSA_DH:(g + 1) * NSA_DH, 2 * hp * tq:(2 * hp + 1) * tq],
             o[g * NSA_DH:(g + 1) * NSA_DH, (2 * hp + 1) * tq:(2 * hp + 2) * tq]], axis=0).T.astype(o_ref.dtype)


def _nsa_attn(q_slab, gates, kc, vc, slc_rows, win_rows, tq=256):
    b, n_pairs, t, _ = q_slab.shape
    tq = min(tq, t)
    n_seg = kc.shape[1]
    nc = n_seg - 1
    ns = t // SLC_BLOCK
    nsp = LANES
    ovt = _overlap_t(nc, ns, n_seg, nsp)
    e = jnp.asarray((np.arange(t)[:, None] // SLC_BLOCK == np.arange(nsp)[None, :]).astype(np.float32), BF16)
    kern = functools.partial(_nsa_attn_kernel, tq=tq, t_len=t, nc=nc, n_sel=min(SLC_TOPN, ns))
    wq = NSA_HEADS * tq
    seq = lambda c: pl.BlockSpec((None, t, LANES), lambda i, j: (i, 0, c))
    return pl.pallas_call(
        kern, grid=(b, t // tq),
        in_specs=[pl.BlockSpec((None, n_pairs, tq, LANES), lambda i, j: (i, 0, j, 0)),
                  pl.BlockSpec((None, tq, LANES), lambda i, j: (i, j, 0)),
                  pl.BlockSpec((None, n_seg, LANES), lambda i, j: (i, 0, 0)),
                  pl.BlockSpec((None, n_seg, LANES), lambda i, j: (i, 0, 0)),
                  seq(0), seq(1), seq(0), seq(1),
                  pl.BlockSpec(ovt.shape, lambda i, j: (0, 0)),
                  pl.BlockSpec(e.shape, lambda i, j: (0, 0))],
        out_specs=pl.BlockSpec((None, n_pairs, tq, LANES), lambda i, j: (i, 0, j, 0)),
        out_shape=jax.ShapeDtypeStruct((b, n_pairs, t, LANES), BF16),
        scratch_shapes=[pltpu.VMEM((LANES, wq), F32),
                        pltpu.VMEM((NSA_G, t // 256, 256, tq), F32),
                        pltpu.VMEM((len(range(-WINDOW, tq, 256)), 256, tq), F32),
                        pltpu.VMEM((LANES, wq), BF16),
                        pltpu.VMEM((t // 256, LANES, 256), BF16), pltpu.VMEM((t // 256, LANES, 256), BF16),
                        pltpu.VMEM((1, wq), F32), pltpu.VMEM((1, wq), F32), pltpu.VMEM((LANES, wq), F32)],
        compiler_params=_cparams('arbitrary', 'arbitrary'), name='nsa_attn')(
            q_slab, gates, kc, vc, slc_rows, slc_rows, win_rows, win_rows, ovt, e)


def _mlstm_kernel(*refs, L, t_real, has_state, pad):
    it = iter(refs)
    z_ref, gc_ref, gr_ref, bc_ref, br_ref = next(it), next(it), next(it), next(it), next(it)
    if has_state:
        c0_ref, n0_ref, m0_ref = next(it), next(it), next(it)
    h_ref, c_out, n_out, m_out = next(it), next(it), next(it), next(it)
    c_scr, n_scr, m_scr = next(it), next(it), next(it)
    if pad:
        zp_ref, gcp_ref, grp_ref = next(it), next(it), next(it)
    ci = pl.program_id(1)
    nchunks = pl.num_programs(1)

    @pl.when(ci == 0)
    def _():
        if has_state:
            c_scr[...] = c0_ref[...]
            n_scr[...] = n0_ref[...]
            m_scr[...] = m0_ref[...]
        else:
            c_scr[...] = jnp.zeros(c_scr.shape, F32)
            n_scr[...] = jnp.zeros(n_scr.shape, F32)
            m_scr[...] = jnp.zeros(m_scr.shape, F32)

    if pad:
        @pl.when((pl.program_id(0) == 0) & (ci == 0))
        def _():
            zp_ref[...] = jnp.zeros(zp_ref.shape, F32)
            gcp_ref[...] = jnp.zeros(gcp_ref.shape, F32)
            grp_ref[...] = jnp.zeros(grp_ref.shape, F32)

        zp_ref[:, 0:t_real, :] = z_ref[...]
        gcp_ref[0:t_real, :] = gc_ref[...]
        grp_ref[:, 0:t_real] = gr_ref[...]
        z_ref, gc_ref, gr_ref = zp_ref, gcp_ref, grp_ref

    H = ML_HEADS
    gcol = gc_ref[...] + bc_ref[...]
    grow = gr_ref[...] + br_ref[...]
    lf_c = _log_sigmoid(gcol)
    lf_r = _log_sigmoid(grow)
    ig_c, ig_r = gcol, grow
    if t_real < L:
        rv = lax.broadcasted_iota(jnp.int32, gcol.shape, 0) < t_real
        lv = lax.broadcasted_iota(jnp.int32, grow.shape, 1) < t_real
        lf_c, ig_c = jnp.where(rv, lf_c, 0.0), jnp.where(rv, ig_c, NEG)
        lf_r, ig_r = jnp.where(lv, lf_r, 0.0), jnp.where(lv, ig_r, NEG)
    bcum_c = _cumsum(lf_c, 0)
    bcum_r = _cumsum(lf_r, 1)
    tri = lax.broadcasted_iota(jnp.int32, (L, L), 0) >= lax.broadcasted_iota(jnp.int32, (L, L), 1)
    heads = range(H)
    n_all = n_scr[...]
    m_all = m_scr[...]
    q = [z_ref[h] for h in heads]
    k = [z_ref[H + h] * (ML_DK ** -0.5) for h in heads]
    v = [jnp.concatenate([z_ref[2 * H + 2 * h], z_ref[2 * H + 2 * h + 1]], axis=1) for h in heads]
    c_st = [c_scr[h] for h in heads]
    s_qk = [_dot_nt(q[h], k[h]) for h in heads]
    s_qc = [_dot_nt(q[h], c_st[h]) for h in heads]
    sc, mt, w_state, m_old, m_new, ws_t, wc = [], [], [], [], [], [], []
    for h in heads:
        bc_t = bcum_c[:, H + h:H + h + 1]
        bc_s = bcum_r[H + h:H + h + 1, :]
        ig_s = ig_r[h:h + 1, :]
        ig_t = ig_c[:, h:h + 1]
        m_old.append(m_all[h:h + 1, 0:1])
        dmat = jnp.where(tri, bc_t - bc_s + ig_s, NEG)
        inter = bc_t + m_old[h]
        mt.append(jnp.maximum(jnp.max(dmat, axis=1, keepdims=True), inter))
        w_state.append(jnp.exp(inter - mt[h]))
        sc.append(s_qk[h] * jnp.exp(dmat - mt[h]))
        bl = bc_s[:, L - 1:L]
        dl_r = bl - bc_s + ig_s
        dl_t = bl - bc_t + ig_t
        m_new.append(jnp.maximum(bl + m_old[h], jnp.max(dl_r, axis=1, keepdims=True)))
        ws_t.append(jnp.exp(dl_t - m_new[h]))
        wc.append(jnp.exp(bl + m_old[h] - m_new[h]))
    s_v = [_dot(sc[h], v[h]) for h in heads]
    upd = [_dot((v[h] * ws_t[h]).T, k[h]) for h in heads]
    for h in heads:
        n_st = n_all[h:h + 1, :]
        og = jnp.concatenate([z_ref[4 * H + 2 * h], z_ref[4 * H + 2 * h + 1]], axis=1)
        num = s_v[h] + w_state[h] * s_qc[h]
        den = jnp.sum(sc[h], axis=1, keepdims=True) + w_state[h] * jnp.sum(q[h] * n_st, axis=1, keepdims=True)
        hh = num / jnp.maximum(jnp.abs(den), jnp.exp(-mt[h])) * og
        h_ref[2 * h] = hh[0:h_ref.shape[1], 0:LANES].astype(h_ref.dtype)
        h_ref[2 * h + 1] = hh[0:h_ref.shape[1], LANES:].astype(h_ref.dtype)
        c_scr[h] = wc[h] * c_st[h] + upd[h]
        n_scr[h:h + 1, :] = wc[h] * n_st + jnp.sum(ws_t[h] * k[h], axis=0, keepdims=True)
        m_scr[h:h + 1, :] = jnp.broadcast_to(m_new[h], (1, LANES))

    @pl.when(ci == nchunks - 1)
    def _():
        c_out[...] = c_scr[...]
        n_out[...] = n_scr[...]
        m_out[...] = m_scr[...]


def _mlstm(z_slab, gates_col, gates_row, gate_b, state=None, L=256):
    b, ns, t, _ = z_slab.shape
    pad = t < 8
    L = 16 if pad else min(L, t)
    lr = t if pad else L
    nch = 1 if pad else t // L
    H = ML_HEADS
    bcol = jnp.zeros((1, LANES), F32).at[0, :2 * H].set(gate_b.reshape(-1).astype(F32))
    brow = gate_b.reshape(2 * H, 1).astype(F32)
    specs = [pl.BlockSpec((None, ns, lr, LANES), lambda i, c: (i, 0, c, 0)),
             pl.BlockSpec((None, lr, LANES), lambda i, c: (i, c, 0)),
             pl.BlockSpec((None, 2 * H, lr), lambda i, c: (i, 0, c)),
             pl.BlockSpec((1, LANES), lambda i, c: (0, 0)),
             pl.BlockSpec((2 * H, 1), lambda i, c: (0, 0))]
    args = [z_slab, gates_col, gates_row, bcol, brow]
    if state is not None:
        c0, n0, m0 = state
        n0p = jnp.zeros((b, 8, LANES), F32).at[:, :H].set(n0.astype(F32))
        m0p = jnp.zeros((b, 8, LANES), F32).at[:, :H].set(jnp.broadcast_to(m0.astype(F32)[..., None], (b, H, LANES)))
        specs += [pl.BlockSpec((None, H, ML_DV, ML_DK), lambda i, c: (i, 0, 0, 0)),
                  pl.BlockSpec((None, 8, LANES), lambda i, c: (i, 0, 0)),
                  pl.BlockSpec((None, 8, LANES), lambda i, c: (i, 0, 0))]
        args += [c0.astype(F32), n0p, m0p]
    scratch = [pltpu.VMEM((H, ML_DV, ML_DK), F32), pltpu.VMEM((8, LANES), F32), pltpu.VMEM((8, LANES), F32)]
    if pad:
        scratch += [pltpu.VMEM((ns, L, LANES), F32), pltpu.VMEM((L, LANES), F32), pltpu.VMEM((2 * H, L), F32)]
    kern = functools.partial(_mlstm_kernel, L=L, t_real=t if pad else L, has_state=state is not None, pad=pad)
    h_slab, c_f, n_f, m_f = pl.pallas_call(
        kern, grid=(b, nch), in_specs=specs,
        out_specs=[pl.BlockSpec((None, 2 * H, lr, LANES), lambda i, c: (i, 0, c, 0)),
                   pl.BlockSpec((None, H, ML_DV, ML_DK), lambda i, c: (i, 0, 0, 0)),
                   pl.BlockSpec((None, 8, LANES), lambda i, c: (i, 0, 0)),
                   pl.BlockSpec((None, 8, LANES), lambda i, c: (i, 0, 0))],
        out_shape=[jax.ShapeDtypeStruct((b, 2 * H, t, LANES), BF16),
                   jax.ShapeDtypeStruct((b, H, ML_DV, ML_DK), F32),
                   jax.ShapeDtypeStruct((b, 8, LANES), F32), jax.ShapeDtypeStruct((b, 8, LANES), F32)],
        scratch_shapes=scratch, compiler_params=_cparams('arbitrary', 'arbitrary'), name='mlstm')(*args)
    return h_slab, c_f, n_f[:, :H], m_f[:, :H, 0]


def _hgrn_levels(L):
    n_lev = int(math.log2(L))
    t = np.arange(L)
    pall = np.zeros((n_lev * L, L), np.float32)
    lmask = np.zeros((n_lev, L, L), np.float32)
    for lev in range(n_lev):
        w = L >> lev
        mid = (t // w) * w + w // 2
        pall[lev * L + t, mid - 1] = 1.0
        same = (t[:, None] // w) == (t[None, :] // w)
        lmask[lev] = same & ((t[:, None] % w) >= w // 2) & ((t[None, :] % w) < w // 2)
    return jnp.asarray(pall, BF16), jnp.asarray(lmask)


def _hgrn_kernel(*refs, L, t_real, has_state, pad):
    it = iter(refs)
    z_ref, lf_ref, gn_ref, pall_ref, lmask_ref = next(it), next(it), next(it), next(it), next(it)
    if has_state:
        s0_ref = next(it)
    o_ref, s_out = next(it), next(it)
    st_scr = next(it)
    if pad:
        zp_ref = next(it)
    ci = pl.program_id(1)
    nchunks = pl.num_programs(1)
    H = HG_HEADS
    n_lev = lmask_ref.shape[0]

    @pl.when(ci == 0)
    def _():
        for h in range(H):
            st_scr[h] = s0_ref[h].T if has_state else jnp.zeros((HG_DV, HG_DK), F32)

    if pad:
        @pl.when((pl.program_id(0) == 0) & (ci == 0))
        def _():
            zp_ref[...] = jnp.zeros(zp_ref.shape, F32)

        zp_ref[:, 0:t_real, :] = z_ref[...]
        z_ref = zp_ref

    rows = lax.broadcasted_iota(jnp.int32, (L, LANES), 0)
    eye = lax.broadcasted_iota(jnp.int32, (L, L), 0) == lax.broadcasted_iota(jnp.int32, (L, L), 1)
    gn = gn_ref[...]

    def head(h, s_t, out):
        q = z_ref[h]
        zf = z_ref[H + h]
        v = z_ref[2 * H + h]
        gate = z_ref[3 * H + h]
        lower = lf_ref[pl.ds(h, 1), :]
        fb = lf_ref[pl.ds(H + h, 1), :]
        f = lower + (1.0 - lower) * _sigmoid(zf + fb)
        lf = jnp.log(f)
        k = 1.0 - f
        if t_real < L:
            lf = jnp.where(rows < t_real, lf, 0.0)
            k = jnp.where(rows < t_real, k, 0.0)
        bcum = _cumsum(lf, 0)
        lev0 = n_lev - max(1, (t_real - 1).bit_length())
        hi = bcum.astype(BF16)
        rem = bcum - hi.astype(F32)
        mid = rem.astype(BF16)
        low = (rem - mid.astype(F32)).astype(BF16)
        pieces = jnp.concatenate([hi, mid, low], axis=1)
        yield
        picked = jnp.dot(pall_ref[lev0 * L:, :], pieces, preferred_element_type=F32)
        refs_all = picked[:, :HG_DK] + picked[:, HG_DK:2 * HG_DK] + picked[:, 2 * HG_DK:]
        att = jnp.where(eye, jnp.sum(q * k, axis=1, keepdims=True), 0.0)
        factors = []
        for lev in range(lev0, n_lev):
            r = refs_all[(lev - lev0) * L:(lev - lev0 + 1) * L]
            factors.append((q * jnp.exp(jnp.minimum(bcum - r, 0.0)), k * jnp.exp(jnp.minimum(r - bcum, 0.0))))
        q_dec = q * jnp.exp(bcum)
        bl = bcum[L - 1:L, :]
        k_dec = k * jnp.exp(bl - bcum)
        v_t = v.T
        yield
        for lev, (qt, kt) in zip(range(lev0, n_lev), factors):
            att = att + _dot_nt(qt, kt) * lmask_ref[lev]
        o_state = _dot_nt(q_dec, s_t)
        upd = _dot(v_t, k_dec)
        yield
        o = _dot(att, v) + o_state
        on = _rms(o, gn) * _silu(gate)
        o_ref[h] = on[0:o_ref.shape[1]].astype(o_ref.dtype)
        out.append(jnp.exp(bl) * s_t + upd)

    s_new = []
    gens = [head(h, st_scr[h], s_new) for h in range(H)]
    for _ in range(4):
        for gen in gens:
            next(gen, None)
    for h in range(H):
        st_scr[h] = s_new[h]

    @pl.when(ci == nchunks - 1)
    def _():
        for h in range(H):
            s_out[h] = st_scr[h].T


def _hgrn(z_slab, lower, f_b, g_norm, state=None, L=128):
    b, ns, t, _ = z_slab.shape
    pad = t < 8
    L = 16 if pad else min(L, t)
    lr = t if pad else L
    nch = 1 if pad else t // L
    H = HG_HEADS
    pall, lmask = _hgrn_levels(L)
    lowfb = jnp.concatenate([lower.reshape(H, HG_DK), f_b.reshape(H, HG_DK)], axis=0).astype(F32)
    specs = [pl.BlockSpec((None, ns, lr, LANES), lambda i, c: (i, 0, c, 0)),
             pl.BlockSpec((2 * H, LANES), lambda i, c: (0, 0)),
             pl.BlockSpec((1, LANES), lambda i, c: (0, 0)),
             pl.BlockSpec(pall.shape, lambda i, c: (0, 0)),
             pl.BlockSpec(lmask.shape, lambda i, c: (0, 0, 0))]
    args = [z_slab, lowfb, g_norm.reshape(1, HG_DV).astype(F32), pall, lmask]
    if state is not None:
        specs.append(pl.BlockSpec((None, H, HG_DK, HG_DV), lambda i, c: (i, 0, 0, 0)))
        args.append(state.astype(F32))
    scratch = [pltpu.VMEM((H, HG_DV, HG_DK), F32)]
    if pad:
        scratch.append(pltpu.VMEM((ns, L, LANES), F32))
    kern = functools.partial(_hgrn_kernel, L=L, t_real=t if pad else L, has_state=state is not None, pad=pad)
    return pl.pallas_call(
        kern, grid=(b, nch), in_specs=specs,
        out_specs=[pl.BlockSpec((None, H, lr, LANES), lambda i, c: (i, 0, c, 0)),
                   pl.BlockSpec((None, H, HG_DK, HG_DV), lambda i, c: (i, 0, 0, 0))],
        out_shape=[jax.ShapeDtypeStruct((b, H, t, LANES), BF16),
                   jax.ShapeDtypeStruct((b, H, HG_DK, HG_DV), F32)],
        scratch_shapes=scratch, compiler_params=_cparams('arbitrary', 'arbitrary'), name='hgrn')(*args)


def _mla_attn_kernel(ql_ref, qr_ref, lc_ref, lr_ref, o_ref, lct_ref, qlt_ref, qrt_ref, m_ref, l_ref, acc_ref,
                     *, tq, t_len):
    qi = pl.program_id(1)
    q0 = qi * tq
    kt = lct_ref.shape[2]
    scale = (MLA_NOPE + MLA_ROPE) ** -0.5
    mq = MLA_HEADS * tq
    per_slab = LANES // MLA_ROPE

    @pl.when(qi == 0)
    def _():
        for jt in range(t_len // kt):
            lct_ref[jt] = lc_ref[jt * kt:(jt + 1) * kt, :].T.astype(BF16)

    zeros_r = jnp.zeros((LANES - MLA_ROPE, tq), F32)
    for h in range(MLA_HEADS):
        qlt_ref[:, h * tq:(h + 1) * tq] = ql_ref[:, h * MLA_KV_LORA:(h + 1) * MLA_KV_LORA].astype(F32).T.astype(BF16)
        if h % per_slab == 0:
            slab_t = qr_ref[:, (h // per_slab) * LANES:(h // per_slab + 1) * LANES].astype(F32).T
        rope_t = slab_t[(h % per_slab) * MLA_ROPE:(h % per_slab + 1) * MLA_ROPE]
        qrt_ref[:, h * tq:(h + 1) * tq] = jnp.concatenate([rope_t, zeros_r], axis=0).astype(BF16)
    q_lat = qlt_ref[...]
    q_rope = qrt_ref[...]
    _online_init(m_ref, l_ref, acc_ref)

    def tile(j, masked):
        start = pl.multiple_of(j * kt, kt)
        st = (_dot(lc_ref[pl.ds(start, kt), :], q_lat) + _dot(lr_ref[pl.ds(start, kt), :], q_rope)) * (
            scale * math.log2(math.e))
        if masked:
            krow = lax.broadcasted_iota(jnp.int32, (kt, mq), 0)
            qpos = q0 + (lax.broadcasted_iota(jnp.int32, (kt, mq), 1) & (tq - 1))
            bias = jnp.where(start + krow <= qpos, 0.0, NEG)
        else:
            bias = None
        _online_update_t(st, bias, lct_ref[j], m_ref, l_ref, acc_ref, base2=True)

    def body(j, carry):
        tile(j, False)
        return carry

    n_full = q0 // kt
    lax.fori_loop(0, n_full, body, 0)
    tile(n_full, True)
    o = _online_result(l_ref, acc_ref)
    for h in range(MLA_HEADS):
        o_ref[:, h * MLA_KV_LORA:(h + 1) * MLA_KV_LORA] = o[:, h * tq:(h + 1) * tq].T.astype(o_ref.dtype)


def _mla_attn(q_lat, q_rope, lat, b, t, tq=256):
    tq = min(tq, t)
    kt = min(256, t)
    wl = MLA_HEADS * MLA_KV_LORA
    wr = MLA_HEADS * MLA_ROPE
    lat3 = lat.reshape(b, t, lat.shape[-1])
    c_blk = MLA_Q_LORA // MLA_KV_LORA
    r_blk = (MLA_Q_LORA + MLA_KV_LORA) // LANES
    return pl.pallas_call(
        functools.partial(_mla_attn_kernel, tq=tq, t_len=t), grid=(b, t // tq),
        in_specs=[pl.BlockSpec((None, tq, wl), lambda i, j: (i, j, 0)),
                  pl.BlockSpec((None, tq, wr), lambda i, j: (i, j, 0)),
                  pl.BlockSpec((None, t, MLA_KV_LORA), lambda i, j: (i, 0, c_blk)),
                  pl.BlockSpec((None, t, LANES), lambda i, j: (i, 0, r_blk))],
        out_specs=pl.BlockSpec((None, tq, wl), lambda i, j: (i, j, 0)),
        out_shape=jax.ShapeDtypeStruct((b, t, wl), BF16),
        scratch_shapes=[pltpu.VMEM((t // kt, MLA_KV_LORA, kt), BF16),
                        pltpu.VMEM((MLA_KV_LORA, MLA_HEADS * tq), BF16), pltpu.VMEM((LANES, MLA_HEADS * tq), BF16),
                        pltpu.VMEM((1, MLA_HEADS * tq), F32), pltpu.VMEM((1, MLA_HEADS * tq), F32),
                        pltpu.VMEM((MLA_KV_LORA, MLA_HEADS * tq), F32)],
        compiler_params=_cparams('arbitrary', 'arbitrary'), name='mla_attn')(
            q_lat.reshape(b, t, wl), q_rope.reshape(b, t, wr), lat3, lat3).reshape(b * t, wl)


def _pad_cols(w, n):
    return jnp.pad(w, ((0, 0), (0, n - w.shape[1])))


def _prep_nsa(w_in, pe, w1, w2, w_out):
    nq = NSA_HEADS * NSA_DH
    kvw = 2 * NSA_G * NSA_DH
    w2 = w2.astype(BF16)
    z = jnp.zeros_like(w2)
    w2p = jnp.stack([jnp.concatenate([w2, z], axis=-1), jnp.concatenate([z, w2], axis=-1)], axis=1)
    return dict(q=w_in[:, :nq].astype(BF16), kv=w_in[:, nq:nq + 3 * kvw].astype(BF16),
                g=_pad_cols(w_in[:, nq + 3 * kvw:], LANES).astype(BF16),
                pe=pe.astype(F32).reshape(2, CMP_LEN // CMP_STRIDE, CMP_STRIDE * NSA_DH),
                w1=w1.astype(BF16), w2p=w2p, out=w_out.astype(BF16))


def _prep_ml(w_in, gate_b, w_out):
    a = 2 * ML_HEADS * ML_DK + ML_HEADS * ML_DV
    main = jnp.concatenate([w_in[:, :a], w_in[:, a + 2 * ML_HEADS:]], axis=1)
    return dict(main=main.astype(BF16), gate=_pad_cols(w_in[:, a:a + 2 * ML_HEADS], LANES).astype(BF16),
                gate_b=gate_b, out=w_out.astype(BF16))


def _prep_mla(w_in, q_norm, kv_norm, w_uq, w_uk, w_uv, w_out):
    uq = w_uq.reshape(MLA_Q_LORA, MLA_HEADS, MLA_NOPE + MLA_ROPE)
    return dict(w_in=_pad_cols(w_in, 7 * LANES).astype(BF16),
                q_norm=q_norm.reshape(1, -1).astype(F32), kv_norm=kv_norm.reshape(1, -1).astype(F32),
                uq_nope=uq[:, :, :MLA_NOPE].reshape(MLA_Q_LORA, -1).astype(BF16),
                uq_rope=uq[:, :, MLA_NOPE:].reshape(MLA_Q_LORA, -1).astype(BF16),
                ukt=jnp.transpose(w_uk, (1, 2, 0)).astype(BF16),
                uv=jnp.transpose(w_uv, (1, 0, 2)).astype(BF16), out=w_out.astype(BF16))


def _sig_tile2(z, j):
    return jnp.where(j == 2, _sigmoid(z), z)


def _rope64_epi(z, j, c, s):
    return _rope_tile(z, c, s, NSA_DH // 2)


def _rope_kv_epi(z, j, c, s):
    return jnp.concatenate([_rope_tile(z[:, :LANES], c, s, NSA_DH // 2), z[:, LANES:]], axis=1)


def _rope32_epi(z, j, c, s):
    return _rope_tile(z, c, s, MLA_ROPE // 2)


def _sigmoid_epi(z, j):
    return _sigmoid(z)


def _mla_in_epi(z, j, qn, kvn, c, s):
    a, bnd = MLA_Q_LORA, MLA_Q_LORA + MLA_KV_LORA
    return jnp.concatenate([_rms(z[:, :a], qn), _rms(z[:, a:bnd], kvn),
                            _rope_tile(z[:, bnd:], c, s, MLA_ROPE // 2)], axis=1)


def _const_aux(a):
    return (a, a.shape, lambda i, j: (0,) * a.ndim)


def _nsa_fresh(h, g, w, b, t, tabs):
    cos, sin = tabs[NSA_DH // 2]
    tm = min(1024, t)
    rope = _rope_aux(cos, sin, tm, t // tm)
    q = _mm(h, w['q'], g=g, epi=_rope64_epi, aux=rope, out_dtype=BF16, layout='slab', seq=t)
    kv = _mm(h, w['kv'], g=g, epi=_rope_kv_epi, aux=rope, layout='tiles', seq=t, tn=256)
    gates = _mm(h, w['g'], g=g, epi=_sigmoid_epi, seq=t)
    rows = [kv[i].reshape(b, t, 256) for i in range(3)]
    kc, vc = _nsa_compress(rows[0], w['pe'], w['w1'], w['w2p'])
    o = _nsa_attn(q, gates.reshape(b, t, LANES), kc, vc, rows[1], rows[2])
    h = _mm(o, w['out'], res=h, x_slab_seq=t)
    return h, rows


def _ml_fresh(h, g, w, b, t):
    z = _mm(h, w['main'], g=g, epi=_sig_tile2, layout='slab', seq=t)
    gc = _mm(h, w['gate'], g=g, seq=t).reshape(b, t, LANES)
    gr = jnp.swapaxes(gc[:, :, :2 * ML_HEADS], 1, 2)
    hs, c_f, n_f, m_f = _mlstm(z, gc, gr, w['gate_b'])
    return _mm(hs, w['out'], res=h, x_slab_seq=t), (c_f, n_f, m_f)


def _mla_fresh(h, g, w, b, t, tabs):
    cos, sin = tabs[MLA_ROPE // 2]
    tm = min(1024, t)
    rope = _rope_aux(cos, sin, tm, t // tm)
    lat = _mm(h, w['w_in'], g=g, epi=_mla_in_epi,
              aux=[_const_aux(w['q_norm']), _const_aux(w['kv_norm'])] + rope, seq=t)
    qn = _mm(lat, w['uq_nope'], x_cols=(MLA_Q_LORA, 0), out_dtype=BF16, seq=t)
    qr = _mm(lat, w['uq_rope'], x_cols=(MLA_Q_LORA, 0), epi=_rope32_epi, aux=rope, out_dtype=BF16, seq=t)
    ql = _headmm(qn, w['ukt'])
    ol = _mla_attn(ql, qr, lat, b, t)
    o = _headmm(ol, w['uv'])
    h = _mm(o, w['out'], res=h)
    new_lat = lat[:, MLA_Q_LORA:MLA_Q_LORA + MLA_KV_LORA + MLA_ROPE].reshape(b, t, -1)
    return h, new_lat


def _hg_fresh(h, g, w_in, lower, f_b, g_norm, w_out, b, t):
    z = _mm(h, w_in, g=g, layout='slab', seq=t)
    os_, s_f = _hgrn(z, lower, f_b, g_norm)
    return _mm(os_, w_out, res=h, x_slab_seq=t), s_f


PAGES_PER_STEP = 16


def _page_specs(width, n):
    return [pl.BlockSpec((None, width, PAGE), functools.partial(
        lambda i, s, pt, k: (pt[i, s * n + k], 0, 0), k=k)) for k in range(n)]


def _pages_feature_major(pool):
    nd = pool.ndim
    return jnp.transpose(pool, (0,) + tuple(range(2, nd)) + (1,)).reshape(pool.shape[0], -1, pool.shape[1])


def _stack8(x):
    return jnp.concatenate([x] * (NSA_HEADS // NSA_G), axis=0)


def _nsa_cmp_past_kernel(pt_ref, *refs, n_pg, t, start, nc):
    pages = refs[:n_pg]
    q_ref, pe_ref, w1_ref, w2p_ref, ocmp_ref, imp_ref, k_scr, v_scr = refs[n_pg:]
    s_i = pl.program_id(1)
    for k in range(n_pg):
        off = pl.multiple_of((s_i * n_pg + k) * PAGE, PAGE)
        rows = pages[k][...].T
        k_scr[pl.ds(off, PAGE), :] = rows[:, 0:LANES]
        v_scr[pl.ds(off, PAGE), :] = rows[:, LANES:2 * LANES]

    @pl.when(s_i == pl.num_programs(1) - 1)
    def _():
        n_seg = k_scr.shape[0] // CMP_STRIDE
        kc, vc = _compress_rows(k_scr, v_scr, n_seg, pe_ref, w1_ref, w2p_ref)
        q = q_ref[...]
        r_n = q.shape[0]
        cidx = lax.broadcasted_iota(jnp.int32, (r_n, n_seg), 1)
        qpos = start + lax.rem(lax.broadcasted_iota(jnp.int32, (r_n, n_seg), 0), t)
        mask = (cidx < nc) & (cidx * CMP_STRIDE + (CMP_LEN - 1) <= qpos)
        p = _softmax_tile(_dot_nt(q, kc) * (NSA_DH ** -0.5), mask)
        ocmp_ref[...] = _dot(p, vc)
        gt = NSA_G * t
        imp = p[0:gt]
        for hh in range(1, r_n // gt):
            imp = imp + p[hh * gt:(hh + 1) * gt]
        imp_ref[...] = imp


def _nsa_cmp_past(page_table, cache, q_ext, w, t, start):
    b, npg = page_table.shape
    n_pg = min(PAGES_PER_STEP, npg)
    r_n = q_ext.shape[1]
    n_seg = npg * PAGE // CMP_STRIDE
    nc = (start + t) // CMP_STRIDE - CMP_LEN // CMP_STRIDE + 1
    full = lambda a: pl.BlockSpec(a.shape, lambda i, s, pt: (0,) * a.ndim)
    gs = pltpu.PrefetchScalarGridSpec(
        num_scalar_prefetch=1, grid=(b, npg // n_pg),
        in_specs=_page_specs(256, n_pg) + [pl.BlockSpec((None, r_n, LANES), lambda i, s, pt: (i, 0, 0)),
                                           full(w['pe']), full(w['w1']), full(w['w2p'])],
        out_specs=[pl.BlockSpec((None, r_n, LANES), lambda i, s, pt: (i, 0, 0)),
                   pl.BlockSpec((None, NSA_G * t, n_seg), lambda i, s, pt: (i, 0, 0))],
        scratch_shapes=[pltpu.VMEM((npg * PAGE, LANES), F32), pltpu.VMEM((npg * PAGE, LANES), F32)])
    return pl.pallas_call(
        functools.partial(_nsa_cmp_past_kernel, n_pg=n_pg, t=t, start=start, nc=nc), grid_spec=gs,
        out_shape=[jax.ShapeDtypeStruct((b, r_n, LANES), F32),
                   jax.ShapeDtypeStruct((b, NSA_G * t, n_seg), F32)],
        compiler_params=_cparams('arbitrary', 'arbitrary'), name='nsa_cmp_past')(
            page_table, *([cache] * n_pg), q_ext, w['pe'], w['w1'], w['w2p']), nc


def _nsa_select_kernel(imp_ref, ovt_ref, pos_ref, sel_ref, sc_scr, rank_scr, *, ns, n_sel):
    shape = sc_scr.shape
    blk = lax.broadcasted_iota(jnp.int32, shape, 0)
    cur = pos_ref[...] // SLC_BLOCK
    forced = (blk == 0) | (blk == cur) | (blk == cur - 1)
    valid = blk <= cur
    sc = _dot_nt_f32(ovt_ref[...], imp_ref[...])
    sc_scr[...] = jnp.where(valid, sc + jnp.where(forced, FORCE_BONUS, 0.0), NEG)
    rank_scr[...] = jnp.zeros(shape, F32)

    def body(j, carry):
        sc_all = sc_scr[...]
        rj = sc_scr[pl.ds(j, 1), :]
        beats = (rj > sc_all) | ((rj == sc_all) & (blk > j))
        rank_scr[...] += jnp.where(beats, 1.0, 0.0)
        return carry

    lax.fori_loop(0, ns, body, 0)
    sel_ref[...] = jnp.where(valid & (rank_scr[...] < n_sel), 1.0, 0.0)


def _nsa_select_past(imp, nc, t, start):
    rows, n_seg = imp.shape
    ns = -(-(start + t) // SLC_BLOCK)
    nsp = -(-ns // 8) * 8
    ovt = _overlap_t(nc, ns, n_seg, nsp)
    pos = (start + jnp.arange(rows, dtype=jnp.int32) % t).reshape(1, rows)
    return pl.pallas_call(
        functools.partial(_nsa_select_kernel, ns=ns, n_sel=min(SLC_TOPN, ns)),
        out_shape=jax.ShapeDtypeStruct((nsp, rows), F32),
        scratch_shapes=[pltpu.VMEM((nsp, rows), F32), pltpu.VMEM((nsp, rows), F32)],
        compiler_params=pltpu.CompilerParams(vmem_limit_bytes=VMEM_LIMIT), name='nsa_select')(imp, ovt, pos)


def _nsa_slcwin_past_kernel(pt_ref, *refs, n_pg, t, wl):
    pages = refs[:n_pg]
    (msk_ref, q_ref, ns_ref, nw_ref, wb_ref, nf_ref, gt_ref, ocmp_ref, o_ref,
     m_ref, l_ref, acc_ref, oslc_scr, pad_scr) = refs[n_pg:]
    s_i = pl.program_id(1)
    scale = NSA_DH ** -0.5
    q = q_ref[...]
    r_n = q.shape[0]

    @pl.when(s_i == 0)
    def _():
        _online_init(m_ref, l_ref, acc_ref)

    s = jnp.concatenate([_dot(q, pg[0:LANES, :]) * scale for pg in pages], axis=1)
    v_t = jnp.concatenate([pg[LANES:2 * LANES, :] for pg in pages], axis=1)
    mask = _stack8(msk_ref[...].astype(F32)) > 0.5
    _online_update(s, mask, v_t, m_ref, l_ref, acc_ref, v_t=True)

    @pl.when(s_i == pl.num_programs(1) - 1)
    def _():
        col = lax.broadcasted_iota(jnp.int32, (r_n, PAGE), 1)
        tq = lax.rem(lax.broadcasted_iota(jnp.int32, (r_n, PAGE), 0), t)
        new_ok = (col < t) & (col <= tq)

        def padded(ref):
            pad_scr[...] = jnp.zeros(pad_scr.shape, F32)
            pad_scr[0:t, :] = ref[...]
            return pad_scr[...]

        rows = padded(ns_ref)
        _online_update(_dot_nt(q, rows[:, 0:LANES]) * scale, new_ok & (_stack8(nf_ref[...]) > 0.5),
                       rows[:, LANES:], m_ref, l_ref, acc_ref)
        oslc_scr[...] = _online_result(l_ref, acc_ref)

        _online_init(m_ref, l_ref, acc_ref)
        colw = lax.broadcasted_iota(jnp.int32, (r_n, wl), 1)
        tqw = lax.rem(lax.broadcasted_iota(jnp.int32, (r_n, wl), 0), t)
        _online_update(_dot(q, wb_ref[0:LANES, :]) * scale, colw > tqw + (wl - WINDOW),
                       wb_ref[LANES:2 * LANES, :], m_ref, l_ref, acc_ref, v_t=True)
        rows = padded(nw_ref)
        _online_update(_dot_nt(q, rows[:, 0:LANES]) * scale, new_ok, rows[:, LANES:], m_ref, l_ref, acc_ref)
        o_win = _online_result(l_ref, acc_ref)
        gt = gt_ref[...]
        o_ref[...] = gt[:, 0:1] * ocmp_ref[...] + gt[:, 1:2] * oslc_scr[...] + gt[:, 2:3] * o_win


def _nsa_slcwin_past(page_table, cache, key_mask, q_ext, new_slc, new_win, win_buf, new_flag, gates_r, ocmp, t):
    b, npg = page_table.shape
    n_pg = min(PAGES_PER_STEP, npg)
    r_n = q_ext.shape[1]
    gtn = NSA_G * t
    wl = win_buf.shape[2]
    per_b = lambda shp: pl.BlockSpec((None,) + shp, lambda i, s, pt: (i,) + (0,) * len(shp))
    gs = pltpu.PrefetchScalarGridSpec(
        num_scalar_prefetch=1, grid=(b, npg // n_pg),
        in_specs=_page_specs(256, n_pg) + [
            pl.BlockSpec((None, gtn, n_pg * PAGE), lambda i, s, pt: (i, 0, s)),
            per_b((r_n, LANES)), per_b((t, 256)), per_b((t, 256)), per_b((256, wl)),
            per_b((gtn, LANES)), per_b((r_n, LANES)), per_b((r_n, LANES))],
        out_specs=per_b((r_n, LANES)),
        scratch_shapes=[pltpu.VMEM((r_n, 1), F32), pltpu.VMEM((r_n, 1), F32), pltpu.VMEM((r_n, LANES), F32),
                        pltpu.VMEM((r_n, LANES), F32), pltpu.VMEM((PAGE, 256), F32)])
    return pl.pallas_call(
        functools.partial(_nsa_slcwin_past_kernel, n_pg=n_pg, t=t, wl=wl), grid_spec=gs,
        out_shape=jax.ShapeDtypeStruct((b, r_n, LANES), F32),
        compiler_params=_cparams('arbitrary', 'arbitrary'), name='nsa_slcwin_past')(
            page_table, *([cache] * n_pg), key_mask, q_ext, new_slc, new_win, win_buf, new_flag, gates_r, ocmp)


def _mla_past_kernel(pt_ref, *refs, n_pg, t):
    pages = refs[:n_pg]
    ql_ref, qr_ref, new_ref, o_ref, m_ref, l_ref, acc_ref, pad_scr = refs[n_pg:]
    s_i = pl.program_id(1)
    scale = (MLA_NOPE + MLA_ROPE) ** -0.5
    ql = ql_ref[...]
    qr = qr_ref[...]
    r_n = ql.shape[0]

    @pl.when(s_i == 0)
    def _():
        _online_init(m_ref, l_ref, acc_ref)

    def scores(rows):
        kc = rows[:, 0:MLA_KV_LORA].astype(BF16)
        kr = rows[:, MLA_KV_LORA:MLA_KV_LORA + MLA_ROPE]
        return (_dot_nt(ql, kc) + _dot_nt(qr, kr)) * scale, kc

    kc_t = [pg[0:MLA_KV_LORA, :].astype(BF16) for pg in pages]
    s = [(_dot(ql, kc) + _dot(qr, pg[MLA_KV_LORA:MLA_KV_LORA + MLA_ROPE, :])) * scale
         for kc, pg in zip(kc_t, pages)]
    _online_update(jnp.concatenate(s, axis=1), None, jnp.concatenate(kc_t, axis=1),
                   m_ref, l_ref, acc_ref, v_t=True)

    @pl.when(s_i == pl.num_programs(1) - 1)
    def _():
        pad_scr[...] = jnp.zeros(pad_scr.shape, F32)
        pad_scr[0:t, :] = new_ref[...]
        s, kc = scores(pad_scr[...])
        col = lax.broadcasted_iota(jnp.int32, (r_n, PAGE), 1)
        tq = lax.rem(lax.broadcasted_iota(jnp.int32, (r_n, PAGE), 0), t)
        _online_update(s, (col < t) & (col <= tq), kc, m_ref, l_ref, acc_ref)
        o_ref[...] = _online_result(l_ref, acc_ref).astype(o_ref.dtype)


def _mla_past_attn(page_table, cache, ql, qr, new_lat, t):
    b, npg = page_table.shape
    n_pg = min(PAGES_PER_STEP, npg)
    r_n = ql.shape[1]
    width = cache.shape[1]
    per_b = lambda shp: pl.BlockSpec((None,) + shp, lambda i, s, pt: (i,) + (0,) * len(shp))
    gs = pltpu.PrefetchScalarGridSpec(
        num_scalar_prefetch=1, grid=(b, npg // n_pg),
        in_specs=_page_specs(width, n_pg) + [per_b((r_n, MLA_KV_LORA)), per_b((r_n, MLA_ROPE)), per_b((t, width))],
        out_specs=per_b((r_n, MLA_KV_LORA)),
        scratch_shapes=[pltpu.VMEM((r_n, 1), F32), pltpu.VMEM((r_n, 1), F32), pltpu.VMEM((r_n, MLA_KV_LORA), F32),
                        pltpu.VMEM((PAGE, width), F32)])
    return pl.pallas_call(
        functools.partial(_mla_past_kernel, n_pg=n_pg, t=t), grid_spec=gs,
        out_shape=jax.ShapeDtypeStruct((b, r_n, MLA_KV_LORA), BF16),
        compiler_params=_cparams('arbitrary', 'arbitrary'), name='mla_past')(
            page_table, *([cache] * n_pg), ql, qr, new_lat)


def _to_slab(z, b, t):
    return jnp.transpose(z.reshape(b, t, -1, LANES), (0, 2, 1, 3))


def _from_slab(s, b, t):
    return jnp.transpose(s, (0, 2, 1, 3)).reshape(b * t, -1)


def _nsa_past(h, g, w, b, t, tabs, start, past, occ):
    assert t < CMP_STRIDE and start % SLC_BLOCK == 0
    cos, sin = tabs[NSA_DH // 2]
    m = b * t
    rope = _rope_aux(cos, sin, min(1024, m), None)
    q = _mm(h, w['q'], g=g, epi=_rope64_epi, aux=rope, out_dtype=BF16)
    kv = _mm(h, w['kv'], g=g, epi=_rope_kv_epi, aux=rope, layout='tiles', tn=256)
    gates = _mm(h, w['g'], g=g, epi=_sigmoid_epi)
    rows = [kv[i].reshape(b, t, 256) for i in range(3)]
    hpg = NSA_HEADS // NSA_G
    r_n = hpg * NSA_G * t
    q5 = jnp.transpose(q.reshape(b, t, NSA_G, hpg, NSA_DH), (0, 3, 2, 1, 4))
    eye = jnp.eye(NSA_G, dtype=q5.dtype)
    q_ext = (q5[:, :, :, :, None, :] * eye[None, None, :, None, :, None]).reshape(b, r_n, NSA_G * NSA_DH)
    pt = past['page_table']
    width = 2 * NSA_G * NSA_DH
    cmp_cache = _pages_feature_major(past['nsa_cmp'][occ])
    slc_cache = _pages_feature_major(past['nsa_slc'][occ])
    win_buf = past['nsa_win'][occ].reshape(b, -1, width)
    win_buf_t = _pages_feature_major(past['nsa_win'][occ])
    (ocmp, imp), nc = _nsa_cmp_past(pt, cmp_cache, q_ext, w, t, start)
    gtn = NSA_G * t
    sel = _nsa_select_past(imp.reshape(b * gtn, -1), nc, t, start)
    sel = jnp.transpose(sel).reshape(b, gtn, -1)
    n_past_blk = start // SLC_BLOCK
    key_mask = jnp.repeat(sel[:, :, :n_past_blk], SLC_BLOCK, axis=-1).astype(BF16)
    new_flag = jnp.broadcast_to(sel[:, :, n_past_blk:n_past_blk + 1], (b, gtn, LANES))
    g5 = jnp.transpose(gates[:, :3 * NSA_HEADS].reshape(b, t, 3, NSA_G, hpg), (0, 4, 3, 1, 2))
    gates_r = jnp.pad(g5.reshape(b, r_n, 3), ((0, 0), (0, 0), (0, LANES - 3)))
    o = _nsa_slcwin_past(pt, slc_cache, key_mask, q_ext, rows[1], rows[2], win_buf_t, new_flag, gates_r, ocmp, t)
    o5 = o.reshape(b, hpg, NSA_G, t, NSA_G, NSA_DH)
    o = jnp.stack([o5[:, :, gi, :, gi, :] for gi in range(NSA_G)], axis=2)
    o = jnp.transpose(o, (0, 3, 2, 1, 4)).reshape(m, NSA_HEADS * NSA_DH).astype(BF16)
    h = _mm(o, w['out'], res=h)
    win = jnp.concatenate([win_buf, rows[2]], axis=1)[:, -win_buf.shape[1]:]
    return h, rows, win


def _ml_past(h, g, w, b, t, past, occ):
    z = _to_slab(_mm(h, w['main'], g=g, epi=_sig_tile2), b, t)
    gc = _mm(h, w['gate'], g=g).reshape(b, t, LANES)
    gr = jnp.swapaxes(gc[:, :, :2 * ML_HEADS], 1, 2)
    hs, c_f, n_f, m_f = _mlstm(z, gc, gr, w['gate_b'],
                               state=(past['ml_C'][occ], past['ml_n'][occ], past['ml_m'][occ]))
    return _mm(_from_slab(hs, b, t), w['out'], res=h), (c_f, n_f, m_f)


def _mla_past(h, g, w, b, t, tabs, past, occ):
    cos, sin = tabs[MLA_ROPE // 2]
    m = b * t
    rope = _rope_aux(cos, sin, min(1024, m), None)
    lat = _mm(h, w['w_in'], g=g, epi=_mla_in_epi,
              aux=[_const_aux(w['q_norm']), _const_aux(w['kv_norm'])] + rope)
    qn = _mm(lat, w['uq_nope'], x_cols=(MLA_Q_LORA, 0), out_dtype=BF16)
    qr = _mm(lat, w['uq_rope'], x_cols=(MLA_Q_LORA, 0), epi=_rope32_epi, aux=rope, out_dtype=BF16)
    ql = _headmm(qn, w['ukt'])
    new_lat = lat[:, MLA_Q_LORA:MLA_Q_LORA + MLA_KV_LORA + MLA_ROPE].reshape(b, t, -1)
    hd = lambda a: jnp.transpose(a.reshape(b, t, MLA_HEADS, -1), (0, 2, 1, 3)).reshape(b, MLA_HEADS * t, -1)
    cache = _pages_feature_major(past['mla'][occ])
    ol = _mla_past_attn(past['page_table'], cache, hd(ql), hd(qr), new_lat, t)
    ol = jnp.transpose(ol.reshape(b, MLA_HEADS, t, -1), (0, 2, 1, 3)).reshape(m, -1)
    o = _headmm(ol, w['uv'])
    return _mm(o, w['out'], res=h), new_lat


def _hg_past(h, g, w_in, lower, f_b, g_norm, w_out, b, t, state):
    z = _to_slab(_mm(h, w_in, g=g), b, t)
    os_, s_f = _hgrn(z, lower, f_b, g_norm, state=state)
    return _mm(_from_slab(os_, b, t), w_out, res=h), s_f


def _prepare(prm):
    depth = prm['norm_mix'].shape[0]
    sm = jax.nn.softmax(prm['hg_lb_logits'].astype(F32), axis=0)
    lower = jnp.cumsum(sm, axis=0) - sm[0]
    w = dict(depth=depth, lower=lower)
    w['nsa'] = [_prep_nsa(prm['nsa_w_in'][o], prm['nsa_cmp_pe'][o], prm['nsa_cmp_w1'][o], prm['nsa_cmp_w2'][o],
                          prm['nsa_w_out'][o]) for o in range(prm['nsa_w_in'].shape[0])]
    w['ml'] = [_prep_ml(prm['ml_w_in'][o], prm['ml_gate_b'][o], prm['ml_w_out'][o])
               for o in range(prm['ml_w_in'].shape[0])]
    w['mla'] = [_prep_mla(prm['mla_w_in'][o], prm['mla_q_norm'][o], prm['mla_kv_norm'][o], prm['mla_w_uq'][o],
                          prm['mla_w_uk'][o], prm['mla_w_uv'][o], prm['mla_w_out'][o])
                for o in range(prm['mla_w_in'].shape[0])]
    w['hg'] = [dict(w_in=prm['hg_w_in'][o].astype(BF16), f_b=prm['hg_f_b'][o], norm=prm['hg_norm'][o],
                    out=prm['hg_w_out'][o].astype(BF16)) for o in range(prm['hg_w_in'].shape[0])]
    w['ffn'] = [dict(up=prm['ffn_w_up'][i].astype(BF16),
                     conv4=jnp.concatenate([prm['ffn_conv_w'][i], prm['ffn_conv_b'][i][None]], axis=0).astype(F32),
                     down=prm['ffn_w_down'][i].astype(BF16)) for i in range(depth)]
    w['ple'] = [dict(proj=prm['ple_w_proj'][i].astype(BF16), gate=prm['ple_w_gate'][i].astype(BF16))
                for i in range(depth)]
    return w


def _trunk(x, p, start, past, prm, w):
    b, t, d = x.shape
    depth = w['depth']
    pos = start + jnp.arange(t, dtype=jnp.int32)
    if past is None:
        tabs = {hf: _rope_tables(pos, hf) for hf in (NSA_DH // 2, MLA_ROPE // 2)}
    else:
        tabs = {hf: tuple(jnp.tile(a, (b, 1)) for a in _rope_tables(pos, hf)) for hf in (NSA_DH // 2, MLA_ROPE // 2)}
    new = {}
    h = x.reshape(b * t, d)
    for i in range(depth):
        kind, occ = i % 4, i // 4
        g = prm['norm_mix'][i]
        if kind == 0:
            if past is None:
                h, rows = _nsa_fresh(h, g, w['nsa'][occ], b, t, tabs)
                win = rows[2][:, -min(WINDOW, t):]
            else:
                h, rows, win = _nsa_past(h, g, w['nsa'][occ], b, t, tabs, start, past, occ)
            for name, r in zip(('nsa_cmp', 'nsa_slc'), rows[:2]):
                new.setdefault(name, []).append(r.reshape(b, t, 2, NSA_G, NSA_DH))
            new.setdefault('nsa_win', []).append(win.reshape(b, win.shape[1], 2, NSA_G, NSA_DH))
        elif kind == 1:
            if past is None:
                h, st = _ml_fresh(h, g, w['ml'][occ], b, t)
            else:
                h, st = _ml_past(h, g, w['ml'][occ], b, t, past, occ)
            for name, s in zip(('ml_C', 'ml_n', 'ml_m'), st):
                new.setdefault(name, []).append(s)
        elif kind == 2:
            if past is None:
                h, lat = _mla_fresh(h, g, w['mla'][occ], b, t, tabs)
            else:
                h, lat = _mla_past(h, g, w['mla'][occ], b, t, tabs, past, occ)
            new.setdefault('mla', []).append(lat)
        else:
            hw = w['hg'][occ]
            if past is None:
                h, s_f = _hg_fresh(h, g, hw['w_in'], w['lower'][i], hw['f_b'], hw['norm'], hw['out'], b, t)
            else:
                h, s_f = _hg_past(h, g, hw['w_in'], w['lower'][i], hw['f_b'], hw['norm'], hw['out'], b, t,
                                  past['hg_S'][occ])
            new.setdefault('hg_S', []).append(s_f)
        fw = w['ffn'][i]
        f = fw['down'].shape[0]
        pw = w['ple'][i]
        ple = (prm['norm_ple'][i], pw['gate'], p[i].reshape(b * t, -1), pw['proj'],
               prm['norm_final'] if i == depth - 1 else None)
        if past is None:
            h, sa, sg = _ffn(h, prm['norm_ffn'][i], fw['up'], fw['conv4'], fw['down'], t, ple)
            new.setdefault('ffn_conv', []).append(jnp.concatenate([sa, sg], axis=-1))
        else:
            buf = past['ffn_conv'][i]
            zero = jnp.zeros((b, t - 1, 2 * f), F32)
            p1 = jnp.concatenate([buf[:, 1:2], zero], axis=1).reshape(b * t, 2 * f)
            p2 = jnp.concatenate([buf, zero[:, 1:]], axis=1).reshape(b * t, 2 * f)
            h, ua, ug = _ffn(h, prm['norm_ffn'][i], fw['up'], fw['conv4'], fw['down'], t, ple, prev=(p1, p2))
            u = jnp.concatenate([ua, ug], axis=-1).reshape(b, t, 2 * f)
            new.setdefault('ffn_conv', []).append(u[:, -(CONV_W - 1):])
    return h.reshape(b, t, d), {k: jnp.stack(v) for k, v in new.items()}


def kernel(x_prompt, x_sample, cache_nsa_cmp_kv, cache_nsa_slc_kv, state_nsa_win_kv, cache_mla_latent,
           state_mlstm_C, state_mlstm_n, state_mlstm_m, state_hgrn_S, state_ffn_conv, page_table,
           p_prompt, p_sample, norm_mix, norm_ffn, norm_ple, norm_final, nsa_w_in, nsa_cmp_pe, nsa_cmp_w1,
           nsa_cmp_w2, nsa_w_out, ml_w_in, ml_gate_b, ml_w_out, mla_w_in, mla_q_norm, mla_kv_norm, mla_w_uq,
           mla_w_uk, mla_w_uv, mla_w_out, hg_w_in, hg_f_b, hg_lb_logits, hg_norm, hg_w_out, ffn_w_up,
           ffn_conv_w, ffn_conv_b, ffn_w_down, ple_w_proj, ple_w_gate):
    prm = {
        'norm_mix': norm_mix, 'norm_ffn': norm_ffn, 'norm_ple': norm_ple, 'norm_final': norm_final,
        'nsa_w_in': nsa_w_in, 'nsa_cmp_pe': nsa_cmp_pe, 'nsa_cmp_w1': nsa_cmp_w1, 'nsa_cmp_w2': nsa_cmp_w2,
        'nsa_w_out': nsa_w_out, 'ml_w_in': ml_w_in, 'ml_gate_b': ml_gate_b, 'ml_w_out': ml_w_out,
        'mla_w_in': mla_w_in, 'mla_q_norm': mla_q_norm, 'mla_kv_norm': mla_kv_norm, 'mla_w_uq': mla_w_uq,
        'mla_w_uk': mla_w_uk, 'mla_w_uv': mla_w_uv, 'mla_w_out': mla_w_out, 'hg_w_in': hg_w_in,
        'hg_f_b': hg_f_b, 'hg_lb_logits': hg_lb_logits, 'hg_norm': hg_norm, 'hg_w_out': hg_w_out,
        'ffn_w_up': ffn_w_up, 'ffn_conv_w': ffn_conv_w, 'ffn_conv_b': ffn_conv_b, 'ffn_w_down': ffn_w_down,
        'ple_w_proj': ple_w_proj, 'ple_w_gate': ple_w_gate,
    }
    past = {
        'nsa_cmp': cache_nsa_cmp_kv, 'nsa_slc': cache_nsa_slc_kv, 'nsa_win': state_nsa_win_kv,
        'mla': cache_mla_latent, 'ml_C': state_mlstm_C, 'ml_n': state_mlstm_n, 'ml_m': state_mlstm_m,
        'hg_S': state_hgrn_S, 'ffn_conv': state_ffn_conv, 'page_table': page_table,
    }
    w = _prepare(prm)
    past_len = page_table.shape[1] * PAGE
    y_p, sp = _trunk(x_prompt, p_prompt, 0, None, prm, w)
    y_s, ss = _trunk(x_sample, p_sample, past_len, past, prm, w)
    return (y_p, y_s,
            sp['nsa_cmp'], ss['nsa_cmp'], sp['nsa_slc'], ss['nsa_slc'], sp['nsa_win'], ss['nsa_win'],
            sp['mla'], ss['mla'], sp['ml_C'], ss['ml_C'], sp['ml_n'], ss['ml_n'], sp['ml_m'], ss['ml_m'],
            sp['hg_S'], ss['hg_S'], sp['ffn_conv'], ss['ffn_conv'])
```
